```python
import math
import jax, jax.numpy as jnp
from jax import lax
import numpy as np

D_MODEL = 1024
BATCH = 8
SEQ = 4096
DEPTH = 4

SSD_HEADS = 16
SSD_HEAD_DIM = D_MODEL // SSD_HEADS
SSD_INNER = SSD_HEADS * SSD_HEAD_DIM
SSD_GROUPS = 2
SSD_STATE = 128
SSD_CONV = 4
SSD_CHUNK = 128
SSD_XBC = SSD_INNER + 2 * SSD_GROUPS * SSD_STATE
CONF_WIDTH = D_MODEL
CONF_CONV_WIDTH = 31
NSA_HEADS = 16
NSA_KV_GROUPS = 4
NSA_HPG = NSA_HEADS // NSA_KV_GROUPS
NSA_HEAD_DIM = 64
NSA_WIDTH = NSA_HEADS * NSA_HEAD_DIM
KV_WIDTH = NSA_KV_GROUPS * NSA_HEAD_DIM
CMP_BLOCK = 32
CMP_STRIDE = 16
CMP_HIDDEN = 4 * NSA_HEAD_DIM
SEL_BLOCK = 64
SEL_TOPK = 16
WINDOW = 512
NSA_QBLOCK = 32
FORCE_BONUS = 1e6
NORM_EPS = 1e-6

N_EVEN = (DEPTH + 1) // 2
N_ODD = DEPTH // 2
E_IN = SSD_INNER + SSD_XBC + SSD_HEADS + 3 * CONF_WIDTH
O_IN = 2 * NSA_WIDTH + 6 * KV_WIDTH + 3 * NSA_HEADS

kernel_name = "hybrid_ssd_conformer_nsa_trunk"


def rmsnorm(x, g):
    xf = x.astype(jnp.float32)
    y = xf * lax.rsqrt(jnp.mean(xf * xf, axis=-1, keepdims=True) + NORM_EPS)
    return (y * g.astype(jnp.float32)).astype(x.dtype)


def causal_dwconv(x, w, b):
    k, c = w.shape
    y = lax.conv_general_dilated(x, w[:, None, :].astype(x.dtype), window_strides=(1,),
                                 padding=[(k - 1, 0)], dimension_numbers=('NWC', 'WIO', 'NWC'),
                                 feature_group_count=c)
    return y + b.astype(x.dtype)


def masked_softmax(s, mask):
    s = jnp.where(mask, s.astype(jnp.float32), -jnp.inf)
    m = jnp.max(s, axis=-1, keepdims=True)
    m = jnp.where(jnp.isfinite(m), m, 0.0)
    e = jnp.exp(s - m)
    return e / jnp.maximum(jnp.sum(e, axis=-1, keepdims=True), 1e-30)


def ssd_scan(x, dt, bm, cm, a_log, d_skip):
    f32 = jnp.float32
    bsz, s = x.shape[:2]
    nc = s // SSD_CHUNK
    r = SSD_HEADS // SSD_GROUPS
    xh = x.astype(f32).reshape(bsz, s, SSD_GROUPS, r, SSD_HEAD_DIM)
    xs = jnp.moveaxis(xh.reshape(bsz, nc, SSD_CHUNK, SSD_GROUPS, r, SSD_HEAD_DIM), 1, 0)
    dts = jnp.moveaxis(dt.reshape(bsz, nc, SSD_CHUNK, SSD_GROUPS, r), 1, 0)
    bs = jnp.moveaxis(bm.astype(f32).reshape(bsz, nc, SSD_CHUNK, SSD_GROUPS, SSD_STATE), 1, 0)
    cs = jnp.moveaxis(cm.astype(f32).reshape(bsz, nc, SSD_CHUNK, SSD_GROUPS, SSD_STATE), 1, 0)
    a = -jnp.exp(a_log.astype(f32)).reshape(SSD_GROUPS, r)
    causal = np.tril(np.ones((SSD_CHUNK, SSD_CHUNK), dtype=bool))[None, :, :, None, None]

    def step(state, inp):
        xc, dtc, bc, cc = inp
        cum = jnp.cumsum(dtc * a, axis=1)
        seg = cum[:, :, None] - cum[:, None, :]
        decay = jnp.exp(jnp.where(causal, seg, -jnp.inf))
        cb = jnp.einsum('btgn,bsgn->btsg', cc, bc)
        wts = cb[..., None] * decay * dtc[:, None]
        y_diag = jnp.einsum('btsgr,bsgrp->btgrp', wts, xc)
        y_off = jnp.einsum('btgn,bgrpn->btgrp', cc, state) * jnp.exp(cum)[..., None]
        to_end = jnp.exp(cum[:, -1:] - cum) * dtc
        new_state = state * jnp.exp(cum[:, -1])[..., None, None] + \
            jnp.einsum('bsgn,bsgr,bsgrp->bgrpn', bc, to_end, xc)
        return new_state, y_diag + y_off

    state0 = jnp.zeros((bsz, SSD_GROUPS, r, SSD_HEAD_DIM, SSD_STATE), f32)
    _, ys = lax.scan(step, state0, (xs, dts, bs, cs))
    ys = jnp.moveaxis(ys, 0, 1).reshape(bsz, s, SSD_GROUPS, r, SSD_HEAD_DIM)
    ys = ys + d_skip.astype(f32).reshape(SSD_GROUPS, r, 1) * xh
    return ys.reshape(bsz, s, SSD_INNER)


def even_layer(h, w_in, ssd_conv_w, ssd_conv_b, dt_bias, a_log, d_skip, ssd_norm,
               conf_conv_w, conf_conv_b, conf_ln_g, conf_ln_b, w_out):
    f32 = jnp.float32
    bsz, s, _ = h.shape
    proj = h @ w_in
    o1 = SSD_INNER
    o2 = o1 + SSD_XBC
    o3 = o2 + SSD_HEADS
    o4 = o3 + 2 * CONF_WIDTH
    z, xbc, dt_raw, glu_in, zc = jnp.split(proj, [o1, o2, o3, o4], axis=-1)
    xbc = jax.nn.silu(causal_dwconv(xbc, ssd_conv_w, ssd_conv_b))
    gn = SSD_GROUPS * SSD_STATE
    xs, bm, cm = jnp.split(xbc, [SSD_INNER, SSD_INNER + gn], axis=-1)
    dt = jax.nn.softplus(dt_raw.astype(f32) + dt_bias.astype(f32))
    y = ssd_scan(xs, dt, bm, cm, a_log, d_skip)
    yg = (y * jax.nn.silu(z.astype(f32))).reshape(bsz, s, SSD_GROUPS, SSD_INNER // SSD_GROUPS)
    yg = yg * lax.rsqrt(jnp.mean(yg * yg, axis=-1, keepdims=True) + NORM_EPS)
    y_a = (yg.reshape(bsz, s, SSD_INNER) * ssd_norm.astype(f32)).astype(h.dtype)
    ua, ub = jnp.split(glu_in, 2, axis=-1)
    u = causal_dwconv(ua * jax.nn.sigmoid(ub), conf_conv_w, conf_conv_b).astype(f32)
    mu = jnp.mean(u, axis=-1, keepdims=True)
    var = jnp.mean(jnp.square(u - mu), axis=-1, keepdims=True)
    un = (u - mu) * lax.rsqrt(var + NORM_EPS) * conf_ln_g.astype(f32) + conf_ln_b.astype(f32)
    y_b = (jax.nn.silu(un) * jax.nn.silu(zc.astype(f32))).astype(h.dtype)
    return jnp.concatenate([y_a, y_b], axis=-1) @ w_out


def compress_kv(t, pe, w1, w2):
    bsz, s = t.shape[:2]
    n_cmp = (s - CMP_BLOCK) // CMP_STRIDE + 1
    idx = np.arange(n_cmp)[:, None] * CMP_STRIDE + np.arange(CMP_BLOCK)[None, :]
    blk = t[:, idx] + pe[None, None, :, None, :].astype(t.dtype)
    blk = blk.transpose(0, 1, 3, 2, 4).reshape(bsz, n_cmp, NSA_KV_GROUPS, CMP_BLOCK * NSA_HEAD_DIM)
    return jax.nn.silu(blk @ w1) @ w2


def odd_layer(h, w_in, gate_bias, pe_k, w1_k, w2_k, pe_v, w1_v, w2_v, w_out):
    f32 = jnp.float32
    bsz, s, _ = h.shape
    G, R, Dh, Q = NSA_KV_GROUPS, NSA_HPG, NSA_HEAD_DIM, NSA_QBLOCK
    scale = NSA_HEAD_DIM ** -0.5
    proj = h @ w_in
    sizes = [NSA_WIDTH] + [KV_WIDTH] * 6 + [3 * NSA_HEADS]
    q, kc, vc, ks, vs, kw, vw, gl, z = jnp.split(proj, np.cumsum(sizes).tolist(), axis=-1)
    q = q.reshape(bsz, s, G, R, Dh)
    kc, vc, ks, vs, kw, vw = [t.reshape(bsz, s, G, Dh) for t in (kc, vc, ks, vs, kw, vw)]
    gates = jax.nn.sigmoid(gl.astype(f32) + gate_bias.astype(f32)).reshape(bsz, s, G, R, 3)
    k_cmp = compress_kv(kc, pe_k, w1_k, w2_k)
    v_cmp = compress_kv(vc, pe_v, w1_v, w2_v).astype(f32)
    n_cmp = k_cmp.shape[1]
    cmp_start = np.arange(n_cmp) * CMP_STRIDE
    cmp_end = cmp_start + CMP_BLOCK - 1
    n_sel = s // SEL_BLOCK
    topk = min(SEL_TOPK, n_sel)
    sel_start = np.arange(n_sel) * SEL_BLOCK
    overlap = jnp.asarray(((cmp_start[:, None] < sel_start[None, :] + SEL_BLOCK) &
                           (cmp_start[:, None] + CMP_BLOCK > sel_start[None, :])).astype(np.float32))
    ksb = ks.reshape(bsz, n_sel, SEL_BLOCK, G, Dh).transpose(0, 3, 1, 2, 4)
    vsb = vs.reshape(bsz, n_sel, SEL_BLOCK, G, Dh).transpose(0, 3, 1, 2, 4)
    b_idx = jnp.arange(bsz)[:, None, None, None]
    g_idx = jnp.arange(G)[None, :, None, None]
    pad = ((0, 0), (WINDOW, 0), (0, 0), (0, 0))
    kw_pad = jnp.pad(kw, pad)
    vw_pad = jnp.pad(vw, pad)

    def block_fn(blk):
        s0 = blk * Q
        qb = lax.dynamic_slice_in_dim(q, s0, Q, axis=1)
        gb = lax.dynamic_slice_in_dim(gates, s0, Q, axis=1)
        tpos = s0 + jnp.arange(Q)
        sc = jnp.einsum('bqgrd,bcgd->bgrqc', qb, k_cmp) * scale
        p_cmp = masked_softmax(sc, cmp_end[None, :] <= tpos[:, None])
        o_cmp = jnp.einsum('bgrqc,bcgd->bqgrd', p_cmp, v_cmp)
        imp = jnp.einsum('bgrqc,cj->bgqj', p_cmp, overlap)
        jb = jnp.arange(n_sel)[None, :]
        cur = (tpos // SEL_BLOCK)[:, None]
        valid = jb * SEL_BLOCK <= tpos[:, None]
        forced = (jb == 0) | (jb == cur) | (jb == cur - 1)
        imp = jnp.where(valid, imp + jnp.where(forced, FORCE_BONUS, 0.0), -jnp.inf)
        _, sel = lax.top_k(imp, topk)
        ksg = ksb[b_idx, g_idx, sel]
        vsg = vsb[b_idx, g_idx, sel].astype(f32)
        ss = jnp.einsum('bqgrd,bgqkld->bgrqkl', qb, ksg) * scale
        kpos = sel[..., None] * SEL_BLOCK + jnp.arange(SEL_BLOCK)
        smask = (kpos <= tpos[None, None, :, None, None])[:, :, None]
        p_sel = masked_softmax(ss.reshape(bsz, G, R, Q, topk * SEL_BLOCK),
                               smask.reshape(bsz, G, 1, Q, topk * SEL_BLOCK))
        o_sel = jnp.einsum('bgrqkl,bgqkld->bqgrd',
                           p_sel.reshape(bsz, G, R, Q, topk, SEL_BLOCK), vsg)
        kwb = lax.dynamic_slice_in_dim(kw_pad, s0, WINDOW + Q, axis=1)
        vwb = lax.dynamic_slice_in_dim(vw_pad, s0, WINDOW + Q, axis=1).astype(f32)
        kp = s0 - WINDOW + jnp.arange(WINDOW + Q)
        rel = tpos[:, None] - kp[None, :]
        wmask = (rel >= 0) & (rel < WINDOW) & (kp[None, :] >= 0)
        sw = jnp.einsum('bqgrd,bkgd->bgrqk', qb, kwb) * scale
        p_win = masked_softmax(sw, wmask)
        o_win = jnp.einsum('bgrqk,bkgd->bqgrd', p_win, vwb)
        return gb[..., 0:1] * o_cmp + gb[..., 1:2] * o_sel + gb[..., 2:3] * o_win

    outs = lax.map(block_fn, jnp.arange(s // Q))
    o = jnp.moveaxis(outs, 0, 1).reshape(bsz, s, NSA_WIDTH)
    y = (o * jax.nn.silu(z.astype(f32))).astype(h.dtype)
    return y @ w_out


def setup_inputs(seed: int = 0) -> dict:
    key = jax.random.key(seed)
    ks = jax.random.split(key, 25)
    f32 = jnp.float32

    def nrm(k, shape, sc):
        return jax.random.normal(k, shape, f32) * sc

    x = nrm(ks[0], (BATCH, SEQ, D_MODEL), 1.0)
    e_norm = 1.0 + nrm(ks[1], (N_EVEN, D_MODEL), 0.05)
    e_w_in = nrm(ks[2], (N_EVEN, D_MODEL, E_IN), D_MODEL ** -0.5)
    e_ssd_conv_w = nrm(ks[3], (N_EVEN, SSD_CONV, SSD_XBC), SSD_CONV ** -0.5)
    e_ssd_conv_b = nrm(ks[4], (N_EVEN, SSD_XBC), 0.02)
    dt0 = jnp.exp(jax.random.uniform(ks[5], (N_EVEN, SSD_HEADS), f32, math.log(1e-3), math.log(1e-1)))
    e_dt_bias = dt0 + jnp.log(-jnp.expm1(-dt0))
    e_a_log = jnp.log(jax.random.uniform(ks[6], (N_EVEN, SSD_HEADS), f32, 1.0, 16.0))
    e_d_skip = 1.0 + nrm(ks[7], (N_EVEN, SSD_HEADS), 0.1)
    e_ssd_norm = 1.0 + nrm(ks[8], (N_EVEN, SSD_INNER), 0.05)
    e_conf_conv_w = nrm(ks[9], (N_EVEN, CONF_CONV_WIDTH, CONF_WIDTH), CONF_CONV_WIDTH ** -0.5)
    e_conf_conv_b = nrm(ks[10], (N_EVEN, CONF_WIDTH), 0.02)
    e_conf_ln_g = 1.0 + nrm(ks[11], (N_EVEN, CONF_WIDTH), 0.05)
    e_conf_ln_b = nrm(ks[12], (N_EVEN, CONF_WIDTH), 0.02)
    e_w_out = nrm(ks[13], (N_EVEN, SSD_INNER + CONF_WIDTH, D_MODEL), (SSD_INNER + CONF_WIDTH) ** -0.5)
    o_norm = 1.0 + nrm(ks[14], (N_ODD, D_MODEL), 0.05)
    o_w_in = nrm(ks[15], (N_ODD, D_MODEL, O_IN), D_MODEL ** -0.5)
    o_gate_bias = nrm(ks[16], (N_ODD, 3 * NSA_HEADS), 0.1)
    o_cmp_pe_k = nrm(ks[17], (N_ODD, CMP_BLOCK, NSA_HEAD_DIM), 0.1)
    o_cmp_w1_k = nrm(ks[18], (N_ODD, CMP_BLOCK * NSA_HEAD_DIM, CMP_HIDDEN), (CMP_BLOCK * NSA_HEAD_DIM) ** -0.5)
    o_cmp_w2_k = nrm(ks[19], (N_ODD, CMP_HIDDEN, NSA_HEAD_DIM), CMP_HIDDEN ** -0.5)
    o_cmp_pe_v = nrm(ks[20], (N_ODD, CMP_BLOCK, NSA_HEAD_DIM), 0.1)
    o_cmp_w1_v = nrm(ks[21], (N_ODD, CMP_BLOCK * NSA_HEAD_DIM, CMP_HIDDEN), (CMP_BLOCK * NSA_HEAD_DIM) ** -0.5)
    o_cmp_w2_v = nrm(ks[22], (N_ODD, CMP_HIDDEN, NSA_HEAD_DIM), CMP_HIDDEN ** -0.5)
    o_w_out = nrm(ks[23], (N_ODD, NSA_WIDTH, D_MODEL), NSA_WIDTH ** -0.5)
    final_norm = 1.0 + nrm(ks[24], (D_MODEL,), 0.05)
    return {"x": x, "e_norm": e_norm, "e_w_in": e_w_in, "e_ssd_conv_w": e_ssd_conv_w,
            "e_ssd_conv_b": e_ssd_conv_b, "e_dt_bias": e_dt_bias, "e_a_log": e_a_log,
            "e_d_skip": e_d_skip, "e_ssd_norm": e_ssd_norm, "e_conf_conv_w": e_conf_conv_w,
            "e_conf_conv_b": e_conf_conv_b, "e_conf_ln_g": e_conf_ln_g, "e_conf_ln_b": e_conf_ln_b,
            "e_w_out": e_w_out, "o_norm": o_norm, "o_w_in": o_w_in, "o_gate_bias": o_gate_bias,
            "o_cmp_pe_k": o_cmp_pe_k, "o_cmp_w1_k": o_cmp_w1_k, "o_cmp_w2_k": o_cmp_w2_k,
            "o_cmp_pe_v": o_cmp_pe_v, "o_cmp_w1_v": o_cmp_w1_v, "o_cmp_w2_v": o_cmp_w2_v,
            "o_w_out": o_w_out, "final_norm": final_norm}


def reference(x, e_norm, e_w_in, e_ssd_conv_w, e_ssd_conv_b, e_dt_bias, e_a_log, e_d_skip,
              e_ssd_norm, e_conf_conv_w, e_conf_conv_b, e_conf_ln_g, e_conf_ln_b, e_w_out,
              o_norm, o_w_in, o_gate_bias, o_cmp_pe_k, o_cmp_w1_k, o_cmp_w2_k, o_cmp_pe_v,
              o_cmp_w1_v, o_cmp_w2_v, o_w_out, final_norm):
    h = x
    for layer in range(DEPTH):
        i = layer // 2
        if layer % 2 == 0:
            h = h + even_layer(rmsnorm(h, e_norm[i]), e_w_in[i], e_ssd_conv_w[i], e_ssd_conv_b[i],
                               e_dt_bias[i], e_a_log[i], e_d_skip[i], e_ssd_norm[i],
                               e_conf_conv_w[i], e_conf_conv_b[i], e_conf_ln_g[i], e_conf_ln_b[i],
                               e_w_out[i])
        else:
            h = h + odd_layer(rmsnorm(h, o_norm[i]), o_w_in[i], o_gate_bias[i], o_cmp_pe_k[i],
                              o_cmp_w1_k[i], o_cmp_w2_k[i], o_cmp_pe_v[i], o_cmp_w1_v[i],
                              o_cmp_w2_v[i], o_w_out[i])
    return rmsnorm(h, final_norm)
```

```python
import functools
import math

import numpy as np
import jax
import jax.numpy as jnp
from jax import lax
from jax.experimental import pallas as pl
from jax.experimental.pallas import tpu as pltpu

F32 = jnp.float32
BF16 = jnp.bfloat16

D_MODEL = 1024
SSD_HEADS = 16
SSD_HEAD_DIM = 64
SSD_INNER = 1024
SSD_GROUPS = 2
SSD_STATE = 128
SSD_CONV = 4
SSD_CHUNK = 128
SSD_XBC = SSD_INNER + 2 * SSD_GROUPS * SSD_STATE
CONF_WIDTH = 1024
CONF_CONV_WIDTH = 31
NSA_HEADS = 16
NSA_KV_GROUPS = 4
NSA_HPG = 4
NSA_HEAD_DIM = 64
NSA_WIDTH = 1024
KV_WIDTH = 256
CMP_BLOCK = 32
CMP_STRIDE = 16
CMP_HIDDEN = 256
SEL_BLOCK = 64
SEL_TOPK = 16
WINDOW = 512
FORCE_BONUS = 1e6
NORM_EPS = 1e-6

LANES = 128
SUBLANES = 8
VMEM_LIMIT_BYTES = 56 * 1024 * 1024

PROJ_ROWS = 256
CONF_ROWS = 256
CONF_ROW_CHUNK = 32
CONF_LANE_CHUNK = 512
ATT_TQ = 256
SEL_TK = 512
NEG_BIG = -1e30


def _dot(a, b, precision=None):
    return jnp.dot(a, b, preferred_element_type=F32, precision=precision)


def _dot_nt(a, b):
    return lax.dot_general(a, b, (((1,), (1,)), ((), ())), preferred_element_type=F32)


def _dot_tn(a, b):
    return lax.dot_general(a, b, (((0,), (0,)), ((), ())), preferred_element_type=F32)


def _silu(x):
    return x * jax.nn.sigmoid(x)


def _softplus(x):
    return jnp.maximum(x, 0.0) + jnp.log1p(jnp.exp(-jnp.abs(x)))


def _params(semantics):
    return pltpu.CompilerParams(dimension_semantics=semantics,
                                vmem_limit_bytes=VMEM_LIMIT_BYTES)


def _proj_kernel(*refs, nat_cols, t_rows):
    x_ref, g_ref = refs[0], refs[1]
    pos = 2
    w_ref = wt_ref = None
    if nat_cols:
        w_ref = refs[pos]
        pos += 1
    if t_rows:
        wt_ref = refs[pos]
        pos += 1
    out_refs = refs[pos:]
    x = x_ref[...]
    ms = jnp.mean(x * x, axis=-1, keepdims=True)
    xn = ((x * lax.rsqrt(ms + NORM_EPS)) * g_ref[...]).astype(BF16)
    k = 0
    for (a, b) in nat_cols:
        out_refs[k][...] = _dot(xn, w_ref[:, a:b]).astype(out_refs[k].dtype)
        k += 1
    for (a, b) in t_rows:
        out_refs[k][...] = _dot_nt(wt_ref[a:b, :], xn).astype(out_refs[k].dtype)
        k += 1


def _norm_proj(h2d, gain, w_nat, nat_spec, w_t, t_spec, tm):
    t_tokens, d = h2d.shape
    nat_cols, off = [], 0
    for width, _ in nat_spec:
        nat_cols.append((off, off + width))
        off += width
    t_rows, off = [], 0
    for rows, _ in t_spec:
        t_rows.append((off, off + rows))
        off += rows
    in_specs = [pl.BlockSpec((tm, d), lambda i: (i, 0)),
                pl.BlockSpec((1, d), lambda i: (0, 0))]
    args = [h2d, gain.reshape(1, d)]
    if nat_spec:
        in_specs.append(pl.BlockSpec(w_nat.shape, lambda i: (0, 0)))
        args.append(w_nat)
    if t_spec:
        in_specs.append(pl.BlockSpec(w_t.shape, lambda i: (0, 0)))
        args.append(w_t)
    out_shape, out_specs = [], []
    for width, dt in nat_spec:
        out_shape.append(jax.ShapeDtypeStruct((t_tokens, width), dt))
        out_specs.append(pl.BlockSpec((tm, width), lambda i: (i, 0)))
    for rows, dt in t_spec:
        out_shape.append(jax.ShapeDtypeStruct((rows, t_tokens), dt))
        out_specs.append(pl.BlockSpec((rows, tm), lambda i: (0, i)))
    return pl.pallas_call(
        functools.partial(_proj_kernel, nat_cols=tuple(nat_cols), t_rows=tuple(t_rows)),
        grid=(t_tokens // tm,),
        in_specs=in_specs,
        out_specs=out_specs,
        out_shape=out_shape,
        compiler_params=_params(("parallel",)),
        name="norm_proj",
    )(*args)


def _ssd_kernel(xbc_ref, z_ref, dt_ref, dtT_ref, cw_ref, cb_ref, dtb_ref, dtbT_ref,
                alog_ref, alogT_ref, dskip_ref, norm_ref, y_ref,
                state_sc, carry_sc, win_sc):
    L = SSD_CHUNK
    hp = LANES // SSD_HEAD_DIM
    gw = SSD_INNER // SSD_GROUPS
    heads_per_group = SSD_HEADS // SSD_GROUPS

    @pl.when(pl.program_id(1) == 0)
    def _():
        state_sc[...] = jnp.zeros_like(state_sc)
        carry_sc[...] = jnp.zeros_like(carry_sc)

    xraw = xbc_ref[0]
    win_sc[0:SUBLANES, :] = carry_sc[...]
    win_sc[SUBLANES:SUBLANES + L, :] = xraw
    carry_sc[...] = xraw[L - SUBLANES:L, :]
    conv = cw_ref[0:1, :] * win_sc[SUBLANES - 3:SUBLANES - 3 + L, :]
    for k in range(1, SSD_CONV):
        o = SUBLANES - (SSD_CONV - 1) + k
        conv = conv + cw_ref[k:k + 1, :] * win_sc[o:o + L, :]
    xact = _silu(conv + cb_ref[...])
    xs = xact[:, :SSD_INNER]
    bm = xact[:, SSD_INNER:SSD_INNER + SSD_GROUPS * SSD_STATE].astype(BF16)
    cm = xact[:, SSD_INNER + SSD_GROUPS * SSD_STATE:].astype(BF16)

    dt = _softplus(dt_ref[0] + dtb_ref[...])
    d_a = dt * (-jnp.exp(alog_ref[...]))
    rows = lax.broadcasted_iota(jnp.int32, (L, L), 0)
    cols = lax.broadcasted_iota(jnp.int32, (L, L), 1)
    causal = rows >= cols
    cum = _dot(causal.astype(F32), d_a, precision=lax.Precision.HIGHEST)
    dt_t = _softplus(dtT_ref[...] + dtbT_ref[...])
    d_a_t = dt_t * (-jnp.exp(alogT_ref[...]))
    cum_t = _dot(d_a_t, (rows <= cols).astype(F32), precision=lax.Precision.HIGHEST)
    ecum = jnp.exp(cum)
    cum_last = cum[L - 1:L, :]
    to_end = jnp.exp(cum_last - cum) * dt
    e_last = jnp.exp(cum_last)

    lane = lax.broadcasted_iota(jnp.int32, (L, LANES), 1)
    first_head = lane < SSD_HEAD_DIM
    lane1 = lax.broadcasted_iota(jnp.int32, (1, LANES), 1)
    first_head1 = lane1 < SSD_HEAD_DIM

    def per_head(arr, h0, mask):
        return jnp.where(mask, arr[:, h0:h0 + 1], arr[:, h0 + 1:h0 + 2])

    y_blocks, ecum_blocks, toend_blocks, elast_blocks = [], [], [], []
    for g in range(SSD_GROUPS):
        bm_g = bm[:, g * SSD_STATE:(g + 1) * SSD_STATE]
        cm_g = cm[:, g * SSD_STATE:(g + 1) * SSD_STATE]
        cb = _dot_nt(cm_g, bm_g)
        for j in range(heads_per_group // hp):
            h0 = g * heads_per_group + j * hp
            c0 = h0 * SSD_HEAD_DIM
            x_pair = xs[:, c0:c0 + LANES].astype(BF16)
            res = []
            for hh in range(hp):
                h = h0 + hh
                seg = cum[:, h:h + 1] - cum_t[h:h + 1, :]
                decay = jnp.exp(jnp.where(causal, seg, -jnp.inf))
                wts = (cb * decay) * dt_t[h:h + 1, :]
                res.append(_dot(wts.astype(BF16), x_pair))
            y_blocks.append(jnp.where(first_head, res[0], res[1]))
            ecum_blocks.append(per_head(ecum, h0, first_head))
            toend_blocks.append(per_head(to_end, h0, first_head))
            elast_blocks.append(per_head(e_last, h0, first_head1))
    y_diag = jnp.concatenate(y_blocks, axis=1)
    ecum_x = jnp.concatenate(ecum_blocks, axis=1)
    toend_x = jnp.concatenate(toend_blocks, axis=1)
    elast_x = jnp.concatenate(elast_blocks, axis=1)

    xw = (xs * toend_x).astype(BF16)
    y_off_blocks = []
    for g in range(SSD_GROUPS):
        bm_g = bm[:, g * SSD_STATE:(g + 1) * SSD_STATE]
        cm_g = cm[:, g * SSD_STATE:(g + 1) * SSD_STATE]
        st = state_sc[g]
        y_off_blocks.append(_dot(cm_g, st.astype(BF16)))
        state_sc[g] = st * elast_x[:, g * gw:(g + 1) * gw] + \
            _dot_tn(bm_g, xw[:, g * gw:(g + 1) * gw])
    y_off = jnp.concatenate(y_off_blocks, axis=1) * ecum_x

    y = y_diag + y_off + dskip_ref[...] * xs
    yg = y * _silu(z_ref[0])
    outs = []
    for g in range(SSD_GROUPS):
        yg_g = yg[:, g * gw:(g + 1) * gw]
        ms = jnp.mean(yg_g * yg_g, axis=-1, keepdims=True)
        outs.append(yg_g * lax.rsqrt(ms + NORM_EPS))
    y_ref[0] = (jnp.concatenate(outs, axis=1) * norm_ref[...]).astype(y_ref.dtype)


def _ssd(xbc, z, dt_nat, dt_t, conv_w, conv_b, dt_bias, a_log, d_skip, ssd_norm, bsz, s):
    nch = s // SSD_CHUNK
    L = SSD_CHUNK
    pad = LANES - SSD_HEADS
    dtb = jnp.pad(dt_bias, (0, pad)).reshape(1, LANES)
    alog = jnp.pad(a_log, (0, pad)).reshape(1, LANES)
    dskip_x = jnp.repeat(d_skip, SSD_HEAD_DIM).reshape(1, SSD_INNER)
    const = lambda shape: pl.BlockSpec(shape, lambda b, c: tuple(0 for _ in shape))
    return pl.pallas_call(
        _ssd_kernel,
        grid=(bsz, nch),
        in_specs=[
            pl.BlockSpec((1, L, SSD_XBC), lambda b, c: (b, c, 0)),
            pl.BlockSpec((1, L, SSD_INNER), lambda b, c: (b, c, 0)),
            pl.BlockSpec((1, L, LANES), lambda b, c: (b, c, 0)),
            pl.BlockSpec((SSD_HEADS, L), lambda b, c: (0, b * nch + c)),
            const((SSD_CONV, SSD_XBC)), const((1, SSD_XBC)),
            const((1, LANES)), const((SSD_HEADS, 1)),
            const((1, LANES)), const((SSD_HEADS, 1)),
            const((1, SSD_INNER)), const((1, SSD_INNER)),
        ],
        out_specs=pl.BlockSpec((1, L, SSD_INNER), lambda b, c: (b, c, 0)),
        out_shape=jax.ShapeDtypeStruct((bsz, s, SSD_INNER), BF16),
        scratch_shapes=[
            pltpu.VMEM((SSD_GROUPS, SSD_STATE, SSD_INNER // SSD_GROUPS), F32),
            pltpu.VMEM((SUBLANES, SSD_XBC), F32),
            pltpu.VMEM((SUBLANES + L, SSD_XBC), F32),
        ],
        compiler_params=_params(("parallel", "arbitrary")),
        name="ssd_scan",
    )(xbc.reshape(bsz, s, SSD_XBC), z.reshape(bsz, s, SSD_INNER),
      dt_nat.reshape(bsz, s, LANES), dt_t,
      conv_w, conv_b.reshape(1, SSD_XBC), dtb, dt_bias.reshape(SSD_HEADS, 1),
      alog, a_log.reshape(SSD_HEADS, 1), dskip_x, ssd_norm.reshape(1, SSD_INNER))


def _conf_kernel(glu_ref, zc_ref, w_ref, b_ref, lng_ref, lnb_ref, y_ref, buf_sc, u_sc):
    ts = glu_ref.shape[1]
    halo = buf_sc.shape[0] - ts
    first = halo - (CONF_CONV_WIDTH - 1)

    @pl.when(pl.program_id(1) == 0)
    def _():
        buf_sc[0:halo, :] = jnp.zeros((halo, CONF_WIDTH), F32)

    buf_sc[halo:halo + ts, :] = glu_ref[0, :, :CONF_WIDTH] * \
        jax.nn.sigmoid(glu_ref[0, :, CONF_WIDTH:])
    for r0 in range(0, ts, CONF_ROW_CHUNK):
        for c0 in range(0, CONF_WIDTH, CONF_LANE_CHUNK):
            cs = slice(c0, c0 + CONF_LANE_CHUNK)
            acc = w_ref[0:1, cs] * buf_sc[r0 + first:r0 + first + CONF_ROW_CHUNK, cs]
            for k in range(1, CONF_CONV_WIDTH):
                o = r0 + first + k
                acc = acc + w_ref[k:k + 1, cs] * buf_sc[o:o + CONF_ROW_CHUNK, cs]
            u_sc[r0:r0 + CONF_ROW_CHUNK, cs] = acc
    buf_sc[0:halo, :] = buf_sc[ts:ts + halo, :]
    u = u_sc[...] + b_ref[...]
    mu = jnp.mean(u, axis=-1, keepdims=True)
    var = jnp.mean(jnp.square(u - mu), axis=-1, keepdims=True)
    un = (u - mu) * lax.rsqrt(var + NORM_EPS) * lng_ref[...] + lnb_ref[...]
    y_ref[0] = (_silu(un) * _silu(zc_ref[0])).astype(y_ref.dtype)


def _conformer(glu, zc, conv_w, conv_b, ln_g, ln_b, bsz, s):
    ts = min(CONF_ROWS, s)
    halo = 32
    wpad = jnp.pad(conv_w, ((0, halo - CONF_CONV_WIDTH), (0, 0)))
    const = lambda shape: pl.BlockSpec(shape, lambda b, i: tuple(0 for _ in shape))
    return pl.pallas_call(
        _conf_kernel,
        grid=(bsz, s // ts),
        in_specs=[
            pl.BlockSpec((1, ts, 2 * CONF_WIDTH), lambda b, i: (b, i, 0)),
            pl.BlockSpec((1, ts, CONF_WIDTH), lambda b, i: (b, i, 0)),
            const((halo, CONF_WIDTH)), const((1, CONF_WIDTH)),
            const((1, CONF_WIDTH)), const((1, CONF_WIDTH)),
        ],
        out_specs=pl.BlockSpec((1, ts, CONF_WIDTH), lambda b, i: (b, i, 0)),
        out_shape=jax.ShapeDtypeStruct((bsz, s, CONF_WIDTH), BF16),
        scratch_shapes=[pltpu.VMEM((halo + ts, CONF_WIDTH), F32),
                        pltpu.VMEM((ts, CONF_WIDTH), F32)],
        compiler_params=_params(("parallel", "arbitrary")),
        name="conformer",
    )(glu.reshape(bsz, s, 2 * CONF_WIDTH), zc.reshape(bsz, s, CONF_WIDTH), wpad,
      conv_b.reshape(1, -1), ln_g.reshape(1, -1), ln_b.reshape(1, -1))


def _outproj_even_kernel(h_ref, a_ref, b_ref, wa_ref, wb_ref, o_ref):
    o_ref[...] = h_ref[...] + (_dot(a_ref[...], wa_ref[...]) + _dot(b_ref[...], wb_ref[...]))


def _outproj_even(h2d, ya, yb, w_out, tm):
    t_tokens, d = h2d.shape
    wa = w_out[:SSD_INNER].astype(BF16)
    wb = w_out[SSD_INNER:].astype(BF16)
    row = lambda w: pl.BlockSpec((tm, w), lambda i: (i, 0))
    return pl.pallas_call(
        _outproj_even_kernel,
        grid=(t_tokens // tm,),
        in_specs=[row(d), row(SSD_INNER), row(CONF_WIDTH),
                  pl.BlockSpec(wa.shape, lambda i: (0, 0)),
                  pl.BlockSpec(wb.shape, lambda i: (0, 0))],
        out_specs=row(d),
        out_shape=jax.ShapeDtypeStruct((t_tokens, d), F32),
        compiler_params=_params(("parallel",)),
        name="outproj_even",
    )(h2d, ya, yb, wa, wb)


def _outproj_odd_kernel(h_ref, oc_ref, os_ref, ow_ref, zt_ref, w_ref, g_ref, o_ref, *, final):
    o = (oc_ref[...] + os_ref[...]) + ow_ref[...]
    y = (o * _silu(zt_ref[...])).astype(BF16)
    out = h_ref[...] + _dot_tn(y, w_ref[...])
    if final:
        ms = jnp.mean(out * out, axis=-1, keepdims=True)
        out = (out * lax.rsqrt(ms + NORM_EPS)) * g_ref[...]
    o_ref[...] = out


def _outproj_odd(h2d, oc, osel, ow, z_t, w_out, final_gain, final, tm):
    t_tokens, d = h2d.shape
    w = w_out.astype(BF16)
    col = pl.BlockSpec((NSA_WIDTH, tm), lambda i: (0, i))
    return pl.pallas_call(
        functools.partial(_outproj_odd_kernel, final=final),
        grid=(t_tokens // tm,),
        in_specs=[pl.BlockSpec((tm, d), lambda i: (i, 0)), col, col, col, col,
                  pl.BlockSpec(w.shape, lambda i: (0, 0)),
                  pl.BlockSpec((1, d), lambda i: (0, 0))],
        out_specs=pl.BlockSpec((tm, d), lambda i: (i, 0)),
        out_shape=jax.ShapeDtypeStruct((t_tokens, d), F32),
        compiler_params=_params(("parallel",)),
        name="outproj_odd",
    )(h2d, oc, osel, ow, z_t, w, final_gain.reshape(1, d))


def _compress_kernel(xk_ref, xv_ref, pek_ref, pev_ref, w1k_ref, w1v_ref, w2k_ref, w2vt_ref,
                     kc_ref, vct_ref):
    half = CMP_STRIDE * NSA_HEAD_DIM

    def hidden(x_ref, pe_ref, w1_ref):
        x = x_ref[0, 0]
        n = x.shape[0]
        lo = (x + pe_ref[0:1, :]).astype(BF16)
        hi = (x + pe_ref[1:2, :]).astype(BF16)
        h = _dot(lo, w1_ref[0:half, :]) + pltpu.roll(_dot(hi, w1_ref[half:2 * half, :]), n - 1, 0)
        return _silu(h).astype(BF16)

    kc_ref[0, 0] = _dot(hidden(xk_ref, pek_ref, w1k_ref), w2k_ref[...]).astype(kc_ref.dtype)
    vct_ref[0, 0] = _dot_nt(w2vt_ref[...], hidden(xv_ref, pev_ref, w1v_ref)).astype(vct_ref.dtype)


def _compress(kc, vc, pe_k, w1_k, w2_k, pe_v, w1_v, w2_v, bsz, s):
    g, d = NSA_KV_GROUPS, NSA_HEAD_DIM
    nh = s // CMP_STRIDE
    half = CMP_STRIDE * d

    def regroup(t):
        return t.reshape(bsz, nh, CMP_STRIDE, g, d).transpose(0, 3, 1, 2, 4).reshape(bsz, g, nh, half)

    const = lambda shape: pl.BlockSpec(shape, lambda b, gi: tuple(0 for _ in shape))
    blk = pl.BlockSpec((1, 1, nh, half), lambda b, gi: (b, gi, 0, 0))
    return pl.pallas_call(
        _compress_kernel,
        grid=(bsz, g),
        in_specs=[blk, blk, const((2, half)), const((2, half)),
                  const((2 * half, CMP_HIDDEN)), const((2 * half, CMP_HIDDEN)),
                  const((CMP_HIDDEN, d)), const((d, CMP_HIDDEN))],
        out_specs=[pl.BlockSpec((1, 1, nh, d), lambda b, gi: (b, gi, 0, 0)),
                   pl.BlockSpec((1, 1, d, nh), lambda b, gi: (b, gi, 0, 0))],
        out_shape=[jax.ShapeDtypeStruct((bsz, g, nh, d), BF16),
                   jax.ShapeDtypeStruct((bsz, g, d, nh), BF16)],
        compiler_params=_params(("parallel", "parallel")),
        name="compress_kv",
    )(regroup(kc), regroup(vc), pe_k.reshape(2, half), pe_v.reshape(2, half),
      w1_k.astype(BF16), w1_v.astype(BF16), w2_k.astype(BF16), w2_v.T.astype(BF16))


def _selection_bias(imp, t0, topk):
    n_sel, tq = imp.shape
    jidx = lax.broadcasted_iota(jnp.int32, (n_sel, tq), 0)
    t = t0 + lax.broadcasted_iota(jnp.int32, (n_sel, tq), 1)
    cur = lax.shift_right_logical(t, int(math.log2(SEL_BLOCK)))
    valid = jidx * SEL_BLOCK <= t
    forced = (jidx == 0) | (jidx == cur) | (jidx == cur - 1)
    v = jnp.where(valid, imp + jnp.where(forced, FORCE_BONUS, 0.0), -jnp.inf)
    cnt = jnp.zeros((n_sel, tq), jnp.int32)
    for i in range(n_sel):
        row = v[i:i + 1, :]
        ge = jnp.where(row >= v, 1, 0)
        gt = jnp.where(row > v, 1, 0)
        cnt = cnt + jnp.where(jidx > i, ge, gt)
    return jnp.where(valid & (cnt < topk), 0.0, -jnp.inf)


def _cmp_attn_kernel(qt_ref, kc_ref, vct_ref, ovt_ref, gl_ref, gb_ref, o_ref, selb_ref,
                     *, n_cmp, topk):
    tq = qt_ref.shape[2]
    nc = kc_ref.shape[2]
    t0 = pl.program_id(2) * tq
    cidx = lax.broadcasted_iota(jnp.int32, (nc, tq), 0)
    t = t0 + lax.broadcasted_iota(jnp.int32, (nc, tq), 1)
    cmask = (cidx * CMP_STRIDE + (CMP_BLOCK - 1) <= t) & (cidx < n_cmp)
    kc = kc_ref[0, 0]
    vct = vct_ref[0, 0]
    gates = jax.nn.sigmoid(gl_ref[0] + gb_ref[0])
    imp = jnp.zeros((ovt_ref.shape[0], tq), F32)
    for r in range(NSA_HPG):
        s = jnp.where(cmask, _dot(kc, qt_ref[r]), -jnp.inf)
        m = jnp.max(s, axis=0, keepdims=True)
        m = jnp.where(m > -jnp.inf, m, 0.0)
        e = jnp.exp(s - m)
        p = e * (1.0 / jnp.maximum(jnp.sum(e, axis=0, keepdims=True), 1e-30))
        pb = p.astype(BF16)
        o_ref[r] = _dot(vct, pb) * gates[r:r + 1, :]
        imp = imp + _dot(ovt_ref[...], pb)
    selb_ref[0, 0] = _selection_bias(imp, t0, topk)


def _gate_specs(nq):
    def spec(branch):
        return (pl.BlockSpec((1, SUBLANES, ATT_TQ), lambda b, g, i: (g * 3 + branch, 0, b * nq + i)),
                pl.BlockSpec((1, SUBLANES, 1), lambda b, g, i: (g * 3 + branch, 0, 0)))
    return spec


def _cmp_attn(q_t, kcmp, vcmp_t, gl_t, gbias, bsz, s):
    g, d = NSA_KV_GROUPS, NSA_HEAD_DIM
    tq = ATT_TQ
    nq = s // tq
    nc = s // CMP_STRIDE
    n_cmp = (s - CMP_BLOCK) // CMP_STRIDE + 1
    n_sel = s // SEL_BLOCK
    topk = min(SEL_TOPK, n_sel)
    cmp_start = np.arange(nc) * CMP_STRIDE
    sel_start = np.arange(n_sel) * SEL_BLOCK
    ov = ((cmp_start[:, None] < sel_start[None, :] + SEL_BLOCK) &
          (cmp_start[:, None] + CMP_BLOCK > sel_start[None, :]) &
          (np.arange(nc)[:, None] < n_cmp))
    ovt = jnp.asarray(ov.T.astype(np.float32), dtype=BF16)
    gl_spec, gb_spec = _gate_specs(nq)(0)
    t_tokens = bsz * s
    return pl.pallas_call(
        functools.partial(_cmp_attn_kernel, n_cmp=n_cmp, topk=topk),
        grid=(bsz, g, nq),
        in_specs=[
            pl.BlockSpec((NSA_HPG, d, tq), lambda b, gi, i: (gi, 0, b * nq + i)),
            pl.BlockSpec((1, 1, nc, d), lambda b, gi, i: (b, gi, 0, 0)),
            pl.BlockSpec((1, 1, d, nc), lambda b, gi, i: (b, gi, 0, 0)),
            pl.BlockSpec((n_sel, nc), lambda b, gi, i: (0, 0)),
            gl_spec, gb_spec,
        ],
        out_specs=[pl.BlockSpec((NSA_HPG, d, tq), lambda b, gi, i: (gi, 0, b * nq + i)),
                   pl.BlockSpec((1, 1, n_sel, tq), lambda b, gi, i: (b, gi, 0, i))],
        out_shape=[jax.ShapeDtypeStruct((NSA_HEADS, d, t_tokens), F32),
                   jax.ShapeDtypeStruct((bsz, g, n_sel, s), F32)],
        compiler_params=_params(("parallel", "parallel", "parallel")),
        name="cmp_attn_select",
    )(q_t, kcmp, vcmp_t, ovt, gl_t, gbias)


def _online_update(s, v_t, m, l, acc):
    m_new = jnp.maximum(m, jnp.max(s, axis=0, keepdims=True))
    alpha = jnp.exp(m - m_new)
    p = jnp.exp(s - m_new)
    l_new = alpha * l + jnp.sum(p, axis=0, keepdims=True)
    acc_new = alpha * acc + _dot(v_t, p.astype(BF16))
    return m_new, l_new, acc_new


def _sel_attn_kernel(qt_ref, k_ref, vt_ref, selb_ref, gl_ref, gb_ref, o_ref,
                     m_sc, l_sc, acc_sc, *, tk):
    tq = qt_ref.shape[2]
    nb = tk // SEL_BLOCK
    i = pl.program_id(2)
    t0 = i * tq
    last = (i * tq) // tk
    m_sc[...] = jnp.full(m_sc.shape, NEG_BIG, F32)
    l_sc[...] = jnp.zeros(l_sc.shape, F32)
    acc_sc[...] = jnp.zeros(acc_sc.shape, F32)

    def tile(j, diagonal):
        k0 = pl.multiple_of(j * tk, tk)
        k = k_ref[0, pl.ds(k0, tk), :]
        v_t = vt_ref[0, :, pl.ds(k0, tk)]
        b8 = selb_ref[0, 0, pl.ds(pl.multiple_of(j * nb, nb), nb), :]
        bias = jnp.concatenate(
            [jnp.broadcast_to(b8[bb:bb + 1, :], (SEL_BLOCK, tq)) for bb in range(nb)], axis=0)
        if diagonal:
            kpos = k0 + lax.broadcasted_iota(jnp.int32, (tk, tq), 0)
            t = t0 + lax.broadcasted_iota(jnp.int32, (tk, tq), 1)
            bias = jnp.where(kpos <= t, bias, -jnp.inf)
        for r in range(NSA_HPG):
            s = _dot(k, qt_ref[r]) + bias
            m, l, acc = _online_update(s, v_t, m_sc[r], l_sc[r], acc_sc[r])
            m_sc[r] = m
            l_sc[r] = l
            acc_sc[r] = acc

    def body(j, carry):
        tile(j, False)
        return carry

    lax.fori_loop(0, last, body, 0)
    tile(last, True)
    gates = jax.nn.sigmoid(gl_ref[0] + gb_ref[0])
    for r in range(NSA_HPG):
        o_ref[r] = acc_sc[r] * (1.0 / l_sc[r]) * gates[r:r + 1, :]


def _sel_attn(q_t, ks_g, vs_t, selb, gl_t, gbias, bsz, s):
    g, d = NSA_KV_GROUPS, NSA_HEAD_DIM
    tq = ATT_TQ
    tk = min(SEL_TK, s)
    nq = s // tq
    n_sel = s // SEL_BLOCK
    gl_spec, gb_spec = _gate_specs(nq)(1)
    return pl.pallas_call(
        functools.partial(_sel_attn_kernel, tk=tk),
        grid=(bsz, g, nq),
        in_specs=[
            pl.BlockSpec((NSA_HPG, d, tq), lambda b, gi, i: (gi, 0, b * nq + i)),
            pl.BlockSpec((1, s, d), lambda b, gi, i: (gi, b, 0)),
            pl.BlockSpec((1, d, s), lambda b, gi, i: (gi, 0, b)),
            pl.BlockSpec((1, 1, n_sel, tq), lambda b, gi, i: (b, gi, 0, i)),
            gl_spec, gb_spec,
        ],
        out_specs=pl.BlockSpec((NSA_HPG, d, tq), lambda b, gi, i: (gi, 0, b * nq + i)),
        out_shape=jax.ShapeDtypeStruct((NSA_HEADS, d, bsz * s), F32),
        scratch_shapes=[pltpu.VMEM((NSA_HPG, 1, tq), F32),
                        pltpu.VMEM((NSA_HPG, 1, tq), F32),
                        pltpu.VMEM((NSA_HPG, d, tq), F32)],
        compiler_params=_params(("parallel", "parallel", "arbitrary")),
        name="sel_attn",
    )(q_t, ks_g, vs_t, selb, gl_t, gbias)


def _win_attn_kernel(*refs, nblk):
    qt_ref = refs[0]
    k_refs = refs[1:1 + nblk]
    v_refs = refs[1 + nblk:1 + 2 * nblk]
    gl_ref, gb_ref, o_ref = refs[1 + 2 * nblk:]
    tq = qt_ref.shape[2]
    i = pl.program_id(2)
    t = i * tq + lax.broadcasted_iota(jnp.int32, (tq, tq), 1)
    krow = lax.broadcasted_iota(jnp.int32, (tq, tq), 0)
    masks = []
    for jj in range(nblk):
        kp = (i - (nblk - 1) + jj) * tq + krow
        rel = t - kp
        masks.append((rel >= 0) & (rel < WINDOW) & (kp >= 0))
    gates = jax.nn.sigmoid(gl_ref[0] + gb_ref[0])
    for r in range(NSA_HPG):
        q = qt_ref[r]
        m = jnp.full((1, tq), NEG_BIG, F32)
        l = jnp.zeros((1, tq), F32)
        acc = jnp.zeros((NSA_HEAD_DIM, tq), F32)
        for jj in reversed(range(nblk)):
            s = jnp.where(masks[jj], _dot(k_refs[jj][0], q), -jnp.inf)
            m, l, acc = _online_update(s, v_refs[jj][0], m, l, acc)
        o_ref[r] = acc * (1.0 / l) * gates[r:r + 1, :]


def _win_attn(q_t, kw_g, vw_t, gl_t, gbias, bsz, s):
    g, d = NSA_KV_GROUPS, NSA_HEAD_DIM
    tq = ATT_TQ
    nq = s // tq
    nblk = WINDOW // tq + 1
    gl_spec, gb_spec = _gate_specs(nq)(2)

    def k_spec(jj):
        return pl.BlockSpec((1, tq, d), lambda b, gi, i: (gi, b * nq + jnp.maximum(i - (nblk - 1) + jj, 0), 0))

    def v_spec(jj):
        return pl.BlockSpec((1, d, tq), lambda b, gi, i: (gi, 0, b * nq + jnp.maximum(i - (nblk - 1) + jj, 0)))

    return pl.pallas_call(
        functools.partial(_win_attn_kernel, nblk=nblk),
        grid=(bsz, g, nq),
        in_specs=[pl.BlockSpec((NSA_HPG, d, tq), lambda b, gi, i: (gi, 0, b * nq + i))]
        + [k_spec(jj) for jj in range(nblk)] + [v_spec(jj) for jj in range(nblk)]
        + [gl_spec, gb_spec],
        out_specs=pl.BlockSpec((NSA_HPG, d, tq), lambda b, gi, i: (gi, 0, b * nq + i)),
        out_shape=jax.ShapeDtypeStruct((NSA_HEADS, d, bsz * s), F32),
        compiler_params=_params(("parallel", "parallel", "parallel")),
        name="win_attn",
    )(q_t, *([kw_g] * nblk), *([vw_t] * nblk), gl_t, gbias)


def _even_layer(h2d, bsz, s, gain, w_in, ssd_conv_w, ssd_conv_b, dt_bias, a_log, d_skip, ssd_norm,
                conf_conv_w, conf_conv_b, conf_ln_g, conf_ln_b, w_out):
    o1 = SSD_INNER
    o2 = o1 + SSD_XBC
    o3 = o2 + SSD_HEADS
    o4 = o3 + 2 * CONF_WIDTH
    w_dt = w_in[:, o2:o3]
    w_nat = jnp.concatenate(
        [w_in[:, :o2], w_in[:, o3:], jnp.pad(w_dt, ((0, 0), (0, LANES - SSD_HEADS)))], axis=1).astype(BF16)
    nat_spec = [(SSD_INNER, F32), (SSD_XBC, F32), (2 * CONF_WIDTH, F32), (CONF_WIDTH, F32), (LANES, F32)]
    tm = min(PROJ_ROWS, s)
    z, xbc, glu, zc, dt_nat, dt_t = _norm_proj(
        h2d, gain, w_nat, nat_spec, w_dt.T.astype(BF16), [(SSD_HEADS, F32)], tm)
    y_a = _ssd(xbc, z, dt_nat, dt_t, ssd_conv_w, ssd_conv_b, dt_bias, a_log, d_skip, ssd_norm, bsz, s)
    y_b = _conformer(glu, zc, conf_conv_w, conf_conv_b, conf_ln_g, conf_ln_b, bsz, s)
    return _outproj_even(h2d, y_a.reshape(bsz * s, SSD_INNER), y_b.reshape(bsz * s, CONF_WIDTH), w_out, tm)


def _odd_layer(h2d, bsz, s, gain, w_in, gate_bias, pe_k, w1_k, w2_k, pe_v, w1_v, w2_v, w_out,
               final_gain, final):
    g, r, d = NSA_KV_GROUPS, NSA_HPG, NSA_HEAD_DIM
    t_tokens = bsz * s
    sizes = [NSA_WIDTH] + [KV_WIDTH] * 6 + [3 * NSA_HEADS, NSA_WIDTH]
    offs = np.cumsum([0] + sizes)
    col = lambda k: w_in[:, offs[k]:offs[k + 1]]
    w_q, w_kc, w_vc, w_ks, w_vs, w_kw, w_vw, w_gl, w_z = [col(k) for k in range(9)]
    perm = np.array([[[(gi * r + ri) * 3 + br for ri in range(r)] for br in range(3)] for gi in range(g)])
    w_gl_g = jnp.pad(w_gl[:, perm.reshape(-1)].reshape(D_MODEL, g * 3, r),
                     ((0, 0), (0, 0), (0, SUBLANES - r))).reshape(D_MODEL, g * 3 * SUBLANES)
    gbias = jnp.pad(gate_bias[perm.reshape(-1)].reshape(g * 3, r), ((0, 0), (0, SUBLANES - r)))
    gbias = gbias.reshape(g * 3, SUBLANES, 1)
    scale = NSA_HEAD_DIM ** -0.5
    w_nat = jnp.concatenate([w_kc, w_vc, w_ks, w_kw], axis=1).astype(BF16)
    w_t = jnp.concatenate([w_q * scale, w_vs, w_vw, w_gl_g, w_z], axis=1).T.astype(BF16)
    nat_spec = [(KV_WIDTH, F32), (KV_WIDTH, F32), (KV_WIDTH, BF16), (KV_WIDTH, BF16)]
    t_spec = [(NSA_WIDTH, BF16), (KV_WIDTH, BF16), (KV_WIDTH, BF16), (g * 3 * SUBLANES, F32), (NSA_WIDTH, F32)]
    tm = min(PROJ_ROWS, s)
    kc, vc, ks, kw, q_t, vs_t, vw_t, gl_t, z_t = _norm_proj(h2d, gain, w_nat, nat_spec, w_t, t_spec, tm)

    q_t = q_t.reshape(NSA_HEADS, d, t_tokens)
    gl_t = gl_t.reshape(g * 3, SUBLANES, t_tokens)
    group_major = lambda t: t.reshape(t_tokens, g, d).transpose(1, 0, 2)
    ks_g, kw_g = group_major(ks), group_major(kw)
    vs_t = vs_t.reshape(g, d, t_tokens)
    vw_t = vw_t.reshape(g, d, t_tokens)

    kcmp, vcmp_t = _compress(kc, vc, pe_k, w1_k, w2_k, pe_v, w1_v, w2_v, bsz, s)
    o_cmp, selb = _cmp_attn(q_t, kcmp, vcmp_t, gl_t, gbias, bsz, s)
    o_sel = _sel_attn(q_t, ks_g, vs_t, selb, gl_t, gbias, bsz, s)
    o_win = _win_attn(q_t, kw_g, vw_t, gl_t, gbias, bsz, s)
    flat = lambda o: o.reshape(NSA_WIDTH, t_tokens)
    return _outproj_odd(h2d, flat(o_cmp), flat(o_sel), flat(o_win), z_t, w_out, final_gain, final, tm)


def kernel(x, e_norm, e_w_in, e_ssd_conv_w, e_ssd_conv_b, e_dt_bias, e_a_log, e_d_skip, e_ssd_norm,
           e_conf_conv_w, e_conf_conv_b, e_conf_ln_g, e_conf_ln_b, e_w_out, o_norm, o_w_in, o_gate_bias,
           o_cmp_pe_k, o_cmp_w1_k, o_cmp_w2_k, o_cmp_pe_v, o_cmp_w1_v, o_cmp_w2_v, o_w_out, final_norm):
    bsz, s, d = x.shape
    depth = e_norm.shape[0] + o_norm.shape[0]
    h = x.reshape(bsz * s, d)
    for layer in range(depth):
        i = layer // 2
        if layer % 2 == 0:
            h = _even_layer(h, bsz, s, e_norm[i], e_w_in[i], e_ssd_conv_w[i], e_ssd_conv_b[i], e_dt_bias[i],
                            e_a_log[i], e_d_skip[i], e_ssd_norm[i], e_conf_conv_w[i], e_conf_conv_b[i],
                            e_conf_ln_g[i], e_conf_ln_b[i], e_w_out[i])
        else:
            h = _odd_layer(h, bsz, s, o_norm[i], o_w_in[i], o_gate_bias[i], o_cmp_pe_k[i], o_cmp_w1_k[i],
                           o_cmp_w2_k[i], o_cmp_pe_v[i], o_cmp_w1_v[i], o_cmp_w2_v[i], o_w_out[i],
                           final_norm, layer == depth - 1)
    return h.reshape(bsz, s, d)
```

```python
import functools
import math

import numpy as np
import jax
import jax.numpy as jnp
from jax import lax
from jax.experimental import pallas as pl
from jax.experimental.pallas import tpu as pltpu

F32 = jnp.float32
BF16 = jnp.bfloat16

D_MODEL = 1024
SSD_HEADS = 16
SSD_HEAD_DIM = 64
SSD_INNER = 1024
SSD_GROUPS = 2
SSD_STATE = 128
SSD_CONV = 4
SSD_CHUNK = 128
SSD_XBC = SSD_INNER + 2 * SSD_GROUPS * SSD_STATE
CONF_WIDTH = 1024
CONF_CONV_WIDTH = 31
NSA_HEADS = 16
NSA_KV_GROUPS = 4
NSA_HPG = 4
NSA_HEAD_DIM = 64
NSA_WIDTH = 1024
KV_WIDTH = 256
CMP_BLOCK = 32
CMP_STRIDE = 16
CMP_HIDDEN = 256
SEL_BLOCK = 64
SEL_TOPK = 16
WINDOW = 512
FORCE_BONUS = 1e6
NORM_EPS = 1e-6

LANES = 128
SUBLANES = 8
VMEM_LIMIT_BYTES = 56 * 1024 * 1024

PROJ_ROWS = 256
CONF_ROWS = 256
CONF_ROW_CHUNK = 32
CONF_LANE_CHUNK = 512
ATT_TQ = 256
SEL_TK = 512
NEG_BIG = -1e30
V_AUG_ROWS = 80


def _dot(a, b, precision=None):
    return jnp.dot(a, b, preferred_element_type=F32, precision=precision)


def _dot_nt(a, b):
    return lax.dot_general(a, b, (((1,), (1,)), ((), ())), preferred_element_type=F32)


def _dot_tn(a, b):
    return lax.dot_general(a, b, (((0,), (0,)), ((), ())), preferred_element_type=F32)


def _silu(x):
    return x * jax.nn.sigmoid(x)


def _softplus(x):
    return jnp.maximum(x, 0.0) + jnp.log1p(jnp.exp(-jnp.abs(x)))


def _params(semantics):
    return pltpu.CompilerParams(dimension_semantics=semantics,
                                vmem_limit_bytes=VMEM_LIMIT_BYTES)


def _proj_kernel(*refs, nat_cols, t_rows):
    x_ref, g_ref = refs[0], refs[1]
    pos = 2
    w_ref = wt_ref = None
    if nat_cols:
        w_ref = refs[pos]
        pos += 1
    if t_rows:
        wt_ref = refs[pos]
        pos += 1
    out_refs = refs[pos:]
    x = x_ref[...]
    ms = jnp.mean(x * x, axis=-1, keepdims=True)
    xn = ((x * lax.rsqrt(ms + NORM_EPS)) * g_ref[...]).astype(BF16)
    k = 0
    for (a, b) in nat_cols:
        out_refs[k][...] = _dot(xn, w_ref[:, a:b]).astype(out_refs[k].dtype)
        k += 1
    for (a, b) in t_rows:
        out_refs[k][...] = _dot_nt(wt_ref[a:b, :], xn).astype(out_refs[k].dtype)
        k += 1


def _norm_proj(h2d, gain, w_nat, nat_spec, w_t, t_spec, tm):
    t_tokens, d = h2d.shape
    nat_cols, off = [], 0
    for width, _ in nat_spec:
        nat_cols.append((off, off + width))
        off += width
    t_rows, off = [], 0
    for rows, _ in t_spec:
        t_rows.append((off, off + rows))
        off += rows
    in_specs = [pl.BlockSpec((tm, d), lambda i: (i, 0)),
                pl.BlockSpec((1, d), lambda i: (0, 0))]
    args = [h2d, gain.reshape(1, d)]
    if nat_spec:
        in_specs.append(pl.BlockSpec(w_nat.shape, lambda i: (0, 0)))
        args.append(w_nat)
    if t_spec:
        in_specs.append(pl.BlockSpec(w_t.shape, lambda i: (0, 0)))
        args.append(w_t)
    out_shape, out_specs = [], []
    for width, dt in nat_spec:
        out_shape.append(jax.ShapeDtypeStruct((t_tokens, width), dt))
        out_specs.append(pl.BlockSpec((tm, width), lambda i: (i, 0)))
    for rows, dt in t_spec:
        out_shape.append(jax.ShapeDtypeStruct((rows, t_tokens), dt))
        out_specs.append(pl.BlockSpec((rows, tm), lambda i: (0, i)))
    return pl.pallas_call(
        functools.partial(_proj_kernel, nat_cols=tuple(nat_cols), t_rows=tuple(t_rows)),
        grid=(t_tokens // tm,),
        in_specs=in_specs,
        out_specs=out_specs,
        out_shape=out_shape,
        compiler_params=_params(("parallel",)),
        name="norm_proj",
    )(*args)


def _ssd_kernel(xbc_ref, z_ref, dt_ref, dtT_ref, cw_ref, cb_ref, dtb_ref, dtbT_ref,
                alog_ref, alogT_ref, dskip_ref, norm_ref, y_ref,
                state_sc, carry_sc, win_sc):
    L = SSD_CHUNK
    hp = LANES // SSD_HEAD_DIM
    gw = SSD_INNER // SSD_GROUPS
    heads_per_group = SSD_HEADS // SSD_GROUPS

    @pl.when(pl.program_id(1) == 0)
    def _():
        state_sc[...] = jnp.zeros_like(state_sc)
        carry_sc[...] = jnp.zeros_like(carry_sc)

    xraw = xbc_ref[0]
    win_sc[0:SUBLANES, :] = carry_sc[...]
    win_sc[SUBLANES:SUBLANES + L, :] = xraw
    carry_sc[...] = xraw[L - SUBLANES:L, :]
    conv = cw_ref[0:1, :] * win_sc[SUBLANES - 3:SUBLANES - 3 + L, :]
    for k in range(1, SSD_CONV):
        o = SUBLANES - (SSD_CONV - 1) + k
        conv = conv + cw_ref[k:k + 1, :] * win_sc[o:o + L, :]
    xact = _silu(conv + cb_ref[...])
    xs = xact[:, :SSD_INNER]
    bm = xact[:, SSD_INNER:SSD_INNER + SSD_GROUPS * SSD_STATE].astype(BF16)
    cm = xact[:, SSD_INNER + SSD_GROUPS * SSD_STATE:].astype(BF16)

    dt = _softplus(dt_ref[0] + dtb_ref[...])
    d_a = dt * (-jnp.exp(alog_ref[...]))
    rows = lax.broadcasted_iota(jnp.int32, (L, L), 0)
    cols = lax.broadcasted_iota(jnp.int32, (L, L), 1)
    causal = rows >= cols
    cum = _dot(causal.astype(F32), d_a, precision=lax.Precision.HIGHEST)
    dt_t = _softplus(dtT_ref[...] + dtbT_ref[...])
    d_a_t = dt_t * (-jnp.exp(alogT_ref[...]))
    cum_t = _dot(d_a_t, (rows <= cols).astype(F32), precision=lax.Precision.HIGHEST)
    ecum = jnp.exp(cum)
    cum_last = cum[L - 1:L, :]
    to_end = jnp.exp(cum_last - cum) * dt
    e_last = jnp.exp(cum_last)

    lane = lax.broadcasted_iota(jnp.int32, (L, LANES), 1)
    first_head = lane < SSD_HEAD_DIM
    lane1 = lax.broadcasted_iota(jnp.int32, (1, LANES), 1)
    first_head1 = lane1 < SSD_HEAD_DIM

    def per_head(arr, h0, mask):
        return jnp.where(mask, arr[:, h0:h0 + 1], arr[:, h0 + 1:h0 + 2])

    y_blocks, ecum_blocks, toend_blocks, elast_blocks = [], [], [], []
    for g in range(SSD_GROUPS):
        bm_g = bm[:, g * SSD_STATE:(g + 1) * SSD_STATE]
        cm_g = cm[:, g * SSD_STATE:(g + 1) * SSD_STATE]
        cb = _dot_nt(cm_g, bm_g)
        for j in range(heads_per_group // hp):
            h0 = g * heads_per_group + j * hp
            c0 = h0 * SSD_HEAD_DIM
            x_pair = xs[:, c0:c0 + LANES].astype(BF16)
            res = []
            for hh in range(hp):
                h = h0 + hh
                seg = cum[:, h:h + 1] - cum_t[h:h + 1, :]
                decay = jnp.exp(jnp.where(causal, seg, -jnp.inf))
                wts = (cb * decay) * dt_t[h:h + 1, :]
                res.append(_dot(wts.astype(BF16), x_pair))
            y_blocks.append(jnp.where(first_head, res[0], res[1]))
            ecum_blocks.append(per_head(ecum, h0, first_head))
            toend_blocks.append(per_head(to_end, h0, first_head))
            elast_blocks.append(per_head(e_last, h0, first_head1))
    y_diag = jnp.concatenate(y_blocks, axis=1)
    ecum_x = jnp.concatenate(ecum_blocks, axis=1)
    toend_x = jnp.concatenate(toend_blocks, axis=1)
    elast_x = jnp.concatenate(elast_blocks, axis=1)

    xw = (xs * toend_x).astype(BF16)
    y_off_blocks = []
    for g in range(SSD_GROUPS):
        bm_g = bm[:, g * SSD_STATE:(g + 1) * SSD_STATE]
        cm_g = cm[:, g * SSD_STATE:(g + 1) * SSD_STATE]
        st = state_sc[g]
        y_off_blocks.append(_dot(cm_g, st.astype(BF16)))
        state_sc[g] = st * elast_x[:, g * gw:(g + 1) * gw] + \
            _dot_tn(bm_g, xw[:, g * gw:(g + 1) * gw])
    y_off = jnp.concatenate(y_off_blocks, axis=1) * ecum_x

    y = y_diag + y_off + dskip_ref[...] * xs
    yg = y * _silu(z_ref[0])
    outs = []
    for g in range(SSD_GROUPS):
        yg_g = yg[:, g * gw:(g + 1) * gw]
        ms = jnp.mean(yg_g * yg_g, axis=-1, keepdims=True)
        outs.append(yg_g * lax.rsqrt(ms + NORM_EPS))
    y_ref[0] = (jnp.concatenate(outs, axis=1) * norm_ref[...]).astype(y_ref.dtype)


def _ssd(xbc, z, dt_nat, dt_t, conv_w, conv_b, dt_bias, a_log, d_skip, ssd_norm, bsz, s):
    nch = s // SSD_CHUNK
    L = SSD_CHUNK
    pad = LANES - SSD_HEADS
    dtb = jnp.pad(dt_bias, (0, pad)).reshape(1, LANES)
    alog = jnp.pad(a_log, (0, pad)).reshape(1, LANES)
    dskip_x = jnp.repeat(d_skip, SSD_HEAD_DIM).reshape(1, SSD_INNER)
    const = lambda shape: pl.BlockSpec(shape, lambda b, c: tuple(0 for _ in shape))
    return pl.pallas_call(
        _ssd_kernel,
        grid=(bsz, nch),
        in_specs=[
            pl.BlockSpec((1, L, SSD_XBC), lambda b, c: (b, c, 0)),
            pl.BlockSpec((1, L, SSD_INNER), lambda b, c: (b, c, 0)),
            pl.BlockSpec((1, L, LANES), lambda b, c: (b, c, 0)),
            pl.BlockSpec((SSD_HEADS, L), lambda b, c: (0, b * nch + c)),
            const((SSD_CONV, SSD_XBC)), const((1, SSD_XBC)),
            const((1, LANES)), const((SSD_HEADS, 1)),
            const((1, LANES)), const((SSD_HEADS, 1)),
            const((1, SSD_INNER)), const((1, SSD_INNER)),
        ],
        out_specs=pl.BlockSpec((1, L, SSD_INNER), lambda b, c: (b, c, 0)),
        out_shape=jax.ShapeDtypeStruct((bsz, s, SSD_INNER), BF16),
        scratch_shapes=[
            pltpu.VMEM((SSD_GROUPS, SSD_STATE, SSD_INNER // SSD_GROUPS), F32),
            pltpu.VMEM((SUBLANES, SSD_XBC), F32),
            pltpu.VMEM((SUBLANES + L, SSD_XBC), F32),
        ],
        compiler_params=_params(("parallel", "arbitrary")),
        name="ssd_scan",
    )(xbc.reshape(bsz, s, SSD_XBC), z.reshape(bsz, s, SSD_INNER),
      dt_nat.reshape(bsz, s, LANES), dt_t,
      conv_w, conv_b.reshape(1, SSD_XBC), dtb, dt_bias.reshape(SSD_HEADS, 1),
      alog, a_log.reshape(SSD_HEADS, 1), dskip_x, ssd_norm.reshape(1, SSD_INNER))


def _conf_kernel(glu_ref, zc_ref, w_ref, b_ref, lng_ref, lnb_ref, y_ref, buf_sc, u_sc):
    ts = glu_ref.shape[1]
    halo = buf_sc.shape[0] - ts
    first = halo - (CONF_CONV_WIDTH - 1)

    @pl.when(pl.program_id(1) == 0)
    def _():
        buf_sc[0:halo, :] = jnp.zeros((halo, CONF_WIDTH), F32)

    buf_sc[halo:halo + ts, :] = glu_ref[0, :, :CONF_WIDTH] * \
        jax.nn.sigmoid(glu_ref[0, :, CONF_WIDTH:])
    for r0 in range(0, ts, CONF_ROW_CHUNK):
        for c0 in range(0, CONF_WIDTH, CONF_LANE_CHUNK):
            cs = slice(c0, c0 + CONF_LANE_CHUNK)
            acc = w_ref[0:1, cs] * buf_sc[r0 + first:r0 + first + CONF_ROW_CHUNK, cs]
            for k in range(1, CONF_CONV_WIDTH):
                o = r0 + first + k
                acc = acc + w_ref[k:k + 1, cs] * buf_sc[o:o + CONF_ROW_CHUNK, cs]
            u_sc[r0:r0 + CONF_ROW_CHUNK, cs] = acc
    buf_sc[0:halo, :] = buf_sc[ts:ts + halo, :]
    u = u_sc[...] + b_ref[...]
    mu = jnp.mean(u, axis=-1, keepdims=True)
    var = jnp.mean(jnp.square(u - mu), axis=-1, keepdims=True)
    un = (u - mu) * lax.rsqrt(var + NORM_EPS) * lng_ref[...] + lnb_ref[...]
    y_ref[0] = (_silu(un) * _silu(zc_ref[0])).astype(y_ref.dtype)


def _conformer(glu, zc, conv_w, conv_b, ln_g, ln_b, bsz, s):
    ts = min(CONF_ROWS, s)
    halo = 32
    wpad = jnp.pad(conv_w, ((0, halo - CONF_CONV_WIDTH), (0, 0)))
    const = lambda shape: pl.BlockSpec(shape, lambda b, i: tuple(0 for _ in shape))
    return pl.pallas_call(
        _conf_kernel,
        grid=(bsz, s // ts),
        in_specs=[
            pl.BlockSpec((1, ts, 2 * CONF_WIDTH), lambda b, i: (b, i, 0)),
            pl.BlockSpec((1, ts, CONF_WIDTH), lambda b, i: (b, i, 0)),
            const((halo, CONF_WIDTH)), const((1, CONF_WIDTH)),
            const((1, CONF_WIDTH)), const((1, CONF_WIDTH)),
        ],
        out_specs=pl.BlockSpec((1, ts, CONF_WIDTH), lambda b, i: (b, i, 0)),
        out_shape=jax.ShapeDtypeStruct((bsz, s, CONF_WIDTH), BF16),
        scratch_shapes=[pltpu.VMEM((halo + ts, CONF_WIDTH), F32),
                        pltpu.VMEM((ts, CONF_WIDTH), F32)],
        compiler_params=_params(("parallel", "arbitrary")),
        name="conformer",
    )(glu.reshape(bsz, s, 2 * CONF_WIDTH), zc.reshape(bsz, s, CONF_WIDTH), wpad,
      conv_b.reshape(1, -1), ln_g.reshape(1, -1), ln_b.reshape(1, -1))


def _outproj_even_kernel(h_ref, a_ref, b_ref, wa_ref, wb_ref, o_ref):
    o_ref[...] = h_ref[...] + (_dot(a_ref[...], wa_ref[...]) + _dot(b_ref[...], wb_ref[...]))


def _outproj_even(h2d, ya, yb, w_out, tm):
    t_tokens, d = h2d.shape
    wa = w_out[:SSD_INNER].astype(BF16)
    wb = w_out[SSD_INNER:].astype(BF16)
    row = lambda w: pl.BlockSpec((tm, w), lambda i: (i, 0))
    return pl.pallas_call(
        _outproj_even_kernel,
        grid=(t_tokens // tm,),
        in_specs=[row(d), row(SSD_INNER), row(CONF_WIDTH),
                  pl.BlockSpec(wa.shape, lambda i: (0, 0)),
                  pl.BlockSpec(wb.shape, lambda i: (0, 0))],
        out_specs=row(d),
        out_shape=jax.ShapeDtypeStruct((t_tokens, d), F32),
        compiler_params=_params(("parallel",)),
        name="outproj_even",
    )(h2d, ya, yb, wa, wb)


def _outproj_odd_kernel(h_ref, oc_ref, os_ref, ow_ref, zt_ref, w_ref, g_ref, o_ref, *, final):
    o = (oc_ref[...] + os_ref[...]) + ow_ref[...]
    y = (o * _silu(zt_ref[...])).astype(BF16)
    out = h_ref[...] + _dot_tn(y, w_ref[...])
    if final:
        ms = jnp.mean(out * out, axis=-1, keepdims=True)
        out = (out * lax.rsqrt(ms + NORM_EPS)) * g_ref[...]
    o_ref[...] = out


def _outproj_odd(h2d, oc, osel, ow, z_t, w_out, final_gain, final, tm):
    t_tokens, d = h2d.shape
    w = w_out.astype(BF16)
    col = pl.BlockSpec((NSA_WIDTH, tm), lambda i: (0, i))
    return pl.pallas_call(
        functools.partial(_outproj_odd_kernel, final=final),
        grid=(t_tokens // tm,),
        in_specs=[pl.BlockSpec((tm, d), lambda i: (i, 0)), col, col, col, col,
                  pl.BlockSpec(w.shape, lambda i: (0, 0)),
                  pl.BlockSpec((1, d), lambda i: (0, 0))],
        out_specs=pl.BlockSpec((tm, d), lambda i: (i, 0)),
        out_shape=jax.ShapeDtypeStruct((t_tokens, d), F32),
        compiler_params=_params(("parallel",)),
        name="outproj_odd",
    )(h2d, oc, osel, ow, z_t, w, final_gain.reshape(1, d))


def _compress_kernel(xk_ref, xv_ref, pek_ref, pev_ref, w1k_ref, w1v_ref, w2k_ref, w2vt_ref,
                     kc_ref, vct_ref):
    half = CMP_STRIDE * NSA_HEAD_DIM

    def hidden(x_ref, pe_ref, w1_ref):
        x = x_ref[0, 0]
        n = x.shape[0]
        lo = (x + pe_ref[0:1, :]).astype(BF16)
        hi = (x + pe_ref[1:2, :]).astype(BF16)
        h = _dot(lo, w1_ref[0:half, :]) + pltpu.roll(_dot(hi, w1_ref[half:2 * half, :]), n - 1, 0)
        return _silu(h).astype(BF16)

    kc_ref[0, 0] = _dot(hidden(xk_ref, pek_ref, w1k_ref), w2k_ref[...]).astype(kc_ref.dtype)
    vct_ref[0, 0] = _dot_nt(w2vt_ref[...], hidden(xv_ref, pev_ref, w1v_ref)).astype(vct_ref.dtype)


def _compress(kc, vc, pe_k, w1_k, w2_k, pe_v, w1_v, w2_v, bsz, s):
    g, d = NSA_KV_GROUPS, NSA_HEAD_DIM
    nh = s // CMP_STRIDE
    half = CMP_STRIDE * d

    def regroup(t):
        return t.reshape(bsz, nh, CMP_STRIDE, g, d).transpose(0, 3, 1, 2, 4).reshape(bsz, g, nh, half)

    const = lambda shape: pl.BlockSpec(shape, lambda b, gi: tuple(0 for _ in shape))
    blk = pl.BlockSpec((1, 1, nh, half), lambda b, gi: (b, gi, 0, 0))
    return pl.pallas_call(
        _compress_kernel,
        grid=(bsz, g),
        in_specs=[blk, blk, const((2, half)), const((2, half)),
                  const((2 * half, CMP_HIDDEN)), const((2 * half, CMP_HIDDEN)),
                  const((CMP_HIDDEN, d)), const((d, CMP_HIDDEN))],
        out_specs=[pl.BlockSpec((1, 1, nh, d), lambda b, gi: (b, gi, 0, 0)),
                   pl.BlockSpec((1, 1, d, nh), lambda b, gi: (b, gi, 0, 0))],
        out_shape=[jax.ShapeDtypeStruct((bsz, g, nh, d), BF16),
                   jax.ShapeDtypeStruct((bsz, g, d, nh), BF16)],
        compiler_params=_params(("parallel", "parallel")),
        name="compress_kv",
    )(regroup(kc), regroup(vc), pe_k.reshape(2, half), pe_v.reshape(2, half),
      w1_k.astype(BF16), w1_v.astype(BF16), w2_k.astype(BF16), w2_v.T.astype(BF16))


def _selection_bias(imp, t0, topk):
    n_sel, tq = imp.shape
    jidx = lax.broadcasted_iota(jnp.int32, (n_sel, tq), 0)
    t = t0 + lax.broadcasted_iota(jnp.int32, (n_sel, tq), 1)
    cur = lax.shift_right_logical(t, int(math.log2(SEL_BLOCK)))
    valid = jidx * SEL_BLOCK <= t
    forced = (jidx == 0) | (jidx == cur) | (jidx == cur - 1)
    v = jnp.where(valid, imp + jnp.where(forced, FORCE_BONUS, 0.0), -jnp.inf)
    cnt = jnp.zeros((n_sel, tq), jnp.int32)
    for i in range(n_sel):
        row = v[i:i + 1, :]
        ge = jnp.where(row >= v, 1, 0)
        gt = jnp.where(row > v, 1, 0)
        cnt = cnt + jnp.where(jidx > i, ge, gt)
    return jnp.where(valid & (cnt < topk), 0.0, NEG_BIG)


def _cmp_attn_kernel(qt_ref, kc_ref, vct_ref, ovt_ref, gl_ref, gb_ref, o_ref, selb_ref,
                     *, n_cmp, topk):
    tq = qt_ref.shape[2]
    nc = kc_ref.shape[2]
    t0 = pl.program_id(2) * tq
    cidx = lax.broadcasted_iota(jnp.int32, (nc, tq), 0)
    t = t0 + lax.broadcasted_iota(jnp.int32, (nc, tq), 1)
    cmask = (cidx * CMP_STRIDE + (CMP_BLOCK - 1) <= t) & (cidx < n_cmp)
    kc = kc_ref[0, 0]
    vct = vct_ref[0, 0]
    gates = jax.nn.sigmoid(gl_ref[0] + gb_ref[0])
    imp = jnp.zeros((ovt_ref.shape[0], tq), F32)
    for r in range(NSA_HPG):
        s = jnp.where(cmask, _dot(kc, qt_ref[r]), -jnp.inf)
        m = jnp.max(s, axis=0, keepdims=True)
        m = jnp.where(m > -jnp.inf, m, 0.0)
        e = jnp.exp(s - m)
        p = e * (1.0 / jnp.maximum(jnp.sum(e, axis=0, keepdims=True), 1e-30))
        pb = p.astype(BF16)
        o_ref[r] = _dot(vct, pb) * gates[r:r + 1, :]
        imp = imp + _dot(ovt_ref[...], pb)
    bias = _selection_bias(imp, t0, topk)
    n_pad = selb_ref.shape[2] - bias.shape[0]
    if n_pad:
        bias = jnp.concatenate([bias, jnp.zeros((n_pad, tq), F32)], axis=0)
    selb_ref[0, 0] = bias.astype(selb_ref.dtype)


def _gate_specs(nq):
    def spec(branch):
        return (pl.BlockSpec((1, SUBLANES, ATT_TQ), lambda b, g, i: (g * 3 + branch, 0, b * nq + i)),
                pl.BlockSpec((1, SUBLANES, 1), lambda b, g, i: (g * 3 + branch, 0, 0)))
    return spec


def _cmp_attn(q_t, kcmp, vcmp_t, gl_t, gbias, bsz, s):
    g, d = NSA_KV_GROUPS, NSA_HEAD_DIM
    tq = ATT_TQ
    nq = s // tq
    nc = s // CMP_STRIDE
    n_cmp = (s - CMP_BLOCK) // CMP_STRIDE + 1
    n_sel = s // SEL_BLOCK
    n_selp = _sel_pad(n_sel)
    topk = min(SEL_TOPK, n_sel)
    cmp_start = np.arange(nc) * CMP_STRIDE
    sel_start = np.arange(n_sel) * SEL_BLOCK
    ov = ((cmp_start[:, None] < sel_start[None, :] + SEL_BLOCK) &
          (cmp_start[:, None] + CMP_BLOCK > sel_start[None, :]) &
          (np.arange(nc)[:, None] < n_cmp))
    ovt = jnp.asarray(ov.T.astype(np.float32), dtype=BF16)
    gl_spec, gb_spec = _gate_specs(nq)(0)
    t_tokens = bsz * s
    return pl.pallas_call(
        functools.partial(_cmp_attn_kernel, n_cmp=n_cmp, topk=topk),
        grid=(bsz, g, nq),
        in_specs=[
            pl.BlockSpec((NSA_HPG, d, tq), lambda b, gi, i: (gi, 0, b * nq + i)),
            pl.BlockSpec((1, 1, nc, d), lambda b, gi, i: (b, gi, 0, 0)),
            pl.BlockSpec((1, 1, d, nc), lambda b, gi, i: (b, gi, 0, 0)),
            pl.BlockSpec((n_sel, nc), lambda b, gi, i: (0, 0)),
            gl_spec, gb_spec,
        ],
        out_specs=[pl.BlockSpec((NSA_HPG, d, tq), lambda b, gi, i: (gi, 0, b * nq + i)),
                   pl.BlockSpec((1, 1, n_selp, tq), lambda b, gi, i: (b, gi, 0, i))],
        out_shape=[jax.ShapeDtypeStruct((NSA_HEADS, d, t_tokens), F32),
                   jax.ShapeDtypeStruct((bsz, g, n_selp, s), BF16)],
        compiler_params=_params(("parallel", "parallel", "parallel")),
        name="cmp_attn_select",
    )(q_t, kcmp, vcmp_t, ovt, gl_t, gbias)


def _sel_pad(n_sel):
    return -(-n_sel // NSA_HEAD_DIM) * NSA_HEAD_DIM


def _augment_values(v_t):
    g, d, t = v_t.shape
    return jnp.concatenate([v_t, jnp.ones((g, 1, t), v_t.dtype),
                            jnp.zeros((g, V_AUG_ROWS - d - 1, t), v_t.dtype)], axis=1)


def _sel_attn_kernel(qt_ref, ka_ref, va_ref, selb_ref, gl_ref, gb_ref, o_ref,
                     qa_sc, sa_sc, sb_sc, ma_sc, mb_sc, m_sc, acc_sc, *, tk):
    tq = qt_ref.shape[2]
    d = NSA_HEAD_DIM
    w = NSA_HPG * tq
    i = pl.program_id(2)
    t0 = i * tq
    last = (i * tq) // tk
    for r in range(NSA_HPG):
        qa_sc[0:d, r * tq:(r + 1) * tq] = qt_ref[r]
        qa_sc[d:, r * tq:(r + 1) * tq] = selb_ref[0, 0]
    m_sc[...] = jnp.full(m_sc.shape, NEG_BIG, F32)
    acc_sc[...] = jnp.zeros(acc_sc.shape, F32)

    def scores(j, s_ref, mx_ref, diagonal):
        k0 = pl.multiple_of(j * tk, tk)
        s = _dot(ka_ref[0, pl.ds(k0, tk), :], qa_sc[...])
        if diagonal:
            kpos = k0 + lax.broadcasted_iota(jnp.int32, (tk, w), 0)
            t = t0 + (lax.broadcasted_iota(jnp.int32, (tk, w), 1) & (tq - 1))
            s = jnp.where(kpos <= t, s, NEG_BIG)
        s_ref[...] = s
        mx_ref[...] = jnp.max(s, axis=0, keepdims=True)

    def accumulate(j, s_ref, mx_ref):
        k0 = pl.multiple_of(j * tk, tk)
        m_old = m_sc[...]
        m_new = jnp.maximum(m_old, mx_ref[...])
        p = jnp.exp(s_ref[...] - m_new).astype(BF16)
        acc_sc[...] = jnp.exp(m_old - m_new) * acc_sc[...] + _dot(va_ref[0, :, pl.ds(k0, tk)], p)
        m_sc[...] = m_new

    scores(last, sa_sc, ma_sc, True)

    def pair(mi, carry):
        j = 2 * mi
        scores(j, sb_sc, mb_sc, False)
        accumulate(jnp.where(mi == 0, last, j - 1), sa_sc, ma_sc)
        scores(j + 1, sa_sc, ma_sc, False)
        accumulate(j, sb_sc, mb_sc)
        return carry

    lax.fori_loop(0, last // 2, pair, 0)

    @pl.when(last % 2 == 1)
    def _():
        scores(last - 1, sb_sc, mb_sc, False)
        accumulate(jnp.where(last == 1, last, last - 2), sa_sc, ma_sc)
        accumulate(last - 1, sb_sc, mb_sc)

    @pl.when(last % 2 == 0)
    def _():
        accumulate(jnp.where(last == 0, last, last - 1), sa_sc, ma_sc)

    gates = jax.nn.sigmoid(gl_ref[0] + gb_ref[0])
    for r in range(NSA_HPG):
        a = acc_sc[:, r * tq:(r + 1) * tq]
        o_ref[r] = a[0:d] * (1.0 / a[d:d + 1]) * gates[r:r + 1, :]


def _sel_attn(q_t, ks_g, vs_t, selb, gl_t, gbias, bsz, s):
    g, d = NSA_KV_GROUPS, NSA_HEAD_DIM
    tq = ATT_TQ
    tk = min(SEL_TK, s)
    nq = s // tq
    w = NSA_HPG * tq
    n_selp = selb.shape[2]
    kd = d + n_selp
    block_of = (np.arange(bsz * s) % s) // SEL_BLOCK
    onehot = jnp.asarray(block_of[:, None] == np.arange(n_selp)[None, :], dtype=BF16)
    ka = jnp.concatenate([ks_g, jnp.broadcast_to(onehot[None], (g, bsz * s, n_selp))], axis=-1)
    va = _augment_values(vs_t)
    gl_spec, gb_spec = _gate_specs(nq)(1)
    return pl.pallas_call(
        functools.partial(_sel_attn_kernel, tk=tk),
        grid=(bsz, g, nq),
        in_specs=[
            pl.BlockSpec((NSA_HPG, d, tq), lambda b, gi, i: (gi, 0, b * nq + i)),
            pl.BlockSpec((1, s, kd), lambda b, gi, i: (gi, b, 0)),
            pl.BlockSpec((1, V_AUG_ROWS, s), lambda b, gi, i: (gi, 0, b)),
            pl.BlockSpec((1, 1, n_selp, tq), lambda b, gi, i: (b, gi, 0, i)),
            gl_spec, gb_spec,
        ],
        out_specs=pl.BlockSpec((NSA_HPG, d, tq), lambda b, gi, i: (gi, 0, b * nq + i)),
        out_shape=jax.ShapeDtypeStruct((NSA_HEADS, d, bsz * s), F32),
        scratch_shapes=[pltpu.VMEM((kd, w), BF16),
                        pltpu.VMEM((tk, w), F32), pltpu.VMEM((tk, w), F32),
                        pltpu.VMEM((1, w), F32), pltpu.VMEM((1, w), F32),
                        pltpu.VMEM((1, w), F32), pltpu.VMEM((V_AUG_ROWS, w), F32)],
        compiler_params=_params(("parallel", "parallel", "arbitrary")),
        name="sel_attn",
    )(q_t, ka, va, selb, gl_t, gbias)


def _win_attn_kernel(*refs, nblk):
    qt_ref = refs[0]
    k_refs = refs[1:1 + nblk]
    v_refs = refs[1 + nblk:1 + 2 * nblk]
    gl_ref, gb_ref, o_ref, qa_sc = refs[1 + 2 * nblk:]
    tq = qt_ref.shape[2]
    d = NSA_HEAD_DIM
    w = NSA_HPG * tq
    i = pl.program_id(2)
    for r in range(NSA_HPG):
        qa_sc[:, r * tq:(r + 1) * tq] = qt_ref[r]
    krow = lax.broadcasted_iota(jnp.int32, (tq, w), 0)
    qcol = lax.broadcasted_iota(jnp.int32, (tq, w), 1) & (tq - 1)
    rel0 = qcol - krow
    s_blocks = []
    for jj in range(nblk):
        back = nblk - 1 - jj
        s = _dot(k_refs[jj][0], qa_sc[...])
        if back == 0:
            s = jnp.where(rel0 >= 0, s, NEG_BIG)
        else:
            limit = jnp.where(i >= back, WINDOW, -1)
            s = jnp.where(rel0 + back * tq < limit, s, NEG_BIG)
        s_blocks.append(s)
    m = jnp.max(s_blocks[0], axis=0, keepdims=True)
    for s in s_blocks[1:]:
        m = jnp.maximum(m, jnp.max(s, axis=0, keepdims=True))
    acc = _dot(v_refs[0][0], jnp.exp(s_blocks[0] - m).astype(BF16))
    for jj in range(1, nblk):
        acc = acc + _dot(v_refs[jj][0], jnp.exp(s_blocks[jj] - m).astype(BF16))
    gates = jax.nn.sigmoid(gl_ref[0] + gb_ref[0])
    for r in range(NSA_HPG):
        a = acc[:, r * tq:(r + 1) * tq]
        o_ref[r] = a[0:d] * (1.0 / a[d:d + 1]) * gates[r:r + 1, :]


def _win_attn(q_t, kw_g, vw_t, gl_t, gbias, bsz, s):
    g, d = NSA_KV_GROUPS, NSA_HEAD_DIM
    tq = ATT_TQ
    nq = s // tq
    nblk = WINDOW // tq + 1
    va = _augment_values(vw_t)
    gl_spec, gb_spec = _gate_specs(nq)(2)

    def k_spec(jj):
        return pl.BlockSpec((1, tq, d), lambda b, gi, i: (gi, b * nq + jnp.maximum(i - (nblk - 1) + jj, 0), 0))

    def v_spec(jj):
        return pl.BlockSpec((1, V_AUG_ROWS, tq),
                            lambda b, gi, i: (gi, 0, b * nq + jnp.maximum(i - (nblk - 1) + jj, 0)))

    return pl.pallas_call(
        functools.partial(_win_attn_kernel, nblk=nblk),
        grid=(bsz, g, nq),
        in_specs=[pl.BlockSpec((NSA_HPG, d, tq), lambda b, gi, i: (gi, 0, b * nq + i))]
        + [k_spec(jj) for jj in range(nblk)] + [v_spec(jj) for jj in range(nblk)]
        + [gl_spec, gb_spec],
        out_specs=pl.BlockSpec((NSA_HPG, d, tq), lambda b, gi, i: (gi, 0, b * nq + i)),
        out_shape=jax.ShapeDtypeStruct((NSA_HEADS, d, bsz * s), F32),
        scratch_shapes=[pltpu.VMEM((d, NSA_HPG * tq), BF16)],
        compiler_params=_params(("parallel", "parallel", "parallel")),
        name="win_attn",
    )(q_t, *([kw_g] * nblk), *([va] * nblk), gl_t, gbias)


def _even_layer(h2d, bsz, s, gain, w_in, ssd_conv_w, ssd_conv_b, dt_bias, a_log, d_skip, ssd_norm,
                conf_conv_w, conf_conv_b, conf_ln_g, conf_ln_b, w_out):
    o1 = SSD_INNER
    o2 = o1 + SSD_XBC
    o3 = o2 + SSD_HEADS
    o4 = o3 + 2 * CONF_WIDTH
    w_dt = w_in[:, o2:o3]
    w_nat = jnp.concatenate(
        [w_in[:, :o2], w_in[:, o3:], jnp.pad(w_dt, ((0, 0), (0, LANES - SSD_HEADS)))], axis=1).astype(BF16)
    nat_spec = [(SSD_INNER, F32), (SSD_XBC, F32), (2 * CONF_WIDTH, F32), (CONF_WIDTH, F32), (LANES, F32)]
    tm = min(PROJ_ROWS, s)
    z, xbc, glu, zc, dt_nat, dt_t = _norm_proj(
        h2d, gain, w_nat, nat_spec, w_dt.T.astype(BF16), [(SSD_HEADS, F32)], tm)
    y_a = _ssd(xbc, z, dt_nat, dt_t, ssd_conv_w, ssd_conv_b, dt_bias, a_log, d_skip, ssd_norm, bsz, s)
    y_b = _conformer(glu, zc, conf_conv_w, conf_conv_b, conf_ln_g, conf_ln_b, bsz, s)
    return _outproj_even(h2d, y_a.reshape(bsz * s, SSD_INNER), y_b.reshape(bsz * s, CONF_WIDTH), w_out, tm)


def _odd_layer(h2d, bsz, s, gain, w_in, gate_bias, pe_k, w1_k, w2_k, pe_v, w1_v, w2_v, w_out,
               final_gain, final):
    g, r, d = NSA_KV_GROUPS, NSA_HPG, NSA_HEAD_DIM
    t_tokens = bsz * s
    sizes = [NSA_WIDTH] + [KV_WIDTH] * 6 + [3 * NSA_HEADS, NSA_WIDTH]
    offs = np.cumsum([0] + sizes)
    col = lambda k: w_in[:, offs[k]:offs[k + 1]]
    w_q, w_kc, w_vc, w_ks, w_vs, w_kw, w_vw, w_gl, w_z = [col(k) for k in range(9)]
    perm = np.array([[[(gi * r + ri) * 3 + br for ri in range(r)] for br in range(3)] for gi in range(g)])
    w_gl_g = jnp.pad(w_gl[:, perm.reshape(-1)].reshape(D_MODEL, g * 3, r),
                     ((0, 0), (0, 0), (0, SUBLANES - r))).reshape(D_MODEL, g * 3 * SUBLANES)
    gbias = jnp.pad(gate_bias[perm.reshape(-1)].reshape(g * 3, r), ((0, 0), (0, SUBLANES - r)))
    gbias = gbias.reshape(g * 3, SUBLANES, 1)
    scale = NSA_HEAD_DIM ** -0.5
    w_nat = jnp.concatenate([w_kc, w_vc, w_ks, w_kw], axis=1).astype(BF16)
    w_t = jnp.concatenate([w_q * scale, w_vs, w_vw, w_gl_g, w_z], axis=1).T.astype(BF16)
    nat_spec = [(KV_WIDTH, F32), (KV_WIDTH, F32), (KV_WIDTH, BF16), (KV_WIDTH, BF16)]
    t_spec = [(NSA_WIDTH, BF16), (KV_WIDTH, BF16), (KV_WIDTH, BF16), (g * 3 * SUBLANES, F32), (NSA_WIDTH, F32)]
    tm = min(PROJ_ROWS, s)
    kc, vc, ks, kw, q_t, vs_t, vw_t, gl_t, z_t = _norm_proj(h2d, gain, w_nat, nat_spec, w_t, t_spec, tm)

    q_t = q_t.reshape(NSA_HEADS, d, t_tokens)
    gl_t = gl_t.reshape(g * 3, SUBLANES, t_tokens)
    group_major = lambda t: t.reshape(t_tokens, g, d).transpose(1, 0, 2)
    ks_g, kw_g = group_major(ks), group_major(kw)
    vs_t = vs_t.reshape(g, d, t_tokens)
    vw_t = vw_t.reshape(g, d, t_tokens)

    kcmp, vcmp_t = _compress(kc, vc, pe_k, w1_k, w2_k, pe_v, w1_v, w2_v, bsz, s)
    o_cmp, selb = _cmp_attn(q_t, kcmp, vcmp_t, gl_t, gbias, bsz, s)
    o_sel = _sel_attn(q_t, ks_g, vs_t, selb, gl_t, gbias, bsz, s)
    o_win = _win_attn(q_t, kw_g, vw_t, gl_t, gbias, bsz, s)
    flat = lambda o: o.reshape(NSA_WIDTH, t_tokens)
    return _outproj_odd(h2d, flat(o_cmp), flat(o_sel), flat(o_win), z_t, w_out, final_gain, final, tm)


def kernel(x, e_norm, e_w_in, e_ssd_conv_w, e_ssd_conv_b, e_dt_bias, e_a_log, e_d_skip, e_ssd_norm,
           e_conf_conv_w, e_conf_conv_b, e_conf_ln_g, e_conf_ln_b, e_w_out, o_norm, o_w_in, o_gate_bias,
           o_cmp_pe_k, o_cmp_w1_k, o_cmp_w2_k, o_cmp_pe_v, o_cmp_w1_v, o_cmp_w2_v, o_w_out, final_norm):
    bsz, s, d = x.shape
    depth = e_norm.shape[0] + o_norm.shape[0]
    h = x.reshape(bsz * s, d)
    for layer in range(depth):
        i = layer // 2
        if layer % 2 == 0:
            h = _even_layer(h, bsz, s, e_norm[i], e_w_in[i], e_ssd_conv_w[i], e_ssd_conv_b[i], e_dt_bias[i],
                            e_a_log[i], e_d_skip[i], e_ssd_norm[i], e_conf_conv_w[i], e_conf_conv_b[i],
                            e_conf_ln_g[i], e_conf_ln_b[i], e_w_out[i])
        else:
            h = _odd_layer(h, bsz, s, o_norm[i], o_w_in[i], o_gate_bias[i], o_cmp_pe_k[i], o_cmp_w1_k[i],
                           o_cmp_w2_k[i], o_cmp_pe_v[i], o_cmp_w1_v[i], o_cmp_w2_v[i], o_w_out[i],
                           final_norm, layer == depth - 1)
    return h.reshape(bsz, s, d)
```

```python
import functools
import math

import numpy as np
import jax
import jax.numpy as jnp
from jax import lax
from jax.experimental import pallas as pl
from jax.experimental.pallas import tpu as pltpu

F32 = jnp.float32
BF16 = jnp.bfloat16

D_MODEL = 1024
SSD_HEADS = 16
SSD_HEAD_DIM = 64
SSD_INNER = 1024
SSD_GROUPS = 2
SSD_STATE = 128
SSD_CONV = 4
SSD_CHUNK = 128
SSD_XBC = SSD_INNER + 2 * SSD_GROUPS * SSD_STATE
CONF_WIDTH = 1024
CONF_CONV_WIDTH = 31
NSA_HEADS = 16
NSA_KV_GROUPS = 4
NSA_HPG = 4
NSA_HEAD_DIM = 64
NSA_WIDTH = 1024
KV_WIDTH = 256
CMP_BLOCK = 32
CMP_STRIDE = 16
CMP_HIDDEN = 256
SEL_BLOCK = 64
SEL_TOPK = 16
WINDOW = 512
FORCE_BONUS = 1e6
NORM_EPS = 1e-6

LANES = 128
SUBLANES = 8
VMEM_LIMIT_BYTES = 56 * 1024 * 1024

PROJ_ROWS = 256
CONF_ROWS = 256
CONF_ROW_CHUNK = 32
CONF_LANE_CHUNK = 512
ATT_TQ = 256
SEL_TK = 512
WIN_HEAD_CHUNK = 4
NEG_BIG = -1e30
V_AUG_ROWS = 80


def _dot(a, b, precision=None):
    return jnp.dot(a, b, preferred_element_type=F32, precision=precision)


def _dot_nt(a, b):
    return lax.dot_general(a, b, (((1,), (1,)), ((), ())), preferred_element_type=F32)


def _dot_tn(a, b):
    return lax.dot_general(a, b, (((0,), (0,)), ((), ())), preferred_element_type=F32)


def _silu(x):
    return x * jax.nn.sigmoid(x)


def _softplus(x):
    return jnp.maximum(x, 0.0) + jnp.log1p(jnp.exp(-jnp.abs(x)))


def _params(semantics):
    return pltpu.CompilerParams(dimension_semantics=semantics,
                                vmem_limit_bytes=VMEM_LIMIT_BYTES)


def _proj_kernel(*refs, nat_cols, t_rows, onehot_out, regroup_outs, seq_len):
    x_ref, g_ref = refs[0], refs[1]
    pos = 2
    w_ref = wt_ref = tb_ref = None
    if nat_cols:
        w_ref = refs[pos]
        pos += 1
    if t_rows:
        wt_ref, tb_ref = refs[pos], refs[pos + 1]
        pos += 2
    n_out = len(nat_cols) + len(t_rows)
    out_refs = refs[pos:pos + n_out]
    scratch = refs[pos + n_out:]
    x = x_ref[...]
    tm = x.shape[0]
    ms = jnp.mean(x * x, axis=-1, keepdims=True)
    xn = ((x * lax.rsqrt(ms + NORM_EPS)) * g_ref[...]).astype(BF16)
    k = 0
    for (a, b) in nat_cols:
        res = _dot(xn, w_ref[:, a:b])
        if k in regroup_outs:
            d = NSA_HEAD_DIM
            res_sc = scratch[0]
            per = LANES // d
            for c in range(res_sc.shape[0]):
                res_sc[c] = res[:, c * LANES:(c + 1) * LANES]
            for l in range(CMP_STRIDE):
                for c in range(res_sc.shape[0]):
                    rows = res_sc[c, pl.ds(l, tm // CMP_STRIDE, stride=CMP_STRIDE), :]
                    for j in range(per):
                        out_refs[k][c * per + j, :, l * d:(l + 1) * d] = rows[:, j * d:(j + 1) * d]
            k += 1
            continue
        if k == onehot_out:
            row = pl.program_id(0) * tm + lax.broadcasted_iota(jnp.int32, (tm, b - a), 0)
            blk = lax.shift_right_logical(lax.rem(row, seq_len), int(math.log2(SEL_BLOCK)))
            slot = lax.broadcasted_iota(jnp.int32, (tm, b - a), 1) & (LANES - 1)
            res = jnp.where(slot - NSA_HEAD_DIM == blk, 1.0, res)
        out_refs[k][...] = res.astype(out_refs[k].dtype)
        k += 1
    for (a, b) in t_rows:
        out_refs[k][...] = (_dot_nt(wt_ref[a:b, :], xn) + tb_ref[a:b, :]).astype(out_refs[k].dtype)
        k += 1


def _norm_proj(h2d, gain, w_nat, nat_spec, w_t, t_bias, t_spec, tm, onehot_out=-1, regroup_outs=(),
               seq_len=1):
    t_tokens, d = h2d.shape
    nat_cols, off = [], 0
    for width, _ in nat_spec:
        nat_cols.append((off, off + width))
        off += width
    t_rows, off = [], 0
    for rows, _ in t_spec:
        t_rows.append((off, off + rows))
        off += rows
    in_specs = [pl.BlockSpec((tm, d), lambda i: (i, 0)),
                pl.BlockSpec((1, d), lambda i: (0, 0))]
    args = [h2d, gain.reshape(1, d)]
    if nat_spec:
        in_specs.append(pl.BlockSpec(w_nat.shape, lambda i: (0, 0)))
        args.append(w_nat)
    if t_spec:
        in_specs.append(pl.BlockSpec(w_t.shape, lambda i: (0, 0)))
        in_specs.append(pl.BlockSpec((w_t.shape[0], 1), lambda i: (0, 0)))
        args += [w_t, t_bias.reshape(-1, 1).astype(F32)]
    out_shape, out_specs, scratch = [], [], []
    for k, (width, dt) in enumerate(nat_spec):
        if k in regroup_outs:
            slab = (NSA_KV_GROUPS, t_tokens // CMP_STRIDE, CMP_STRIDE * NSA_HEAD_DIM)
            out_shape.append(jax.ShapeDtypeStruct(slab, dt))
            out_specs.append(pl.BlockSpec((slab[0], tm // CMP_STRIDE, slab[2]), lambda i: (0, i, 0)))
            scratch = [pltpu.VMEM((width // LANES, tm, LANES), F32)]
        else:
            out_shape.append(jax.ShapeDtypeStruct((t_tokens, width), dt))
            out_specs.append(pl.BlockSpec((tm, width), lambda i: (i, 0)))
    for rows, dt in t_spec:
        out_shape.append(jax.ShapeDtypeStruct((rows, t_tokens), dt))
        out_specs.append(pl.BlockSpec((rows, tm), lambda i: (0, i)))
    return pl.pallas_call(
        functools.partial(_proj_kernel, nat_cols=tuple(nat_cols), t_rows=tuple(t_rows),
                          onehot_out=onehot_out, regroup_outs=tuple(regroup_outs), seq_len=seq_len),
        grid=(t_tokens // tm,),
        in_specs=in_specs,
        out_specs=out_specs,
        out_shape=out_shape,
        scratch_shapes=scratch,
        compiler_params=_params(("parallel",)),
        name="norm_proj",
    )(*args)


def _ssd_kernel(xbc_ref, z_ref, dt_ref, dtT_ref, cw_ref, cb_ref, dtb_ref, dtbT_ref,
                alog_ref, alogT_ref, dskip_ref, norm_ref, y_ref,
                state_sc, carry_sc, win_sc):
    L = SSD_CHUNK
    hp = LANES // SSD_HEAD_DIM
    gw = SSD_INNER // SSD_GROUPS
    heads_per_group = SSD_HEADS // SSD_GROUPS

    @pl.when(pl.program_id(1) == 0)
    def _():
        state_sc[...] = jnp.zeros_like(state_sc)
        carry_sc[...] = jnp.zeros_like(carry_sc)

    xraw = xbc_ref[0]
    win_sc[0:SUBLANES, :] = carry_sc[...]
    win_sc[SUBLANES:SUBLANES + L, :] = xraw
    carry_sc[...] = xraw[L - SUBLANES:L, :]
    conv = cw_ref[0:1, :] * win_sc[SUBLANES - 3:SUBLANES - 3 + L, :]
    for k in range(1, SSD_CONV):
        o = SUBLANES - (SSD_CONV - 1) + k
        conv = conv + cw_ref[k:k + 1, :] * win_sc[o:o + L, :]
    xact = _silu(conv + cb_ref[...])
    xs = xact[:, :SSD_INNER]
    bm = xact[:, SSD_INNER:SSD_INNER + SSD_GROUPS * SSD_STATE].astype(BF16)
    cm = xact[:, SSD_INNER + SSD_GROUPS * SSD_STATE:].astype(BF16)

    dt = _softplus(dt_ref[0] + dtb_ref[...])
    d_a = dt * (-jnp.exp(alog_ref[...]))
    rows = lax.broadcasted_iota(jnp.int32, (L, L), 0)
    cols = lax.broadcasted_iota(jnp.int32, (L, L), 1)
    causal = rows >= cols
    cum = _dot(causal.astype(F32), d_a, precision=lax.Precision.HIGHEST)
    dt_t = _softplus(dtT_ref[...] + dtbT_ref[...])
    d_a_t = dt_t * (-jnp.exp(alogT_ref[...]))
    cum_t = _dot(d_a_t, (rows <= cols).astype(F32), precision=lax.Precision.HIGHEST)
    ecum = jnp.exp(cum)
    cum_last = cum[L - 1:L, :]
    to_end = jnp.exp(cum_last - cum) * dt
    e_last = jnp.exp(cum_last)

    lane = lax.broadcasted_iota(jnp.int32, (L, LANES), 1)
    first_head = lane < SSD_HEAD_DIM
    lane1 = lax.broadcasted_iota(jnp.int32, (1, LANES), 1)
    first_head1 = lane1 < SSD_HEAD_DIM

    def per_head(arr, h0, mask):
        return jnp.where(mask, arr[:, h0:h0 + 1], arr[:, h0 + 1:h0 + 2])

    y_blocks, ecum_blocks, toend_blocks, elast_blocks = [], [], [], []
    for g in range(SSD_GROUPS):
        bm_g = bm[:, g * SSD_STATE:(g + 1) * SSD_STATE]
        cm_g = cm[:, g * SSD_STATE:(g + 1) * SSD_STATE]
        cb = _dot_nt(cm_g, bm_g)
        for j in range(heads_per_group // hp):
            h0 = g * heads_per_group + j * hp
            c0 = h0 * SSD_HEAD_DIM
            x_pair = xs[:, c0:c0 + LANES].astype(BF16)
            res = []
            for hh in range(hp):
                h = h0 + hh
                seg = cum[:, h:h + 1] - cum_t[h:h + 1, :]
                decay = jnp.exp(jnp.where(causal, seg, -jnp.inf))
                wts = (cb * decay) * dt_t[h:h + 1, :]
                res.append(_dot(wts.astype(BF16), x_pair))
            y_blocks.append(jnp.where(first_head, res[0], res[1]))
            ecum_blocks.append(per_head(ecum, h0, first_head))
            toend_blocks.append(per_head(to_end, h0, first_head))
            elast_blocks.append(per_head(e_last, h0, first_head1))
    y_diag = jnp.concatenate(y_blocks, axis=1)
    ecum_x = jnp.concatenate(ecum_blocks, axis=1)
    toend_x = jnp.concatenate(toend_blocks, axis=1)
    elast_x = jnp.concatenate(elast_blocks, axis=1)

    xw = (xs * toend_x).astype(BF16)
    y_off_blocks = []
    for g in range(SSD_GROUPS):
        bm_g = bm[:, g * SSD_STATE:(g + 1) * SSD_STATE]
        cm_g = cm[:, g * SSD_STATE:(g + 1) * SSD_STATE]
        st = state_sc[g]
        y_off_blocks.append(_dot(cm_g, st.astype(BF16)))
        state_sc[g] = st * elast_x[:, g * gw:(g + 1) * gw] + \
            _dot_tn(bm_g, xw[:, g * gw:(g + 1) * gw])
    y_off = jnp.concatenate(y_off_blocks, axis=1) * ecum_x

    y = y_diag + y_off + dskip_ref[...] * xs
    yg = y * _silu(z_ref[0])
    outs = []
    for g in range(SSD_GROUPS):
        yg_g = yg[:, g * gw:(g + 1) * gw]
        ms = jnp.mean(yg_g * yg_g, axis=-1, keepdims=True)
        outs.append(yg_g * lax.rsqrt(ms + NORM_EPS))
    y_ref[0] = (jnp.concatenate(outs, axis=1) * norm_ref[...]).astype(y_ref.dtype)


def _ssd(xbc, z, dt_nat, dt_t, conv_w, conv_b, dt_bias, a_log, d_skip, ssd_norm, bsz, s):
    nch = s // SSD_CHUNK
    L = SSD_CHUNK
    pad = LANES - SSD_HEADS
    dtb = jnp.pad(dt_bias, (0, pad)).reshape(1, LANES)
    alog = jnp.pad(a_log, (0, pad)).reshape(1, LANES)
    dskip_x = jnp.repeat(d_skip, SSD_HEAD_DIM).reshape(1, SSD_INNER)
    const = lambda shape: pl.BlockSpec(shape, lambda b, c: tuple(0 for _ in shape))
    return pl.pallas_call(
        _ssd_kernel,
        grid=(bsz, nch),
        in_specs=[
            pl.BlockSpec((1, L, SSD_XBC), lambda b, c: (b, c, 0)),
            pl.BlockSpec((1, L, SSD_INNER), lambda b, c: (b, c, 0)),
            pl.BlockSpec((1, L, LANES), lambda b, c: (b, c, 0)),
            pl.BlockSpec((SSD_HEADS, L), lambda b, c: (0, b * nch + c)),
            const((SSD_CONV, SSD_XBC)), const((1, SSD_XBC)),
            const((1, LANES)), const((SSD_HEADS, 1)),
            const((1, LANES)), const((SSD_HEADS, 1)),
            const((1, SSD_INNER)), const((1, SSD_INNER)),
        ],
        out_specs=pl.BlockSpec((1, L, SSD_INNER), lambda b, c: (b, c, 0)),
        out_shape=jax.ShapeDtypeStruct((bsz, s, SSD_INNER), BF16),
        scratch_shapes=[
            pltpu.VMEM((SSD_GROUPS, SSD_STATE, SSD_INNER // SSD_GROUPS), F32),
            pltpu.VMEM((SUBLANES, SSD_XBC), F32),
            pltpu.VMEM((SUBLANES + L, SSD_XBC), F32),
        ],
        compiler_params=_params(("parallel", "arbitrary")),
        name="ssd_scan",
    )(xbc.reshape(bsz, s, SSD_XBC), z.reshape(bsz, s, SSD_INNER),
      dt_nat.reshape(bsz, s, LANES), dt_t,
      conv_w, conv_b.reshape(1, SSD_XBC), dtb, dt_bias.reshape(SSD_HEADS, 1),
      alog, a_log.reshape(SSD_HEADS, 1), dskip_x, ssd_norm.reshape(1, SSD_INNER))


def _conf_kernel(glu_ref, zc_ref, w_ref, b_ref, lng_ref, lnb_ref, y_ref, sh_sc, u_sc):
    ts = glu_ref.shape[1]
    n = sh_sc.shape[1]
    halo = n - ts
    first = halo - (CONF_CONV_WIDTH - 1)

    @pl.when(pl.program_id(1) == 0)
    def _():
        sh_sc[0, 0:halo, :] = jnp.zeros((halo, CONF_WIDTH), F32)

    sh_sc[0, halo:n, :] = glu_ref[0, :, :CONF_WIDTH] * jax.nn.sigmoid(glu_ref[0, :, CONF_WIDTH:])
    tiles = sh_sc[0].reshape(n // SUBLANES, SUBLANES, CONF_WIDTH)
    sub = lax.broadcasted_iota(jnp.int32, (n // SUBLANES - 1, SUBLANES, CONF_WIDTH), 1)
    for s in range(1, SUBLANES):
        rolled = pltpu.roll(tiles, SUBLANES - s, 1)
        shifted = jnp.where(sub < SUBLANES - s, rolled[:-1], rolled[1:])
        sh_sc[s, 0:n - SUBLANES, :] = shifted.reshape(n - SUBLANES, CONF_WIDTH)
    for r0 in range(0, ts, CONF_ROW_CHUNK):
        for c0 in range(0, CONF_WIDTH, CONF_LANE_CHUNK):
            cs = slice(c0, c0 + CONF_LANE_CHUNK)
            acc = None
            for k in range(CONF_CONV_WIDTH):
                a, s = divmod(first + k, SUBLANES)
                o = r0 + a * SUBLANES
                window = sh_sc[s, o:o + CONF_ROW_CHUNK, cs]
                term = w_ref[k][None, :, cs] * window.reshape(CONF_ROW_CHUNK // SUBLANES, SUBLANES, -1)
                acc = term if acc is None else acc + term
            u_sc[r0:r0 + CONF_ROW_CHUNK, cs] = acc.reshape(CONF_ROW_CHUNK, -1)
    sh_sc[0, 0:halo, :] = sh_sc[0, ts:n, :]
    u = u_sc[...] + b_ref[...]
    mu = jnp.mean(u, axis=-1, keepdims=True)
    var = jnp.mean(jnp.square(u - mu), axis=-1, keepdims=True)
    un = (u - mu) * lax.rsqrt(var + NORM_EPS) * lng_ref[...] + lnb_ref[...]
    y_ref[0] = (_silu(un) * _silu(zc_ref[0])).astype(y_ref.dtype)


def _conformer(glu, zc, conv_w, conv_b, ln_g, ln_b, bsz, s):
    ts = min(CONF_ROWS, s)
    halo = 32
    w_tiles = jnp.broadcast_to(conv_w[:, None, :], (CONF_CONV_WIDTH, SUBLANES, CONF_WIDTH))
    const = lambda shape: pl.BlockSpec(shape, lambda b, i: tuple(0 for _ in shape))
    return pl.pallas_call(
        _conf_kernel,
        grid=(bsz, s // ts),
        in_specs=[
            pl.BlockSpec((1, ts, 2 * CONF_WIDTH), lambda b, i: (b, i, 0)),
            pl.BlockSpec((1, ts, CONF_WIDTH), lambda b, i: (b, i, 0)),
            const(w_tiles.shape), const((1, CONF_WIDTH)),
            const((1, CONF_WIDTH)), const((1, CONF_WIDTH)),
        ],
        out_specs=pl.BlockSpec((1, ts, CONF_WIDTH), lambda b, i: (b, i, 0)),
        out_shape=jax.ShapeDtypeStruct((bsz, s, CONF_WIDTH), BF16),
        scratch_shapes=[pltpu.VMEM((SUBLANES, halo + ts, CONF_WIDTH), F32),
                        pltpu.VMEM((ts, CONF_WIDTH), F32)],
        compiler_params=_params(("parallel", "arbitrary")),
        name="conformer",
    )(glu.reshape(bsz, s, 2 * CONF_WIDTH), zc.reshape(bsz, s, CONF_WIDTH), w_tiles,
      conv_b.reshape(1, -1), ln_g.reshape(1, -1), ln_b.reshape(1, -1))


def _outproj_even_kernel(h_ref, a_ref, b_ref, wa_ref, wb_ref, o_ref):
    o_ref[...] = h_ref[...] + (_dot(a_ref[...], wa_ref[...]) + _dot(b_ref[...], wb_ref[...]))


def _outproj_even(h2d, ya, yb, w_out, tm):
    t_tokens, d = h2d.shape
    wa = w_out[:SSD_INNER].astype(BF16)
    wb = w_out[SSD_INNER:].astype(BF16)
    row = lambda w: pl.BlockSpec((tm, w), lambda i: (i, 0))
    return pl.pallas_call(
        _outproj_even_kernel,
        grid=(t_tokens // tm,),
        in_specs=[row(d), row(SSD_INNER), row(CONF_WIDTH),
                  pl.BlockSpec(wa.shape, lambda i: (0, 0)),
                  pl.BlockSpec(wb.shape, lambda i: (0, 0))],
        out_specs=row(d),
        out_shape=jax.ShapeDtypeStruct((t_tokens, d), F32),
        compiler_params=_params(("parallel",)),
        name="outproj_even",
    )(h2d, ya, yb, wa, wb)


def _outproj_odd_kernel(h_ref, oc_ref, os_ref, ow_ref, zt_ref, w_ref, g_ref, o_ref, *, final):
    o = (oc_ref[...].astype(F32) + os_ref[...].astype(F32)) + ow_ref[...].astype(F32)
    y = (o * _silu(zt_ref[...])).astype(BF16)
    out = h_ref[...] + _dot_tn(y, w_ref[...])
    if final:
        ms = jnp.mean(out * out, axis=-1, keepdims=True)
        out = (out * lax.rsqrt(ms + NORM_EPS)) * g_ref[...]
    o_ref[...] = out


def _outproj_odd(h2d, oc, osel, ow, z_t, w_out, final_gain, final, tm):
    t_tokens, d = h2d.shape
    w = w_out.astype(BF16)
    col = pl.BlockSpec((NSA_WIDTH, tm), lambda i: (0, i))
    return pl.pallas_call(
        functools.partial(_outproj_odd_kernel, final=final),
        grid=(t_tokens // tm,),
        in_specs=[pl.BlockSpec((tm, d), lambda i: (i, 0)), col, col, col, col,
                  pl.BlockSpec(w.shape, lambda i: (0, 0)),
                  pl.BlockSpec((1, d), lambda i: (0, 0))],
        out_specs=pl.BlockSpec((tm, d), lambda i: (i, 0)),
        out_shape=jax.ShapeDtypeStruct((t_tokens, d), F32),
        compiler_params=_params(("parallel",)),
        name="outproj_odd",
    )(h2d, oc, osel, ow, z_t, w, final_gain.reshape(1, d))


def _compress_kernel(xk_ref, xv_ref, pek_ref, pev_ref, w1k_ref, w1v_ref, w2k_ref, w2vt_ref,
                     kc_ref, vct_ref):
    half = CMP_STRIDE * NSA_HEAD_DIM

    def hidden(x_ref, pe_ref, w1_ref):
        x = x_ref[0, 0]
        n = x.shape[0]
        lo = (x + pe_ref[0:1, :]).astype(BF16)
        hi = (x + pe_ref[1:2, :]).astype(BF16)
        h = _dot(lo, w1_ref[0:half, :]) + pltpu.roll(_dot(hi, w1_ref[half:2 * half, :]), n - 1, 0)
        return _silu(h).astype(BF16)

    kc_ref[0, 0] = _dot(hidden(xk_ref, pek_ref, w1k_ref), w2k_ref[...]).astype(kc_ref.dtype)
    vct_ref[0, 0] = _dot_nt(w2vt_ref[...], hidden(xv_ref, pev_ref, w1v_ref)).astype(vct_ref.dtype)


def _compress(kc, vc, pe_k, w1_k, w2_k, pe_v, w1_v, w2_v, bsz, s):
    g, d = NSA_KV_GROUPS, NSA_HEAD_DIM
    nh = s // CMP_STRIDE
    half = CMP_STRIDE * d

    def regroup(t):
        return t.reshape(g, bsz, nh, half)

    const = lambda shape: pl.BlockSpec(shape, lambda b, gi: tuple(0 for _ in shape))
    blk = pl.BlockSpec((1, 1, nh, half), lambda b, gi: (gi, b, 0, 0))
    return pl.pallas_call(
        _compress_kernel,
        grid=(bsz, g),
        in_specs=[blk, blk, const((2, half)), const((2, half)),
                  const((2 * half, CMP_HIDDEN)), const((2 * half, CMP_HIDDEN)),
                  const((CMP_HIDDEN, d)), const((d, CMP_HIDDEN))],
        out_specs=[pl.BlockSpec((1, 1, nh, d), lambda b, gi: (b, gi, 0, 0)),
                   pl.BlockSpec((1, 1, d, nh), lambda b, gi: (b, gi, 0, 0))],
        out_shape=[jax.ShapeDtypeStruct((bsz, g, nh, d), BF16),
                   jax.ShapeDtypeStruct((bsz, g, d, nh), BF16)],
        compiler_params=_params(("parallel", "parallel")),
        name="compress_kv",
    )(regroup(kc), regroup(vc), pe_k.reshape(2, half), pe_v.reshape(2, half),
      w1_k.astype(BF16), w1_v.astype(BF16), w2_k.astype(BF16), w2_v.T.astype(BF16))


def _selection_bias(imp, t0, topk, v_sc, cnt_sc):
    n_sel, tq = imp.shape
    shift = int(math.log2(SEL_BLOCK))
    ngroups = n_sel // SUBLANES
    jidx = lax.broadcasted_iota(jnp.int32, (n_sel, tq), 0)
    t = t0 + lax.broadcasted_iota(jnp.int32, (n_sel, tq), 1)
    cur = lax.shift_right_logical(t, shift)
    valid = jidx * SEL_BLOCK <= t
    forced = (jidx == 0) | (jidx == cur) | (jidx == cur - 1)
    v_sc[...] = jnp.where(valid, imp + jnp.where(forced, FORCE_BONUS, 0.0), -jnp.inf)
    cnt_sc[...] = jnp.zeros((n_sel, tq), jnp.int32)
    last_block = lax.shift_right_logical(t0 + tq - 1, shift)
    last_group = lax.shift_right_logical(last_block, int(math.log2(SUBLANES)))
    sub = lax.broadcasted_iota(jnp.int32, (SUBLANES, tq), 0)

    def group(ref, gi):
        return ref[gi * SUBLANES:(gi + 1) * SUBLANES, :]

    def rows_of(vals):
        return [jnp.broadcast_to(vals[ii:ii + 1, :], (SUBLANES, tq)) for ii in range(SUBLANES)]

    for hi in range(ngroups):
        @pl.when((hi <= last_group) & (last_block >= topk))
        def _():
            v_hi = group(v_sc, hi)
            rows_hi = rows_of(v_hi)
            own = jnp.zeros((SUBLANES, tq), jnp.int32)
            for ii in range(SUBLANES):
                own = own + jnp.where(sub > ii, jnp.where(rows_hi[ii] >= v_hi, 1, 0),
                                      jnp.where(rows_hi[ii] > v_hi, 1, 0))
            for lo in range(hi):
                v_lo = group(v_sc, lo)
                add_lo = jnp.zeros((SUBLANES, tq), jnp.int32)
                for ii in range(SUBLANES):
                    add_lo = add_lo + jnp.where(rows_hi[ii] > v_lo, 1, 0)
                cnt_sc[lo * SUBLANES:(lo + 1) * SUBLANES, :] += add_lo
                for row in rows_of(v_lo):
                    own = own + jnp.where(row >= v_hi, 1, 0)
            cnt_sc[hi * SUBLANES:(hi + 1) * SUBLANES, :] += own

    return jnp.where(valid & (cnt_sc[...] < topk), 0.0, NEG_BIG)


def _cmp_attn_kernel(qt_ref, kc_ref, vct_ref, ovt_ref, gl_ref, gb_ref, o_ref, selb_ref,
                     v_sc, cnt_sc, *, n_cmp, topk):
    tq = qt_ref.shape[2]
    nc = kc_ref.shape[2]
    t0 = pl.program_id(2) * tq
    cidx = lax.broadcasted_iota(jnp.int32, (nc, tq), 0)
    t = t0 + lax.broadcasted_iota(jnp.int32, (nc, tq), 1)
    cmask = (cidx * CMP_STRIDE + (CMP_BLOCK - 1) <= t) & (cidx < n_cmp)
    kc = kc_ref[0, 0]
    vct = vct_ref[0, 0]
    gates = jax.nn.sigmoid(gl_ref[0] + gb_ref[0])
    imp = jnp.zeros((ovt_ref.shape[0], tq), F32)
    for r in range(NSA_HPG):
        s = jnp.where(cmask, _dot(kc, qt_ref[r]), -jnp.inf)
        m = jnp.max(s, axis=0, keepdims=True)
        m = jnp.where(m > -jnp.inf, m, 0.0)
        e = jnp.exp(s - m)
        p = e * (1.0 / jnp.maximum(jnp.sum(e, axis=0, keepdims=True), 1e-30))
        pb = p.astype(BF16)
        o_ref[r] = (_dot(vct, pb) * gates[r:r + 1, :]).astype(o_ref.dtype)
        imp = imp + _dot(ovt_ref[...], pb)
    bias = _selection_bias(imp, t0, topk, v_sc, cnt_sc)
    n_pad = selb_ref.shape[2] - bias.shape[0]
    if n_pad:
        bias = jnp.concatenate([bias, jnp.zeros((n_pad, tq), F32)], axis=0)
    selb_ref[0, 0] = bias.astype(selb_ref.dtype)


def _gate_specs(nq):
    def spec(branch):
        return (pl.BlockSpec((1, SUBLANES, ATT_TQ), lambda b, g, i: (g * 3 + branch, 0, b * nq + i)),
                pl.BlockSpec((1, SUBLANES, 1), lambda b, g, i: (g * 3 + branch, 0, 0)))
    return spec


def _cmp_attn(q_t, kcmp, vcmp_t, gl_t, gbias, bsz, s):
    g, d = NSA_KV_GROUPS, NSA_HEAD_DIM
    tq = ATT_TQ
    nq = s // tq
    nc = s // CMP_STRIDE
    n_cmp = (s - CMP_BLOCK) // CMP_STRIDE + 1
    n_sel = s // SEL_BLOCK
    n_selp = _sel_pad(n_sel)
    topk = min(SEL_TOPK, n_sel)
    cmp_start = np.arange(nc) * CMP_STRIDE
    sel_start = np.arange(n_sel) * SEL_BLOCK
    ov = ((cmp_start[:, None] < sel_start[None, :] + SEL_BLOCK) &
          (cmp_start[:, None] + CMP_BLOCK > sel_start[None, :]) &
          (np.arange(nc)[:, None] < n_cmp))
    ovt = jnp.asarray(ov.T.astype(np.float32), dtype=BF16)
    gl_spec, gb_spec = _gate_specs(nq)(0)
    t_tokens = bsz * s
    return pl.pallas_call(
        functools.partial(_cmp_attn_kernel, n_cmp=n_cmp, topk=topk),
        grid=(bsz, g, nq),
        in_specs=[
            pl.BlockSpec((NSA_HPG, d, tq), lambda b, gi, i: (gi, 0, b * nq + i)),
            pl.BlockSpec((1, 1, nc, d), lambda b, gi, i: (b, gi, 0, 0)),
            pl.BlockSpec((1, 1, d, nc), lambda b, gi, i: (b, gi, 0, 0)),
            pl.BlockSpec((n_sel, nc), lambda b, gi, i: (0, 0)),
            gl_spec, gb_spec,
        ],
        out_specs=[pl.BlockSpec((NSA_HPG, d, tq), lambda b, gi, i: (gi, 0, b * nq + i)),
                   pl.BlockSpec((1, 1, n_selp, tq), lambda b, gi, i: (b, gi, 0, i))],
        out_shape=[jax.ShapeDtypeStruct((NSA_HEADS, d, t_tokens), BF16),
                   jax.ShapeDtypeStruct((bsz, g, n_selp, s), BF16)],
        scratch_shapes=[pltpu.VMEM((n_sel, tq), F32), pltpu.VMEM((n_sel, tq), jnp.int32)],
        compiler_params=_params(("parallel", "parallel", "parallel")),
        name="cmp_attn_select",
    )(q_t, kcmp, vcmp_t, ovt, gl_t, gbias)


def _sel_pad(n_sel):
    return -(-n_sel // NSA_HEAD_DIM) * NSA_HEAD_DIM


def _sel_attn_kernel(qt_ref, ka_ref, va_ref, selb_ref, gl_ref, gb_ref, o_ref,
                     qa_sc, sa_sc, sb_sc, ma_sc, mb_sc, m_sc, acc_sc, *, tk):
    tq = qt_ref.shape[2]
    d = NSA_HEAD_DIM
    w = NSA_HPG * tq
    i = pl.program_id(2)
    t0 = i * tq
    last = (i * tq) // tk
    for r in range(NSA_HPG):
        qa_sc[0:d, r * tq:(r + 1) * tq] = qt_ref[r]
        qa_sc[d:, r * tq:(r + 1) * tq] = selb_ref[0, 0]
    m_sc[...] = jnp.full(m_sc.shape, NEG_BIG, F32)
    acc_sc[...] = jnp.zeros(acc_sc.shape, F32)

    def scores(j, s_ref, mx_ref, diagonal):
        k0 = pl.multiple_of(j * tk, tk)
        s = _dot(ka_ref[pl.ds(k0, tk), :], qa_sc[...])
        if diagonal:
            kpos = k0 + lax.broadcasted_iota(jnp.int32, (tk, w), 0)
            t = t0 + (lax.broadcasted_iota(jnp.int32, (tk, w), 1) & (tq - 1))
            s = jnp.where(kpos <= t, s, NEG_BIG)
        s_ref[...] = s
        mx_ref[...] = jnp.max(s, axis=0, keepdims=True)

    def accumulate(j, s_ref, mx_ref):
        k0 = pl.multiple_of(j * tk, tk)
        m_old = m_sc[...]
        m_new = jnp.maximum(m_old, mx_ref[...])
        p = jnp.exp(s_ref[...] - m_new).astype(BF16)
        acc_sc[...] = jnp.exp(m_old - m_new) * acc_sc[...] + _dot(va_ref[0, :, pl.ds(k0, tk)], p)
        m_sc[...] = m_new

    scores(last, sa_sc, ma_sc, True)

    def pair(mi, carry):
        j = 2 * mi
        scores(j, sb_sc, mb_sc, False)
        accumulate(jnp.where(mi == 0, last, j - 1), sa_sc, ma_sc)
        scores(j + 1, sa_sc, ma_sc, False)
        accumulate(j, sb_sc, mb_sc)
        return carry

    lax.fori_loop(0, last // 2, pair, 0)

    @pl.when(last % 2 == 1)
    def _():
        scores(last - 1, sb_sc, mb_sc, False)
        accumulate(jnp.where(last == 1, last, last - 2), sa_sc, ma_sc)
        accumulate(last - 1, sb_sc, mb_sc)

    @pl.when(last % 2 == 0)
    def _():
        accumulate(jnp.where(last == 0, last, last - 1), sa_sc, ma_sc)

    gates = jax.nn.sigmoid(gl_ref[0] + gb_ref[0])
    for r in range(NSA_HPG):
        a = acc_sc[:, r * tq:(r + 1) * tq]
        o_ref[r] = (a[0:d] * (1.0 / a[d:d + 1]) * gates[r:r + 1, :]).astype(o_ref.dtype)


def _sel_attn(q_t, ka, va, selb, gl_t, gbias, bsz, s):
    g, d = NSA_KV_GROUPS, NSA_HEAD_DIM
    tq = ATT_TQ
    tk = min(SEL_TK, s)
    nq = s // tq
    w = NSA_HPG * tq
    n_selp = selb.shape[2]
    kd = d + n_selp
    gl_spec, gb_spec = _gate_specs(nq)(1)
    return pl.pallas_call(
        functools.partial(_sel_attn_kernel, tk=tk),
        grid=(bsz, g, nq),
        in_specs=[
            pl.BlockSpec((NSA_HPG, d, tq), lambda b, gi, i: (gi, 0, b * nq + i)),
            pl.BlockSpec((s, kd), lambda b, gi, i: (b, gi)),
            pl.BlockSpec((1, V_AUG_ROWS, s), lambda b, gi, i: (gi, 0, b)),
            pl.BlockSpec((1, 1, n_selp, tq), lambda b, gi, i: (b, gi, 0, i)),
            gl_spec, gb_spec,
        ],
        out_specs=pl.BlockSpec((NSA_HPG, d, tq), lambda b, gi, i: (gi, 0, b * nq + i)),
        out_shape=jax.ShapeDtypeStruct((NSA_HEADS, d, bsz * s), BF16),
        scratch_shapes=[pltpu.VMEM((kd, w), BF16),
                        pltpu.VMEM((tk, w), F32), pltpu.VMEM((tk, w), F32),
                        pltpu.VMEM((1, w), F32), pltpu.VMEM((1, w), F32),
                        pltpu.VMEM((1, w), F32), pltpu.VMEM((V_AUG_ROWS, w), F32)],
        compiler_params=_params(("parallel", "parallel", "arbitrary")),
        name="sel_attn",
    )(q_t, ka, va, selb, gl_t, gbias)


def _win_attn_kernel(*refs, nblk):
    qt_ref = refs[0]
    k_refs = refs[1:1 + nblk]
    v_refs = refs[1 + nblk:1 + 2 * nblk]
    gl_ref, gb_ref, o_ref, qa_sc = refs[1 + 2 * nblk:]
    tq = qt_ref.shape[2]
    d = NSA_HEAD_DIM
    w = NSA_HPG * tq
    i = pl.program_id(2)
    qa_sc[d:, :] = jnp.zeros((qa_sc.shape[0] - d, w), qa_sc.dtype)
    for r in range(NSA_HPG):
        qa_sc[0:d, r * tq:(r + 1) * tq] = qt_ref[r]
    krow = lax.broadcasted_iota(jnp.int32, (tq, tq), 0)
    qcol = lax.broadcasted_iota(jnp.int32, (tq, tq), 1)
    rel0 = qcol - krow
    biases = []
    for jj in range(nblk):
        back = nblk - 1 - jj
        if back == 0:
            biases.append(jnp.where(rel0 >= 0, 0.0, NEG_BIG))
        else:
            limit = jnp.where(i >= back, WINDOW, -1)
            biases.append(jnp.where(rel0 + back * tq < limit, 0.0, NEG_BIG))
    gates = jax.nn.sigmoid(gl_ref[0] + gb_ref[0])
    hc = WIN_HEAD_CHUNK
    for c in range(NSA_HPG // hc):
        q = qa_sc[:, c * hc * tq:(c + 1) * hc * tq]
        s_blocks = [_dot(k_refs[jj][...], q) + pltpu.repeat(biases[jj], hc, 1) for jj in range(nblk)]
        m = jnp.max(s_blocks[0], axis=0, keepdims=True)
        for s in s_blocks[1:]:
            m = jnp.maximum(m, jnp.max(s, axis=0, keepdims=True))
        acc = _dot(v_refs[0][0], jnp.exp(s_blocks[0] - m).astype(BF16))
        for jj in range(1, nblk):
            acc = acc + _dot(v_refs[jj][0], jnp.exp(s_blocks[jj] - m).astype(BF16))
        for r in range(hc):
            a = acc[:, r * tq:(r + 1) * tq]
            h = c * hc + r
            o_ref[h] = (a[0:d] * (1.0 / a[d:d + 1]) * gates[h:h + 1, :]).astype(o_ref.dtype)


def _win_attn(q_t, ka, va, gl_t, gbias, bsz, s):
    g, d = NSA_KV_GROUPS, NSA_HEAD_DIM
    tq = ATT_TQ
    nq = s // tq
    nblk = WINDOW // tq + 1
    gl_spec, gb_spec = _gate_specs(nq)(2)

    def k_spec(jj):
        return pl.BlockSpec((tq, LANES), lambda b, gi, i: (b * nq + jnp.maximum(i - (nblk - 1) + jj, 0), gi))

    def v_spec(jj):
        return pl.BlockSpec((1, V_AUG_ROWS, tq),
                            lambda b, gi, i: (gi, 0, b * nq + jnp.maximum(i - (nblk - 1) + jj, 0)))

    return pl.pallas_call(
        functools.partial(_win_attn_kernel, nblk=nblk),
        grid=(bsz, g, nq),
        in_specs=[pl.BlockSpec((NSA_HPG, d, tq), lambda b, gi, i: (gi, 0, b * nq + i))]
        + [k_spec(jj) for jj in range(nblk)] + [v_spec(jj) for jj in range(nblk)]
        + [gl_spec, gb_spec],
        out_specs=pl.BlockSpec((NSA_HPG, d, tq), lambda b, gi, i: (gi, 0, b * nq + i)),
        out_shape=jax.ShapeDtypeStruct((NSA_HEADS, d, bsz * s), BF16),
        scratch_shapes=[pltpu.VMEM((LANES, NSA_HPG * tq), BF16)],
        compiler_params=_params(("parallel", "parallel", "parallel")),
        name="win_attn",
    )(q_t, *([ka] * nblk), *([va] * nblk), gl_t, gbias)


def _even_layer(h2d, bsz, s, gain, w_in, ssd_conv_w, ssd_conv_b, dt_bias, a_log, d_skip, ssd_norm,
                conf_conv_w, conf_conv_b, conf_ln_g, conf_ln_b, w_out):
    o1 = SSD_INNER
    o2 = o1 + SSD_XBC
    o3 = o2 + SSD_HEADS
    o4 = o3 + 2 * CONF_WIDTH
    w_dt = w_in[:, o2:o3]
    w_nat = jnp.concatenate(
        [w_in[:, :o2], w_in[:, o3:], jnp.pad(w_dt, ((0, 0), (0, LANES - SSD_HEADS)))], axis=1).astype(BF16)
    nat_spec = [(SSD_INNER, F32), (SSD_XBC, F32), (2 * CONF_WIDTH, F32), (CONF_WIDTH, F32), (LANES, F32)]
    tm = min(PROJ_ROWS, s)
    z, xbc, glu, zc, dt_nat, dt_t = _norm_proj(
        h2d, gain, w_nat, nat_spec, w_dt.T.astype(BF16), jnp.zeros((SSD_HEADS,), F32),
        [(SSD_HEADS, F32)], tm)
    y_a = _ssd(xbc, z, dt_nat, dt_t, ssd_conv_w, ssd_conv_b, dt_bias, a_log, d_skip, ssd_norm, bsz, s)
    y_b = _conformer(glu, zc, conf_conv_w, conf_conv_b, conf_ln_g, conf_ln_b, bsz, s)
    return _outproj_even(h2d, y_a.reshape(bsz * s, SSD_INNER), y_b.reshape(bsz * s, CONF_WIDTH), w_out, tm)


def _odd_layer(h2d, bsz, s, gain, w_in, gate_bias, pe_k, w1_k, w2_k, pe_v, w1_v, w2_v, w_out,
               final_gain, final):
    g, r, d = NSA_KV_GROUPS, NSA_HPG, NSA_HEAD_DIM
    t_tokens = bsz * s
    sizes = [NSA_WIDTH] + [KV_WIDTH] * 6 + [3 * NSA_HEADS, NSA_WIDTH]
    offs = np.cumsum([0] + sizes)
    col = lambda k: w_in[:, offs[k]:offs[k + 1]]
    w_q, w_kc, w_vc, w_ks, w_vs, w_kw, w_vw, w_gl, w_z = [col(k) for k in range(9)]
    perm = np.array([[[(gi * r + ri) * 3 + br for ri in range(r)] for br in range(3)] for gi in range(g)])
    w_gl_g = jnp.pad(w_gl[:, perm.reshape(-1)].reshape(D_MODEL, g * 3, r),
                     ((0, 0), (0, 0), (0, SUBLANES - r))).reshape(D_MODEL, g * 3 * SUBLANES)
    gbias = jnp.pad(gate_bias[perm.reshape(-1)].reshape(g * 3, r), ((0, 0), (0, SUBLANES - r)))
    gbias = gbias.reshape(g * 3, SUBLANES, 1)
    scale = NSA_HEAD_DIM ** -0.5
    assert d + _sel_pad(s // SEL_BLOCK) == LANES

    def key_slots(w):
        return jnp.pad(w.reshape(D_MODEL, g, d), ((0, 0), (0, 0), (0, LANES - d))).reshape(D_MODEL, g * LANES)

    def value_rows(w):
        return jnp.pad(w.reshape(D_MODEL, g, d), ((0, 0), (0, 0), (0, V_AUG_ROWS - d))).reshape(D_MODEL, g * V_AUG_ROWS)

    ones_row = jnp.zeros((g, V_AUG_ROWS), F32).at[:, d].set(1.0).reshape(-1)
    w_nat = jnp.concatenate([w_kc, w_vc, key_slots(w_ks), key_slots(w_kw)], axis=1).astype(BF16)
    w_t = jnp.concatenate([w_q * scale, value_rows(w_vs), value_rows(w_vw), w_gl_g, w_z], axis=1).T.astype(BF16)
    t_bias = jnp.concatenate([jnp.zeros((NSA_WIDTH,), F32), ones_row, ones_row,
                              jnp.zeros((g * 3 * SUBLANES + NSA_WIDTH,), F32)])
    nat_spec = [(KV_WIDTH, F32), (KV_WIDTH, F32), (g * LANES, BF16), (g * LANES, BF16)]
    t_spec = [(NSA_WIDTH, BF16), (g * V_AUG_ROWS, BF16), (g * V_AUG_ROWS, BF16),
              (g * 3 * SUBLANES, F32), (NSA_WIDTH, F32)]
    tm = min(PROJ_ROWS, s)
    kc, vc, ks_a, kw_a, q_t, vs_a, vw_a, gl_t, z_t = _norm_proj(
        h2d, gain, w_nat, nat_spec, w_t, t_bias, t_spec, tm, onehot_out=2, regroup_outs=(0, 1), seq_len=s)

    q_t = q_t.reshape(NSA_HEADS, d, t_tokens)
    gl_t = gl_t.reshape(g * 3, SUBLANES, t_tokens)
    vs_a = vs_a.reshape(g, V_AUG_ROWS, t_tokens)
    vw_a = vw_a.reshape(g, V_AUG_ROWS, t_tokens)

    kcmp, vcmp_t = _compress(kc, vc, pe_k, w1_k, w2_k, pe_v, w1_v, w2_v, bsz, s)
    o_cmp, selb = _cmp_attn(q_t, kcmp, vcmp_t, gl_t, gbias, bsz, s)
    o_sel = _sel_attn(q_t, ks_a, vs_a, selb, gl_t, gbias, bsz, s)
    o_win = _win_attn(q_t, kw_a, vw_a, gl_t, gbias, bsz, s)
    flat = lambda o: o.reshape(NSA_WIDTH, t_tokens)
    return _outproj_odd(h2d, flat(o_cmp), flat(o_sel), flat(o_win), z_t, w_out, final_gain, final, tm)


def kernel(x, e_norm, e_w_in, e_ssd_conv_w, e_ssd_conv_b, e_dt_bias, e_a_log, e_d_skip, e_ssd_norm,
           e_conf_conv_w, e_conf_conv_b, e_conf_ln_g, e_conf_ln_b, e_w_out, o_norm, o_w_in, o_gate_bias,
           o_cmp_pe_k, o_cmp_w1_k, o_cmp_w2_k, o_cmp_pe_v, o_cmp_w1_v, o_cmp_w2_v, o_w_out, final_norm):
    bsz, s, d = x.shape
    depth = e_norm.shape[0] + o_norm.shape[0]
    h = x.reshape(bsz * s, d)
    for layer in range(depth):
        i = layer // 2
        if layer % 2 == 0:
            h = _even_layer(h, bsz, s, e_norm[i], e_w_in[i], e_ssd_conv_w[i], e_ssd_conv_b[i], e_dt_bias[i],
                            e_a_log[i], e_d_skip[i], e_ssd_norm[i], e_conf_conv_w[i], e_conf_conv_b[i],
                            e_conf_ln_g[i], e_conf_ln_b[i], e_w_out[i])
        else:
            h = _odd_layer(h, bsz, s, o_norm[i], o_w_in[i], o_gate_bias[i], o_cmp_pe_k[i], o_cmp_w1_k[i],
                           o_cmp_w2_k[i], o_cmp_pe_v[i], o_cmp_w1_v[i], o_cmp_w2_v[i], o_w_out[i],
                           final_norm, layer == depth - 1)
    return h.reshape(bsz, s, d)
```

```python
import functools
import math

import numpy as np
import jax
import jax.numpy as jnp
from jax import lax
from jax.experimental import pallas as pl
from jax.experimental.pallas import tpu as pltpu

F32 = jnp.float32
BF16 = jnp.bfloat16

D_MODEL = 1024
SSD_HEADS = 16
SSD_HEAD_DIM = 64
SSD_INNER = 1024
SSD_GROUPS = 2
SSD_STATE = 128
SSD_CONV = 4
SSD_CHUNK = 128
SSD_XBC = SSD_INNER + 2 * SSD_GROUPS * SSD_STATE
CONF_WIDTH = 1024
CONF_CONV_WIDTH = 31
NSA_HEADS = 16
NSA_KV_GROUPS = 4
NSA_HPG = 4
NSA_HEAD_DIM = 64
NSA_WIDTH = 1024
KV_WIDTH = 256
CMP_BLOCK = 32
CMP_STRIDE = 16
CMP_HIDDEN = 256
SEL_BLOCK = 64
SEL_TOPK = 16
WINDOW = 512
FORCE_BONUS = 1e6
NORM_EPS = 1e-6

LANES = 128
SUBLANES = 8
VMEM_LIMIT_BYTES = 56 * 1024 * 1024

PROJ_ROWS = 256
CONF_ROWS = 256
CONF_ROW_CHUNK = 32
CONF_LANE_CHUNK = 512
ATT_TQ = 256
SEL_TK = 512
NEG_BIG = -1e30
V_AUG_ROWS = 80


def _dot(a, b, precision=None):
    return jnp.dot(a, b, preferred_element_type=F32, precision=precision)


def _dot_nt(a, b):
    return lax.dot_general(a, b, (((1,), (1,)), ((), ())), preferred_element_type=F32)


def _dot_tn(a, b):
    return lax.dot_general(a, b, (((0,), (0,)), ((), ())), preferred_element_type=F32)


def _silu(x):
    return x * jax.nn.sigmoid(x)


def _softplus(x):
    return jnp.maximum(x, 0.0) + jnp.log1p(jnp.exp(-jnp.abs(x)))


def _params(semantics):
    return pltpu.CompilerParams(dimension_semantics=semantics,
                                vmem_limit_bytes=VMEM_LIMIT_BYTES)


def _proj_kernel(*refs, nat_cols, t_rows, onehot_out, regroup_outs, seq_len):
    x_ref, g_ref = refs[0], refs[1]
    pos = 2
    w_ref = wt_ref = tb_ref = None
    if nat_cols:
        w_ref = refs[pos]
        pos += 1
    if t_rows:
        wt_ref, tb_ref = refs[pos], refs[pos + 1]
        pos += 2
    n_out = len(nat_cols) + len(t_rows)
    out_refs = refs[pos:pos + n_out]
    scratch = refs[pos + n_out:]
    x = x_ref[...]
    tm = x.shape[0]
    ms = jnp.mean(x * x, axis=-1, keepdims=True)
    xn = ((x * lax.rsqrt(ms + NORM_EPS)) * g_ref[...]).astype(BF16)
    k = 0
    for (a, b) in nat_cols:
        res = _dot(xn, w_ref[:, a:b])
        if k in regroup_outs:
            d = NSA_HEAD_DIM
            res_sc = scratch[0]
            per = LANES // d
            for c in range(res_sc.shape[0]):
                res_sc[c] = res[:, c * LANES:(c + 1) * LANES]
            for l in range(CMP_STRIDE):
                for c in range(res_sc.shape[0]):
                    rows = res_sc[c, pl.ds(l, tm // CMP_STRIDE, stride=CMP_STRIDE), :]
                    for j in range(per):
                        out_refs[k][c * per + j, :, l * d:(l + 1) * d] = rows[:, j * d:(j + 1) * d]
            k += 1
            continue
        if k == onehot_out:
            row = pl.program_id(0) * tm + lax.broadcasted_iota(jnp.int32, (tm, b - a), 0)
            blk = lax.shift_right_logical(lax.rem(row, seq_len), int(math.log2(SEL_BLOCK)))
            slot = lax.broadcasted_iota(jnp.int32, (tm, b - a), 1) & (LANES - 1)
            res = jnp.where(slot - NSA_HEAD_DIM == blk, 1.0, res)
        out_refs[k][...] = res.astype(out_refs[k].dtype)
        k += 1
    for (a, b) in t_rows:
        out_refs[k][...] = (_dot_nt(wt_ref[a:b, :], xn) + tb_ref[a:b, :]).astype(out_refs[k].dtype)
        k += 1


def _norm_proj(h2d, gain, w_nat, nat_spec, w_t, t_bias, t_spec, tm, onehot_out=-1, regroup_outs=(),
               seq_len=1):
    t_tokens, d = h2d.shape
    nat_cols, off = [], 0
    for width, _ in nat_spec:
        nat_cols.append((off, off + width))
        off += width
    t_rows, off = [], 0
    for rows, _ in t_spec:
        t_rows.append((off, off + rows))
        off += rows
    in_specs = [pl.BlockSpec((tm, d), lambda i: (i, 0)),
                pl.BlockSpec((1, d), lambda i: (0, 0))]
    args = [h2d, gain.reshape(1, d)]
    if nat_spec:
        in_specs.append(pl.BlockSpec(w_nat.shape, lambda i: (0, 0)))
        args.append(w_nat)
    if t_spec:
        in_specs.append(pl.BlockSpec(w_t.shape, lambda i: (0, 0)))
        in_specs.append(pl.BlockSpec((w_t.shape[0], 1), lambda i: (0, 0)))
        args += [w_t, t_bias.reshape(-1, 1).astype(F32)]
    out_shape, out_specs, scratch = [], [], []
    for k, (width, dt) in enumerate(nat_spec):
        if k in regroup_outs:
            slab = (NSA_KV_GROUPS, t_tokens // CMP_STRIDE, CMP_STRIDE * NSA_HEAD_DIM)
            out_shape.append(jax.ShapeDtypeStruct(slab, dt))
            out_specs.append(pl.BlockSpec((slab[0], tm // CMP_STRIDE, slab[2]), lambda i: (0, i, 0)))
            scratch = [pltpu.VMEM((width // LANES, tm, LANES), F32)]
        else:
            out_shape.append(jax.ShapeDtypeStruct((t_tokens, width), dt))
            out_specs.append(pl.BlockSpec((tm, width), lambda i: (i, 0)))
    for rows, dt in t_spec:
        out_shape.append(jax.ShapeDtypeStruct((rows, t_tokens), dt))
        out_specs.append(pl.BlockSpec((rows, tm), lambda i: (0, i)))
    return pl.pallas_call(
        functools.partial(_proj_kernel, nat_cols=tuple(nat_cols), t_rows=tuple(t_rows),
                          onehot_out=onehot_out, regroup_outs=tuple(regroup_outs), seq_len=seq_len),
        grid=(t_tokens // tm,),
        in_specs=in_specs,
        out_specs=out_specs,
        out_shape=out_shape,
        scratch_shapes=scratch,
        compiler_params=_params(("parallel",)),
        name="norm_proj",
    )(*args)


def _ssd_kernel(xbc_ref, z_ref, dt_ref, dtT_ref, cw_ref, cb_ref, dtb_ref, dtbT_ref,
                alog_ref, alogT_ref, dskip_ref, norm_ref, y_ref,
                state_sc, carry_sc, win_sc):
    L = SSD_CHUNK
    hp = LANES // SSD_HEAD_DIM
    gw = SSD_INNER // SSD_GROUPS
    heads_per_group = SSD_HEADS // SSD_GROUPS

    @pl.when(pl.program_id(1) == 0)
    def _():
        state_sc[...] = jnp.zeros_like(state_sc)
        carry_sc[...] = jnp.zeros_like(carry_sc)

    xraw = xbc_ref[0].astype(F32)
    win_sc[0:SUBLANES, :] = carry_sc[...]
    win_sc[SUBLANES:SUBLANES + L, :] = xraw
    carry_sc[...] = xraw[L - SUBLANES:L, :]
    conv = cw_ref[0:1, :] * win_sc[SUBLANES - 3:SUBLANES - 3 + L, :]
    for k in range(1, SSD_CONV):
        o = SUBLANES - (SSD_CONV - 1) + k
        conv = conv + cw_ref[k:k + 1, :] * win_sc[o:o + L, :]
    xact = _silu(conv + cb_ref[...])
    xs = xact[:, :SSD_INNER]
    bm = xact[:, SSD_INNER:SSD_INNER + SSD_GROUPS * SSD_STATE].astype(BF16)
    cm = xact[:, SSD_INNER + SSD_GROUPS * SSD_STATE:].astype(BF16)

    dt = _softplus(dt_ref[0] + dtb_ref[...])
    d_a = dt * (-jnp.exp(alog_ref[...]))
    rows = lax.broadcasted_iota(jnp.int32, (L, L), 0)
    cols = lax.broadcasted_iota(jnp.int32, (L, L), 1)
    causal = rows >= cols
    cum = _dot(causal.astype(F32), d_a, precision=lax.Precision.HIGHEST)
    dt_t = _softplus(dtT_ref[...] + dtbT_ref[...])
    d_a_t = dt_t * (-jnp.exp(alogT_ref[...]))
    cum_t = _dot(d_a_t, (rows <= cols).astype(F32), precision=lax.Precision.HIGHEST)
    ecum = jnp.exp(cum)
    cum_last = cum[L - 1:L, :]
    to_end = jnp.exp(cum_last - cum) * dt
    e_last = jnp.exp(cum_last)

    lane = lax.broadcasted_iota(jnp.int32, (L, LANES), 1)
    first_head = lane < SSD_HEAD_DIM
    lane1 = lax.broadcasted_iota(jnp.int32, (1, LANES), 1)
    first_head1 = lane1 < SSD_HEAD_DIM

    def per_head(arr, h0, mask):
        return jnp.where(mask, arr[:, h0:h0 + 1], arr[:, h0 + 1:h0 + 2])

    y_blocks, ecum_blocks, toend_blocks, elast_blocks = [], [], [], []
    for g in range(SSD_GROUPS):
        bm_g = bm[:, g * SSD_STATE:(g + 1) * SSD_STATE]
        cm_g = cm[:, g * SSD_STATE:(g + 1) * SSD_STATE]
        cb = _dot_nt(cm_g, bm_g)
        for j in range(heads_per_group // hp):
            h0 = g * heads_per_group + j * hp
            c0 = h0 * SSD_HEAD_DIM
            x_pair = xs[:, c0:c0 + LANES].astype(BF16)
            res = []
            for hh in range(hp):
                h = h0 + hh
                seg = cum[:, h:h + 1] - cum_t[h:h + 1, :]
                decay = jnp.exp(jnp.where(causal, seg, -jnp.inf))
                wts = (cb * decay) * dt_t[h:h + 1, :]
                res.append(_dot(wts.astype(BF16), x_pair))
            y_blocks.append(jnp.where(first_head, res[0], res[1]))
            ecum_blocks.append(per_head(ecum, h0, first_head))
            toend_blocks.append(per_head(to_end, h0, first_head))
            elast_blocks.append(per_head(e_last, h0, first_head1))
    y_diag = jnp.concatenate(y_blocks, axis=1)
    ecum_x = jnp.concatenate(ecum_blocks, axis=1)
    toend_x = jnp.concatenate(toend_blocks, axis=1)
    elast_x = jnp.concatenate(elast_blocks, axis=1)

    xw = (xs * toend_x).astype(BF16)
    y_off_blocks = []
    for g in range(SSD_GROUPS):
        bm_g = bm[:, g * SSD_STATE:(g + 1) * SSD_STATE]
        cm_g = cm[:, g * SSD_STATE:(g + 1) * SSD_STATE]
        st = state_sc[g]
        y_off_blocks.append(_dot(cm_g, st.astype(BF16)))
        state_sc[g] = st * elast_x[:, g * gw:(g + 1) * gw] + \
            _dot_tn(bm_g, xw[:, g * gw:(g + 1) * gw])
    y_off = jnp.concatenate(y_off_blocks, axis=1) * ecum_x

    y = y_diag + y_off + dskip_ref[...] * xs
    yg = y * _silu(z_ref[0].astype(F32))
    outs = []
    for g in range(SSD_GROUPS):
        yg_g = yg[:, g * gw:(g + 1) * gw]
        ms = jnp.mean(yg_g * yg_g, axis=-1, keepdims=True)
        outs.append(yg_g * lax.rsqrt(ms + NORM_EPS))
    y_ref[0] = (jnp.concatenate(outs, axis=1) * norm_ref[...]).astype(y_ref.dtype)


def _ssd(xbc, z, dt_nat, dt_t, conv_w, conv_b, dt_bias, a_log, d_skip, ssd_norm, bsz, s):
    nch = s // SSD_CHUNK
    L = SSD_CHUNK
    pad = LANES - SSD_HEADS
    dtb = jnp.pad(dt_bias, (0, pad)).reshape(1, LANES)
    alog = jnp.pad(a_log, (0, pad)).reshape(1, LANES)
    dskip_x = jnp.repeat(d_skip, SSD_HEAD_DIM).reshape(1, SSD_INNER)
    const = lambda shape: pl.BlockSpec(shape, lambda b, c: tuple(0 for _ in shape))
    return pl.pallas_call(
        _ssd_kernel,
        grid=(bsz, nch),
        in_specs=[
            pl.BlockSpec((1, L, SSD_XBC), lambda b, c: (b, c, 0)),
            pl.BlockSpec((1, L, SSD_INNER), lambda b, c: (b, c, 0)),
            pl.BlockSpec((1, L, LANES), lambda b, c: (b, c, 0)),
            pl.BlockSpec((SSD_HEADS, L), lambda b, c: (0, b * nch + c)),
            const((SSD_CONV, SSD_XBC)), const((1, SSD_XBC)),
            const((1, LANES)), const((SSD_HEADS, 1)),
            const((1, LANES)), const((SSD_HEADS, 1)),
            const((1, SSD_INNER)), const((1, SSD_INNER)),
        ],
        out_specs=pl.BlockSpec((1, L, SSD_INNER), lambda b, c: (b, c, 0)),
        out_shape=jax.ShapeDtypeStruct((bsz, s, SSD_INNER), BF16),
        scratch_shapes=[
            pltpu.VMEM((SSD_GROUPS, SSD_STATE, SSD_INNER // SSD_GROUPS), F32),
            pltpu.VMEM((SUBLANES, SSD_XBC), F32),
            pltpu.VMEM((SUBLANES + L, SSD_XBC), F32),
        ],
        compiler_params=_params(("parallel", "arbitrary")),
        name="ssd_scan",
    )(xbc.reshape(bsz, s, SSD_XBC), z.reshape(bsz, s, SSD_INNER),
      dt_nat.reshape(bsz, s, LANES), dt_t,
      conv_w, conv_b.reshape(1, SSD_XBC), dtb, dt_bias.reshape(SSD_HEADS, 1),
      alog, a_log.reshape(SSD_HEADS, 1), dskip_x, ssd_norm.reshape(1, SSD_INNER))


def _conf_kernel(glu_ref, zc_ref, w_ref, b_ref, lng_ref, lnb_ref, y_ref, sh_sc, u_sc):
    ts = glu_ref.shape[1]
    n = sh_sc.shape[1]
    halo = n - ts
    first = halo - (CONF_CONV_WIDTH - 1)

    @pl.when(pl.program_id(1) == 0)
    def _():
        sh_sc[0, 0:halo, :] = jnp.zeros((halo, CONF_WIDTH), F32)

    sh_sc[0, halo:n, :] = glu_ref[0, :, :CONF_WIDTH].astype(F32) * \
        jax.nn.sigmoid(glu_ref[0, :, CONF_WIDTH:].astype(F32))
    tiles = sh_sc[0].reshape(n // SUBLANES, SUBLANES, CONF_WIDTH)
    sub = lax.broadcasted_iota(jnp.int32, (n // SUBLANES - 1, SUBLANES, CONF_WIDTH), 1)
    for s in range(1, SUBLANES):
        rolled = pltpu.roll(tiles, SUBLANES - s, 1)
        shifted = jnp.where(sub < SUBLANES - s, rolled[:-1], rolled[1:])
        sh_sc[s, 0:n - SUBLANES, :] = shifted.reshape(n - SUBLANES, CONF_WIDTH)
    for r0 in range(0, ts, CONF_ROW_CHUNK):
        for c0 in range(0, CONF_WIDTH, CONF_LANE_CHUNK):
            cs = slice(c0, c0 + CONF_LANE_CHUNK)
            acc = None
            for k in range(CONF_CONV_WIDTH):
                a, s = divmod(first + k, SUBLANES)
                o = r0 + a * SUBLANES
                window = sh_sc[s, o:o + CONF_ROW_CHUNK, cs]
                term = w_ref[k][None, :, cs] * window.reshape(CONF_ROW_CHUNK // SUBLANES, SUBLANES, -1)
                acc = term if acc is None else acc + term
            u_sc[r0:r0 + CONF_ROW_CHUNK, cs] = acc.reshape(CONF_ROW_CHUNK, -1)
    sh_sc[0, 0:halo, :] = sh_sc[0, ts:n, :]
    u = u_sc[...] + b_ref[...]
    mu = jnp.mean(u, axis=-1, keepdims=True)
    var = jnp.mean(jnp.square(u - mu), axis=-1, keepdims=True)
    un = (u - mu) * lax.rsqrt(var + NORM_EPS) * lng_ref[...] + lnb_ref[...]
    y_ref[0] = (_silu(un) * _silu(zc_ref[0].astype(F32))).astype(y_ref.dtype)


def _conformer(glu, zc, conv_w, conv_b, ln_g, ln_b, bsz, s):
    ts = min(CONF_ROWS, s)
    halo = 32
    w_tiles = jnp.broadcast_to(conv_w[:, None, :], (CONF_CONV_WIDTH, SUBLANES, CONF_WIDTH))
    const = lambda shape: pl.BlockSpec(shape, lambda b, i: tuple(0 for _ in shape))
    return pl.pallas_call(
        _conf_kernel,
        grid=(bsz, s // ts),
        in_specs=[
            pl.BlockSpec((1, ts, 2 * CONF_WIDTH), lambda b, i: (b, i, 0)),
            pl.BlockSpec((1, ts, CONF_WIDTH), lambda b, i: (b, i, 0)),
            const(w_tiles.shape), const((1, CONF_WIDTH)),
            const((1, CONF_WIDTH)), const((1, CONF_WIDTH)),
        ],
        out_specs=pl.BlockSpec((1, ts, CONF_WIDTH), lambda b, i: (b, i, 0)),
        out_shape=jax.ShapeDtypeStruct((bsz, s, CONF_WIDTH), BF16),
        scratch_shapes=[pltpu.VMEM((SUBLANES, halo + ts, CONF_WIDTH), F32),
                        pltpu.VMEM((ts, CONF_WIDTH), F32)],
        compiler_params=_params(("parallel", "arbitrary")),
        name="conformer",
    )(glu.reshape(bsz, s, 2 * CONF_WIDTH), zc.reshape(bsz, s, CONF_WIDTH), w_tiles,
      conv_b.reshape(1, -1), ln_g.reshape(1, -1), ln_b.reshape(1, -1))


def _outproj_even_kernel(h_ref, a_ref, b_ref, wa_ref, wb_ref, o_ref):
    o_ref[...] = h_ref[...] + (_dot(a_ref[...], wa_ref[...]) + _dot(b_ref[...], wb_ref[...]))


def _outproj_even(h2d, ya, yb, w_out, tm):
    t_tokens, d = h2d.shape
    wa = w_out[:SSD_INNER].astype(BF16)
    wb = w_out[SSD_INNER:].astype(BF16)
    row = lambda w: pl.BlockSpec((tm, w), lambda i: (i, 0))
    return pl.pallas_call(
        _outproj_even_kernel,
        grid=(t_tokens // tm,),
        in_specs=[row(d), row(SSD_INNER), row(CONF_WIDTH),
                  pl.BlockSpec(wa.shape, lambda i: (0, 0)),
                  pl.BlockSpec(wb.shape, lambda i: (0, 0))],
        out_specs=row(d),
        out_shape=jax.ShapeDtypeStruct((t_tokens, d), F32),
        compiler_params=_params(("parallel",)),
        name="outproj_even",
    )(h2d, ya, yb, wa, wb)


def _outproj_odd_kernel(h_ref, oc_ref, os_ref, ow_ref, zt_ref, w_ref, g_ref, o_ref, *, final):
    o = (oc_ref[...].astype(F32) + os_ref[...].astype(F32)) + ow_ref[...].astype(F32)
    y = (o * _silu(zt_ref[...])).astype(BF16)
    out = h_ref[...] + _dot_tn(y, w_ref[...])
    if final:
        ms = jnp.mean(out * out, axis=-1, keepdims=True)
        out = (out * lax.rsqrt(ms + NORM_EPS)) * g_ref[...]
    o_ref[...] = out


def _outproj_odd(h2d, oc, osel, ow, z_t, w_out, final_gain, final, tm):
    t_tokens, d = h2d.shape
    w = w_out.astype(BF16)
    col = pl.BlockSpec((NSA_WIDTH, tm), lambda i: (0, i))
    return pl.pallas_call(
        functools.partial(_outproj_odd_kernel, final=final),
        grid=(t_tokens // tm,),
        in_specs=[pl.BlockSpec((tm, d), lambda i: (i, 0)), col, col, col, col,
                  pl.BlockSpec(w.shape, lambda i: (0, 0)),
                  pl.BlockSpec((1, d), lambda i: (0, 0))],
        out_specs=pl.BlockSpec((tm, d), lambda i: (i, 0)),
        out_shape=jax.ShapeDtypeStruct((t_tokens, d), F32),
        compiler_params=_params(("parallel",)),
        name="outproj_odd",
    )(h2d, oc, osel, ow, z_t, w, final_gain.reshape(1, d))


def _compress_kernel(xk_ref, xv_ref, pek_ref, pev_ref, w1k_ref, w1v_ref, w2k_ref, w2vt_ref,
                     kc_ref, vct_ref):
    half = CMP_STRIDE * NSA_HEAD_DIM

    def hidden(x_ref, pe_ref, w1_ref):
        x = x_ref[0, 0]
        n = x.shape[0]
        lo = (x + pe_ref[0:1, :]).astype(BF16)
        hi = (x + pe_ref[1:2, :]).astype(BF16)
        h = _dot(lo, w1_ref[0:half, :]) + pltpu.roll(_dot(hi, w1_ref[half:2 * half, :]), n - 1, 0)
        return _silu(h).astype(BF16)

    kc_ref[0, 0] = _dot(hidden(xk_ref, pek_ref, w1k_ref), w2k_ref[...]).astype(kc_ref.dtype)
    vct = _dot_nt(w2vt_ref[...], hidden(xv_ref, pev_ref, w1v_ref))
    row = lax.broadcasted_iota(jnp.int32, vct.shape, 0)
    vct_ref[0, 0] = jnp.where(row == NSA_HEAD_DIM, 1.0, vct).astype(vct_ref.dtype)


def _compress(kc, vc, pe_k, w1_k, w2_k, pe_v, w1_v, w2_v, bsz, s):
    g, d = NSA_KV_GROUPS, NSA_HEAD_DIM
    nh = s // CMP_STRIDE
    half = CMP_STRIDE * d

    def regroup(t):
        return t.reshape(g, bsz, nh, half)

    const = lambda shape: pl.BlockSpec(shape, lambda b, gi: tuple(0 for _ in shape))
    blk = pl.BlockSpec((1, 1, nh, half), lambda b, gi: (gi, b, 0, 0))
    return pl.pallas_call(
        _compress_kernel,
        grid=(bsz, g),
        in_specs=[blk, blk, const((2, half)), const((2, half)),
                  const((2 * half, CMP_HIDDEN)), const((2 * half, CMP_HIDDEN)),
                  const((CMP_HIDDEN, d)), const((V_AUG_ROWS, CMP_HIDDEN))],
        out_specs=[pl.BlockSpec((1, 1, nh, d), lambda b, gi: (b, gi, 0, 0)),
                   pl.BlockSpec((1, 1, V_AUG_ROWS, nh), lambda b, gi: (b, gi, 0, 0))],
        out_shape=[jax.ShapeDtypeStruct((bsz, g, nh, d), BF16),
                   jax.ShapeDtypeStruct((bsz, g, V_AUG_ROWS, nh), BF16)],
        compiler_params=_params(("parallel", "parallel")),
        name="compress_kv",
    )(regroup(kc), regroup(vc), pe_k.reshape(2, half), pe_v.reshape(2, half),
      w1_k.astype(BF16), w1_v.astype(BF16), w2_k.astype(BF16),
      jnp.pad(w2_v.T, ((0, V_AUG_ROWS - d), (0, 0))).astype(BF16))


def _selection_bias(imp, t0, topk, v_sc, cnt_sc):
    n_sel, tq = imp.shape
    shift = int(math.log2(SEL_BLOCK))
    ngroups = n_sel // SUBLANES
    jidx = lax.broadcasted_iota(jnp.int32, (n_sel, tq), 0)
    t = t0 + lax.broadcasted_iota(jnp.int32, (n_sel, tq), 1)
    cur = lax.shift_right_logical(t, shift)
    valid = jidx * SEL_BLOCK <= t
    forced = (jidx == 0) | (jidx == cur) | (jidx == cur - 1)
    v_sc[...] = jnp.where(valid, imp + jnp.where(forced, FORCE_BONUS, 0.0), -jnp.inf)
    cnt_sc[...] = jnp.zeros((n_sel, tq), jnp.int32)
    last_block = lax.shift_right_logical(t0 + tq - 1, shift)
    last_group = lax.shift_right_logical(last_block, int(math.log2(SUBLANES)))
    sub = lax.broadcasted_iota(jnp.int32, (SUBLANES, tq), 0)

    def group(ref, gi):
        return ref[gi * SUBLANES:(gi + 1) * SUBLANES, :]

    def rows_of(vals):
        return [jnp.broadcast_to(vals[ii:ii + 1, :], (SUBLANES, tq)) for ii in range(SUBLANES)]

    for hi in range(ngroups):
        @pl.when((hi <= last_group) & (last_block >= topk))
        def _():
            v_hi = group(v_sc, hi)
            rows_hi = rows_of(v_hi)
            own = jnp.zeros((SUBLANES, tq), jnp.int32)
            for ii in range(SUBLANES):
                own = own + jnp.where(sub > ii, jnp.where(rows_hi[ii] >= v_hi, 1, 0),
                                      jnp.where(rows_hi[ii] > v_hi, 1, 0))
            for lo in range(hi):
                v_lo = group(v_sc, lo)
                add_lo = jnp.zeros((SUBLANES, tq), jnp.int32)
                for ii in range(SUBLANES):
                    add_lo = add_lo + jnp.where(rows_hi[ii] > v_lo, 1, 0)
                cnt_sc[lo * SUBLANES:(lo + 1) * SUBLANES, :] += add_lo
                for row in rows_of(v_lo):
                    own = own + jnp.where(row >= v_hi, 1, 0)
            cnt_sc[hi * SUBLANES:(hi + 1) * SUBLANES, :] += own

    return jnp.where(valid & (cnt_sc[...] < topk), 0.0, NEG_BIG)


def _lane_tile(x, reps):
    return jnp.concatenate([x] * reps, axis=1)


def _window_branch(qa, k_refs, v_refs, i, tq):
    nblk = len(k_refs)
    krow = lax.broadcasted_iota(jnp.int32, (tq, tq), 0)
    qcol = lax.broadcasted_iota(jnp.int32, (tq, tq), 1)
    rel0 = qcol - krow
    s_blocks = []
    for jj in range(nblk):
        back = nblk - 1 - jj
        if back == 0:
            bias = jnp.where(rel0 >= 0, 0.0, NEG_BIG)
        else:
            limit = jnp.where(i >= back, WINDOW, -1)
            bias = jnp.where(rel0 + back * tq < limit, 0.0, NEG_BIG)
        s_blocks.append(_dot(k_refs[jj][...], qa) + _lane_tile(bias, NSA_HPG))
    m = jnp.max(s_blocks[0], axis=0, keepdims=True)
    for s in s_blocks[1:]:
        m = jnp.maximum(m, jnp.max(s, axis=0, keepdims=True))
    acc = _dot(v_refs[0][0], jnp.exp(s_blocks[0] - m).astype(BF16))
    for jj in range(1, nblk):
        acc = acc + _dot(v_refs[jj][0], jnp.exp(s_blocks[jj] - m).astype(BF16))
    return acc


def _cmp_win_kernel(*refs, nblk, n_cmp, topk):
    qt_ref, kc_ref, vct_ref, ovt_ref = refs[0:4]
    kw_refs = refs[4:4 + nblk]
    vw_refs = refs[4 + nblk:4 + 2 * nblk]
    glc_ref, gbc_ref, glw_ref, gbw_ref, oc_ref, ow_ref, selb_ref, qa_sc, v_sc, cnt_sc = refs[4 + 2 * nblk:]
    tq = qt_ref.shape[2]
    d = NSA_HEAD_DIM
    w = NSA_HPG * tq
    nc = kc_ref.shape[2]
    n_sel = ovt_ref.shape[0]
    i = pl.program_id(2)
    t0 = i * tq
    qa_sc[d:, :] = jnp.zeros((qa_sc.shape[0] - d, w), qa_sc.dtype)
    for r in range(NSA_HPG):
        qa_sc[0:d, r * tq:(r + 1) * tq] = qt_ref[r]

    acc_w = _window_branch(qa_sc[...], kw_refs, vw_refs, i, tq)
    gates_w = jax.nn.sigmoid(glw_ref[0] + gbw_ref[0])
    for r in range(NSA_HPG):
        a = acc_w[:, r * tq:(r + 1) * tq]
        ow_ref[r] = (a[0:d] * (1.0 / a[d:d + 1]) * gates_w[r:r + 1, :]).astype(ow_ref.dtype)

    cidx = lax.broadcasted_iota(jnp.int32, (nc, tq), 0)
    t = t0 + lax.broadcasted_iota(jnp.int32, (nc, tq), 1)
    cbias = jnp.where((cidx * CMP_STRIDE + (CMP_BLOCK - 1) <= t) & (cidx < n_cmp), 0.0, -jnp.inf)
    s = _dot(kc_ref[0, 0], qa_sc[0:d, :]) + _lane_tile(cbias, NSA_HPG)
    m = jnp.max(s, axis=0, keepdims=True)
    m = jnp.where(m > -jnp.inf, m, 0.0)
    e = jnp.exp(s - m).astype(BF16)
    both = _dot(jnp.concatenate([vct_ref[0, 0], ovt_ref[...]], axis=0), e)
    rinv = 1.0 / jnp.maximum(both[d:d + 1, :], 1e-30)
    gates_c = jax.nn.sigmoid(glc_ref[0] + gbc_ref[0])
    imp = jnp.zeros((n_sel, tq), F32)
    for r in range(NSA_HPG):
        cols = slice(r * tq, (r + 1) * tq)
        oc_ref[r] = (both[0:d, cols] * rinv[:, cols] * gates_c[r:r + 1, :]).astype(oc_ref.dtype)
        imp = imp + both[V_AUG_ROWS:, cols] * rinv[:, cols]
    bias = _selection_bias(imp, t0, topk, v_sc, cnt_sc)
    n_pad = selb_ref.shape[2] - bias.shape[0]
    if n_pad:
        bias = jnp.concatenate([bias, jnp.zeros((n_pad, tq), F32)], axis=0)
    selb_ref[0, 0] = bias.astype(selb_ref.dtype)


def _gate_specs(nq):
    def spec(branch):
        return (pl.BlockSpec((1, SUBLANES, ATT_TQ), lambda b, g, i: (g * 3 + branch, 0, b * nq + i)),
                pl.BlockSpec((1, SUBLANES, 1), lambda b, g, i: (g * 3 + branch, 0, 0)))
    return spec


def _cmp_win_attn(q_t, kcmp, vcmp_t, kw_a, vw_a, gl_t, gbias, bsz, s):
    g, d = NSA_KV_GROUPS, NSA_HEAD_DIM
    tq = ATT_TQ
    nq = s // tq
    nblk = WINDOW // tq + 1
    nc = s // CMP_STRIDE
    n_cmp = (s - CMP_BLOCK) // CMP_STRIDE + 1
    n_sel = s // SEL_BLOCK
    n_selp = _sel_pad(n_sel)
    topk = min(SEL_TOPK, n_sel)
    cmp_start = np.arange(nc) * CMP_STRIDE
    sel_start = np.arange(n_sel) * SEL_BLOCK
    ov = ((cmp_start[:, None] < sel_start[None, :] + SEL_BLOCK) &
          (cmp_start[:, None] + CMP_BLOCK > sel_start[None, :]) &
          (np.arange(nc)[:, None] < n_cmp))
    ovt = jnp.asarray(ov.T.astype(np.float32), dtype=BF16)
    glc_spec, gbc_spec = _gate_specs(nq)(0)
    glw_spec, gbw_spec = _gate_specs(nq)(2)
    t_tokens = bsz * s

    def kw_spec(jj):
        return pl.BlockSpec((tq, LANES), lambda b, gi, i: (b * nq + jnp.maximum(i - (nblk - 1) + jj, 0), gi))

    def vw_spec(jj):
        return pl.BlockSpec((1, V_AUG_ROWS, tq),
                            lambda b, gi, i: (gi, 0, b * nq + jnp.maximum(i - (nblk - 1) + jj, 0)))

    head_tile = pl.BlockSpec((NSA_HPG, d, tq), lambda b, gi, i: (gi, 0, b * nq + i))
    return pl.pallas_call(
        functools.partial(_cmp_win_kernel, nblk=nblk, n_cmp=n_cmp, topk=topk),
        grid=(bsz, g, nq),
        in_specs=[
            head_tile,
            pl.BlockSpec((1, 1, nc, d), lambda b, gi, i: (b, gi, 0, 0)),
            pl.BlockSpec((1, 1, V_AUG_ROWS, nc), lambda b, gi, i: (b, gi, 0, 0)),
            pl.BlockSpec((n_sel, nc), lambda b, gi, i: (0, 0)),
        ] + [kw_spec(jj) for jj in range(nblk)] + [vw_spec(jj) for jj in range(nblk)]
        + [glc_spec, gbc_spec, glw_spec, gbw_spec],
        out_specs=[head_tile, head_tile,
                   pl.BlockSpec((1, 1, n_selp, tq), lambda b, gi, i: (b, gi, 0, i))],
        out_shape=[jax.ShapeDtypeStruct((NSA_HEADS, d, t_tokens), BF16),
                   jax.ShapeDtypeStruct((NSA_HEADS, d, t_tokens), BF16),
                   jax.ShapeDtypeStruct((bsz, g, n_selp, s), BF16)],
        scratch_shapes=[pltpu.VMEM((LANES, NSA_HPG * tq), BF16),
                        pltpu.VMEM((n_sel, tq), F32), pltpu.VMEM((n_sel, tq), jnp.int32)],
        compiler_params=_params(("parallel", "parallel", "parallel")),
        name="cmp_win_attn",
    )(q_t, kcmp, vcmp_t, ovt, *([kw_a] * nblk), *([vw_a] * nblk), gl_t, gbias, gl_t, gbias)


def _sel_pad(n_sel):
    return -(-n_sel // NSA_HEAD_DIM) * NSA_HEAD_DIM


def _sel_attn_kernel(qt_ref, ka_ref, va_ref, selb_ref, gl_ref, gb_ref, o_ref,
                     qa_sc, sa_sc, sb_sc, ma_sc, mb_sc, m_sc, acc_sc, *, tk):
    tq = qt_ref.shape[2]
    d = NSA_HEAD_DIM
    w = NSA_HPG * tq
    i = pl.program_id(2)
    t0 = i * tq
    last = (i * tq) // tk
    for r in range(NSA_HPG):
        qa_sc[0:d, r * tq:(r + 1) * tq] = qt_ref[r]
        qa_sc[d:, r * tq:(r + 1) * tq] = selb_ref[0, 0]
    m_sc[...] = jnp.full(m_sc.shape, NEG_BIG, F32)
    acc_sc[...] = jnp.zeros(acc_sc.shape, F32)

    def scores(j, s_ref, mx_ref, diagonal):
        k0 = pl.multiple_of(j * tk, tk)
        s = _dot(ka_ref[pl.ds(k0, tk), :], qa_sc[...])
        if diagonal:
            kpos = k0 + lax.broadcasted_iota(jnp.int32, (tk, w), 0)
            t = t0 + (lax.broadcasted_iota(jnp.int32, (tk, w), 1) & (tq - 1))
            s = jnp.where(kpos <= t, s, NEG_BIG)
        s_ref[...] = s
        mx_ref[...] = jnp.max(s, axis=0, keepdims=True)

    def accumulate(j, s_ref, mx_ref):
        k0 = pl.multiple_of(j * tk, tk)
        m_old = m_sc[...]
        m_new = jnp.maximum(m_old, mx_ref[...])
        p = jnp.exp(s_ref[...] - m_new).astype(BF16)
        acc_sc[...] = jnp.exp(m_old - m_new) * acc_sc[...] + _dot(va_ref[0, :, pl.ds(k0, tk)], p)
        m_sc[...] = m_new

    scores(last, sa_sc, ma_sc, True)

    def pair(mi, carry):
        j = 2 * mi
        scores(j, sb_sc, mb_sc, False)
        accumulate(jnp.where(mi == 0, last, j - 1), sa_sc, ma_sc)
        scores(j + 1, sa_sc, ma_sc, False)
        accumulate(j, sb_sc, mb_sc)
        return carry

    lax.fori_loop(0, last // 2, pair, 0)

    @pl.when(last % 2 == 1)
    def _():
        scores(last - 1, sb_sc, mb_sc, False)
        accumulate(jnp.where(last == 1, last, last - 2), sa_sc, ma_sc)
        accumulate(last - 1, sb_sc, mb_sc)

    @pl.when(last % 2 == 0)
    def _():
        accumulate(jnp.where(last == 0, last, last - 1), sa_sc, ma_sc)

    gates = jax.nn.sigmoid(gl_ref[0] + gb_ref[0])
    for r in range(NSA_HPG):
        a = acc_sc[:, r * tq:(r + 1) * tq]
        o_ref[r] = (a[0:d] * (1.0 / a[d:d + 1]) * gates[r:r + 1, :]).astype(o_ref.dtype)


def _sel_attn(q_t, ka, va, selb, gl_t, gbias, bsz, s):
    g, d = NSA_KV_GROUPS, NSA_HEAD_DIM
    tq = ATT_TQ
    tk = min(SEL_TK, s)
    nq = s // tq
    w = NSA_HPG * tq
    n_selp = selb.shape[2]
    kd = d + n_selp
    gl_spec, gb_spec = _gate_specs(nq)(1)
    return pl.pallas_call(
        functools.partial(_sel_attn_kernel, tk=tk),
        grid=(bsz, g, nq),
        in_specs=[
            pl.BlockSpec((NSA_HPG, d, tq), lambda b, gi, i: (gi, 0, b * nq + i)),
            pl.BlockSpec((s, kd), lambda b, gi, i: (b, gi)),
            pl.BlockSpec((1, V_AUG_ROWS, s), lambda b, gi, i: (gi, 0, b)),
            pl.BlockSpec((1, 1, n_selp, tq), lambda b, gi, i: (b, gi, 0, i)),
            gl_spec, gb_spec,
        ],
        out_specs=pl.BlockSpec((NSA_HPG, d, tq), lambda b, gi, i: (gi, 0, b * nq + i)),
        out_shape=jax.ShapeDtypeStruct((NSA_HEADS, d, bsz * s), BF16),
        scratch_shapes=[pltpu.VMEM((kd, w), BF16),
                        pltpu.VMEM((tk, w), F32), pltpu.VMEM((tk, w), F32),
                        pltpu.VMEM((1, w), F32), pltpu.VMEM((1, w), F32),
                        pltpu.VMEM((1, w), F32), pltpu.VMEM((V_AUG_ROWS, w), F32)],
        compiler_params=_params(("parallel", "parallel", "arbitrary")),
        name="sel_attn",
    )(q_t, ka, va, selb, gl_t, gbias)


def _even_layer(h2d, bsz, s, gain, w_in, ssd_conv_w, ssd_conv_b, dt_bias, a_log, d_skip, ssd_norm,
                conf_conv_w, conf_conv_b, conf_ln_g, conf_ln_b, w_out):
    o1 = SSD_INNER
    o2 = o1 + SSD_XBC
    o3 = o2 + SSD_HEADS
    o4 = o3 + 2 * CONF_WIDTH
    w_dt = w_in[:, o2:o3]
    w_nat = jnp.concatenate(
        [w_in[:, :o2], w_in[:, o3:], jnp.pad(w_dt, ((0, 0), (0, LANES - SSD_HEADS)))], axis=1).astype(BF16)
    nat_spec = [(SSD_INNER, BF16), (SSD_XBC, BF16), (2 * CONF_WIDTH, BF16), (CONF_WIDTH, BF16), (LANES, F32)]
    tm = min(PROJ_ROWS, s)
    z, xbc, glu, zc, dt_nat, dt_t = _norm_proj(
        h2d, gain, w_nat, nat_spec, w_dt.T.astype(BF16), jnp.zeros((SSD_HEADS,), F32),
        [(SSD_HEADS, F32)], tm)
    y_a = _ssd(xbc, z, dt_nat, dt_t, ssd_conv_w, ssd_conv_b, dt_bias, a_log, d_skip, ssd_norm, bsz, s)
    y_b = _conformer(glu, zc, conf_conv_w, conf_conv_b, conf_ln_g, conf_ln_b, bsz, s)
    return _outproj_even(h2d, y_a.reshape(bsz * s, SSD_INNER), y_b.reshape(bsz * s, CONF_WIDTH), w_out, tm)


def _odd_layer(h2d, bsz, s, gain, w_in, gate_bias, pe_k, w1_k, w2_k, pe_v, w1_v, w2_v, w_out,
               final_gain, final):
    g, r, d = NSA_KV_GROUPS, NSA_HPG, NSA_HEAD_DIM
    t_tokens = bsz * s
    sizes = [NSA_WIDTH] + [KV_WIDTH] * 6 + [3 * NSA_HEADS, NSA_WIDTH]
    offs = np.cumsum([0] + sizes)
    col = lambda k: w_in[:, offs[k]:offs[k + 1]]
    w_q, w_kc, w_vc, w_ks, w_vs, w_kw, w_vw, w_gl, w_z = [col(k) for k in range(9)]
    perm = np.array([[[(gi * r + ri) * 3 + br for ri in range(r)] for br in range(3)] for gi in range(g)])
    w_gl_g = jnp.pad(w_gl[:, perm.reshape(-1)].reshape(D_MODEL, g * 3, r),
                     ((0, 0), (0, 0), (0, SUBLANES - r))).reshape(D_MODEL, g * 3 * SUBLANES)
    gbias = jnp.pad(gate_bias[perm.reshape(-1)].reshape(g * 3, r), ((0, 0), (0, SUBLANES - r)))
    gbias = gbias.reshape(g * 3, SUBLANES, 1)
    scale = NSA_HEAD_DIM ** -0.5
    assert d + _sel_pad(s // SEL_BLOCK) == LANES

    def key_slots(w):
        return jnp.pad(w.reshape(D_MODEL, g, d), ((0, 0), (0, 0), (0, LANES - d))).reshape(D_MODEL, g * LANES)

    def value_rows(w):
        return jnp.pad(w.reshape(D_MODEL, g, d), ((0, 0), (0, 0), (0, V_AUG_ROWS - d))).reshape(D_MODEL, g * V_AUG_ROWS)

    ones_row = jnp.zeros((g, V_AUG_ROWS), F32).at[:, d].set(1.0).reshape(-1)
    w_nat = jnp.concatenate([w_kc, w_vc, key_slots(w_ks), key_slots(w_kw)], axis=1).astype(BF16)
    w_t = jnp.concatenate([w_q * scale, value_rows(w_vs), value_rows(w_vw), w_gl_g, w_z], axis=1).T.astype(BF16)
    t_bias = jnp.concatenate([jnp.zeros((NSA_WIDTH,), F32), ones_row, ones_row,
                              jnp.zeros((g * 3 * SUBLANES + NSA_WIDTH,), F32)])
    nat_spec = [(KV_WIDTH, F32), (KV_WIDTH, F32), (g * LANES, BF16), (g * LANES, BF16)]
    t_spec = [(NSA_WIDTH, BF16), (g * V_AUG_ROWS, BF16), (g * V_AUG_ROWS, BF16),
              (g * 3 * SUBLANES, F32), (NSA_WIDTH, F32)]
    tm = min(PROJ_ROWS, s)
    kc, vc, ks_a, kw_a, q_t, vs_a, vw_a, gl_t, z_t = _norm_proj(
        h2d, gain, w_nat, nat_spec, w_t, t_bias, t_spec, tm, onehot_out=2, regroup_outs=(0, 1), seq_len=s)

    q_t = q_t.reshape(NSA_HEADS, d, t_tokens)
    gl_t = gl_t.reshape(g * 3, SUBLANES, t_tokens)
    vs_a = vs_a.reshape(g, V_AUG_ROWS, t_tokens)
    vw_a = vw_a.reshape(g, V_AUG_ROWS, t_tokens)

    kcmp, vcmp_t = _compress(kc, vc, pe_k, w1_k, w2_k, pe_v, w1_v, w2_v, bsz, s)
    o_cmp, o_win, selb = _cmp_win_attn(q_t, kcmp, vcmp_t, kw_a, vw_a, gl_t, gbias, bsz, s)
    o_sel = _sel_attn(q_t, ks_a, vs_a, selb, gl_t, gbias, bsz, s)
    flat = lambda o: o.reshape(NSA_WIDTH, t_tokens)
    return _outproj_odd(h2d, flat(o_cmp), flat(o_sel), flat(o_win), z_t, w_out, final_gain, final, tm)


def kernel(x, e_norm, e_w_in, e_ssd_conv_w, e_ssd_conv_b, e_dt_bias, e_a_log, e_d_skip, e_ssd_norm,
           e_conf_conv_w, e_conf_conv_b, e_conf_ln_g, e_conf_ln_b, e_w_out, o_norm, o_w_in, o_gate_bias,
           o_cmp_pe_k, o_cmp_w1_k, o_cmp_w2_k, o_cmp_pe_v, o_cmp_w1_v, o_cmp_w2_v, o_w_out, final_norm):
    bsz, s, d = x.shape
    depth = e_norm.shape[0] + o_norm.shape[0]
    h = x.reshape(bsz * s, d)
    for layer in range(depth):
        i = layer // 2
        if layer % 2 == 0:
            h = _even_layer(h, bsz, s, e_norm[i], e_w_in[i], e_ssd_conv_w[i], e_ssd_conv_b[i], e_dt_bias[i],
                            e_a_log[i], e_d_skip[i], e_ssd_norm[i], e_conf_conv_w[i], e_conf_conv_b[i],
                            e_conf_ln_g[i], e_conf_ln_b[i], e_w_out[i])
        else:
            h = _odd_layer(h, bsz, s, o_norm[i], o_w_in[i], o_gate_bias[i], o_cmp_pe_k[i], o_cmp_w1_k[i],
                           o_cmp_w2_k[i], o_cmp_pe_v[i], o_cmp_w1_v[i], o_cmp_w2_v[i], o_w_out[i],
                           final_norm, layer == depth - 1)
    return h.reshape(bsz, s, d)
```

```python
import functools
import math

import numpy as np
import jax
import jax.numpy as jnp
from jax import lax
from jax.experimental import pallas as pl
from jax.experimental.pallas import tpu as pltpu

F32 = jnp.float32
BF16 = jnp.bfloat16

D_MODEL = 1024
SSD_HEADS = 16
SSD_HEAD_DIM = 64
SSD_INNER = 1024
SSD_GROUPS = 2
SSD_STATE = 128
SSD_CONV = 4
SSD_CHUNK = 128
SSD_XBC = SSD_INNER + 2 * SSD_GROUPS * SSD_STATE
CONF_WIDTH = 1024
CONF_CONV_WIDTH = 31
NSA_HEADS = 16
NSA_KV_GROUPS = 4
NSA_HPG = 4
NSA_HEAD_DIM = 64
NSA_WIDTH = 1024
KV_WIDTH = 256
CMP_BLOCK = 32
CMP_STRIDE = 16
CMP_HIDDEN = 256
SEL_BLOCK = 64
SEL_TOPK = 16
WINDOW = 512
FORCE_BONUS = 1e6
NORM_EPS = 1e-6

LANES = 128
SUBLANES = 8
VMEM_LIMIT_BYTES = 56 * 1024 * 1024

PROJ_ROWS = 512
CONF_ROWS = 256
CONF_ROW_CHUNK = 32
CONF_LANE_CHUNK = 512
ATT_TQ = 256
SEL_TK = 512
NEG_BIG = -1e30
V_AUG_ROWS = 80


def _dot(a, b, precision=None):
    return jnp.dot(a, b, preferred_element_type=F32, precision=precision)


def _dot_nt(a, b):
    return lax.dot_general(a, b, (((1,), (1,)), ((), ())), preferred_element_type=F32)


def _dot_tn(a, b):
    return lax.dot_general(a, b, (((0,), (0,)), ((), ())), preferred_element_type=F32)


def _silu(x):
    return x * jax.nn.sigmoid(x)


def _softplus(x):
    return jnp.maximum(x, 0.0) + jnp.log1p(jnp.exp(-jnp.abs(x)))


def _params(semantics):
    return pltpu.CompilerParams(dimension_semantics=semantics,
                                vmem_limit_bytes=VMEM_LIMIT_BYTES)


def _proj_kernel(*refs, nat_cols, t_rows, onehot_out, regroup_outs, seq_len):
    x_ref, g_ref = refs[0], refs[1]
    pos = 2
    w_ref = wt_ref = tb_ref = None
    if nat_cols:
        w_ref = refs[pos]
        pos += 1
    if t_rows:
        wt_ref, tb_ref = refs[pos], refs[pos + 1]
        pos += 2
    n_out = len(nat_cols) + len(t_rows)
    out_refs = refs[pos:pos + n_out]
    scratch = refs[pos + n_out:]
    x = x_ref[...]
    tm = x.shape[0]
    ms = jnp.mean(x * x, axis=-1, keepdims=True)
    xn = ((x * lax.rsqrt(ms + NORM_EPS)) * g_ref[...]).astype(BF16)
    k = 0
    for (a, b) in nat_cols:
        res = _dot(xn, w_ref[:, a:b])
        if k in regroup_outs:
            d = NSA_HEAD_DIM
            res_sc = scratch[0]
            per = LANES // d
            for c in range(res_sc.shape[0]):
                res_sc[c] = res[:, c * LANES:(c + 1) * LANES]
            for l in range(CMP_STRIDE):
                for c in range(res_sc.shape[0]):
                    rows = res_sc[c, pl.ds(l, tm // CMP_STRIDE, stride=CMP_STRIDE), :]
                    for j in range(per):
                        out_refs[k][c * per + j, :, l * d:(l + 1) * d] = rows[:, j * d:(j + 1) * d]
            k += 1
            continue
        if k == onehot_out:
            row = pl.program_id(0) * tm + lax.broadcasted_iota(jnp.int32, (tm, b - a), 0)
            blk = lax.shift_right_logical(lax.rem(row, seq_len), int(math.log2(SEL_BLOCK)))
            slot = lax.broadcasted_iota(jnp.int32, (tm, b - a), 1) & (LANES - 1)
            res = jnp.where(slot - NSA_HEAD_DIM == blk, 1.0, res)
        out_refs[k][...] = res.astype(out_refs[k].dtype)
        k += 1
    for (a, b) in t_rows:
        out_refs[k][...] = (_dot_nt(wt_ref[a:b, :], xn) + tb_ref[a:b, :]).astype(out_refs[k].dtype)
        k += 1


def _norm_proj(h2d, gain, w_nat, nat_spec, w_t, t_bias, t_spec, tm, onehot_out=-1, regroup_outs=(),
               seq_len=1):
    t_tokens, d = h2d.shape
    nat_cols, off = [], 0
    for width, _ in nat_spec:
        nat_cols.append((off, off + width))
        off += width
    t_rows, off = [], 0
    for rows, _ in t_spec:
        t_rows.append((off, off + rows))
        off += rows
    in_specs = [pl.BlockSpec((tm, d), lambda i: (i, 0)),
                pl.BlockSpec((1, d), lambda i: (0, 0))]
    args = [h2d, gain.reshape(1, d)]
    if nat_spec:
        in_specs.append(pl.BlockSpec(w_nat.shape, lambda i: (0, 0)))
        args.append(w_nat)
    if t_spec:
        in_specs.append(pl.BlockSpec(w_t.shape, lambda i: (0, 0)))
        in_specs.append(pl.BlockSpec((w_t.shape[0], 1), lambda i: (0, 0)))
        args += [w_t, t_bias.reshape(-1, 1).astype(F32)]
    out_shape, out_specs, scratch = [], [], []
    for k, (width, dt) in enumerate(nat_spec):
        if k in regroup_outs:
            slab = (NSA_KV_GROUPS, t_tokens // CMP_STRIDE, CMP_STRIDE * NSA_HEAD_DIM)
            out_shape.append(jax.ShapeDtypeStruct(slab, dt))
            out_specs.append(pl.BlockSpec((slab[0], tm // CMP_STRIDE, slab[2]), lambda i: (0, i, 0)))
            scratch = [pltpu.VMEM((width // LANES, tm, LANES), F32)]
        else:
            out_shape.append(jax.ShapeDtypeStruct((t_tokens, width), dt))
            out_specs.append(pl.BlockSpec((tm, width), lambda i: (i, 0)))
    for rows, dt in t_spec:
        out_shape.append(jax.ShapeDtypeStruct((rows, t_tokens), dt))
        out_specs.append(pl.BlockSpec((rows, tm), lambda i: (0, i)))
    return pl.pallas_call(
        functools.partial(_proj_kernel, nat_cols=tuple(nat_cols), t_rows=tuple(t_rows),
                          onehot_out=onehot_out, regroup_outs=tuple(regroup_outs), seq_len=seq_len),
        grid=(t_tokens // tm,),
        in_specs=in_specs,
        out_specs=out_specs,
        out_shape=out_shape,
        scratch_shapes=scratch,
        compiler_params=_params(("parallel",)),
        name="norm_proj",
    )(*args)


def _ssd_kernel(xbc_ref, z_ref, dt_ref, dtT_ref, cw_ref, cb_ref, dtb_ref, dtbT_ref,
                alog_ref, alogT_ref, dskip_ref, norm_ref, y_ref,
                state_sc, carry_sc, win_sc):
    L = SSD_CHUNK
    hp = LANES // SSD_HEAD_DIM
    gw = SSD_INNER // SSD_GROUPS
    heads_per_group = SSD_HEADS // SSD_GROUPS

    @pl.when(pl.program_id(1) == 0)
    def _():
        state_sc[...] = jnp.zeros_like(state_sc)
        carry_sc[...] = jnp.zeros_like(carry_sc)

    xraw = xbc_ref[0]
    win_sc[0:SUBLANES, :] = carry_sc[...]
    win_sc[SUBLANES:SUBLANES + L, :] = xraw
    carry_sc[...] = xraw[L - SUBLANES:L, :]
    conv = cw_ref[0:1, :] * win_sc[SUBLANES - 3:SUBLANES - 3 + L, :]
    for k in range(1, SSD_CONV):
        o = SUBLANES - (SSD_CONV - 1) + k
        conv = conv + cw_ref[k:k + 1, :] * win_sc[o:o + L, :]
    xact = _silu(conv + cb_ref[...])
    xs = xact[:, :SSD_INNER]
    bm = xact[:, SSD_INNER:SSD_INNER + SSD_GROUPS * SSD_STATE].astype(BF16)
    cm = xact[:, SSD_INNER + SSD_GROUPS * SSD_STATE:].astype(BF16)

    dt = _softplus(dt_ref[0] + dtb_ref[...])
    d_a = dt * (-jnp.exp(alog_ref[...]))
    rows = lax.broadcasted_iota(jnp.int32, (L, L), 0)
    cols = lax.broadcasted_iota(jnp.int32, (L, L), 1)
    causal = rows >= cols
    cum = _dot(causal.astype(F32), d_a, precision=lax.Precision.HIGHEST)
    dt_t = _softplus(dtT_ref[...] + dtbT_ref[...])
    d_a_t = dt_t * (-jnp.exp(alogT_ref[...]))
    cum_t = _dot(d_a_t, (rows <= cols).astype(F32), precision=lax.Precision.HIGHEST)
    ecum = jnp.exp(cum)
    cum_last = cum[L - 1:L, :]
    to_end = jnp.exp(cum_last - cum) * dt
    e_last = jnp.exp(cum_last)

    lane = lax.broadcasted_iota(jnp.int32, (L, LANES), 1)
    first_head = lane < SSD_HEAD_DIM
    lane1 = lax.broadcasted_iota(jnp.int32, (1, LANES), 1)
    first_head1 = lane1 < SSD_HEAD_DIM

    def per_head(arr, h0, mask):
        return jnp.where(mask, arr[:, h0:h0 + 1], arr[:, h0 + 1:h0 + 2])

    y_blocks, ecum_blocks, toend_blocks, elast_blocks = [], [], [], []
    for g in range(SSD_GROUPS):
        bm_g = bm[:, g * SSD_STATE:(g + 1) * SSD_STATE]
        cm_g = cm[:, g * SSD_STATE:(g + 1) * SSD_STATE]
        cb = _dot_nt(cm_g, bm_g)
        for j in range(heads_per_group // hp):
            h0 = g * heads_per_group + j * hp
            c0 = h0 * SSD_HEAD_DIM
            x_pair = xs[:, c0:c0 + LANES].astype(BF16)
            res = []
            for hh in range(hp):
                h = h0 + hh
                seg = cum[:, h:h + 1] - cum_t[h:h + 1, :]
                decay = jnp.exp(jnp.where(causal, seg, -jnp.inf))
                wts = (cb * decay) * dt_t[h:h + 1, :]
                res.append(_dot(wts.astype(BF16), x_pair))
            y_blocks.append(jnp.where(first_head, res[0], res[1]))
            ecum_blocks.append(per_head(ecum, h0, first_head))
            toend_blocks.append(per_head(to_end, h0, first_head))
            elast_blocks.append(per_head(e_last, h0, first_head1))
    y_diag = jnp.concatenate(y_blocks, axis=1)
    ecum_x = jnp.concatenate(ecum_blocks, axis=1)
    toend_x = jnp.concatenate(toend_blocks, axis=1)
    elast_x = jnp.concatenate(elast_blocks, axis=1)

    xw = (xs * toend_x).astype(BF16)
    y_off_blocks = []
    for g in range(SSD_GROUPS):
        bm_g = bm[:, g * SSD_STATE:(g + 1) * SSD_STATE]
        cm_g = cm[:, g * SSD_STATE:(g + 1) * SSD_STATE]
        st = state_sc[g]
        y_off_blocks.append(_dot(cm_g, st.astype(BF16)))
        state_sc[g] = st * elast_x[:, g * gw:(g + 1) * gw] + \
            _dot_tn(bm_g, xw[:, g * gw:(g + 1) * gw])
    y_off = jnp.concatenate(y_off_blocks, axis=1) * ecum_x

    y = y_diag + y_off + dskip_ref[...] * xs
    yg = y * _silu(z_ref[0])
    outs = []
    for g in range(SSD_GROUPS):
        yg_g = yg[:, g * gw:(g + 1) * gw]
        ms = jnp.mean(yg_g * yg_g, axis=-1, keepdims=True)
        outs.append(yg_g * lax.rsqrt(ms + NORM_EPS))
    y_ref[0] = (jnp.concatenate(outs, axis=1) * norm_ref[...]).astype(y_ref.dtype)


def _ssd(xbc, z, dt_nat, dt_t, conv_w, conv_b, dt_bias, a_log, d_skip, ssd_norm, bsz, s):
    nch = s // SSD_CHUNK
    L = SSD_CHUNK
    pad = LANES - SSD_HEADS
    dtb = jnp.pad(dt_bias, (0, pad)).reshape(1, LANES)
    alog = jnp.pad(a_log, (0, pad)).reshape(1, LANES)
    dskip_x = jnp.repeat(d_skip, SSD_HEAD_DIM).reshape(1, SSD_INNER)
    const = lambda shape: pl.BlockSpec(shape, lambda b, c: tuple(0 for _ in shape))
    return pl.pallas_call(
        _ssd_kernel,
        grid=(bsz, nch),
        in_specs=[
            pl.BlockSpec((1, L, SSD_XBC), lambda b, c: (b, c, 0)),
            pl.BlockSpec((1, L, SSD_INNER), lambda b, c: (b, c, 0)),
            pl.BlockSpec((1, L, LANES), lambda b, c: (b, c, 0)),
            pl.BlockSpec((SSD_HEADS, L), lambda b, c: (0, b * nch + c)),
            const((SSD_CONV, SSD_XBC)), const((1, SSD_XBC)),
            const((1, LANES)), const((SSD_HEADS, 1)),
            const((1, LANES)), const((SSD_HEADS, 1)),
            const((1, SSD_INNER)), const((1, SSD_INNER)),
        ],
        out_specs=pl.BlockSpec((1, L, SSD_INNER), lambda b, c: (b, c, 0)),
        out_shape=jax.ShapeDtypeStruct((bsz, s, SSD_INNER), BF16),
        scratch_shapes=[
            pltpu.VMEM((SSD_GROUPS, SSD_STATE, SSD_INNER // SSD_GROUPS), F32),
            pltpu.VMEM((SUBLANES, SSD_XBC), F32),
            pltpu.VMEM((SUBLANES + L, SSD_XBC), F32),
        ],
        compiler_params=_params(("parallel", "arbitrary")),
        name="ssd_scan",
    )(xbc.reshape(bsz, s, SSD_XBC), z.reshape(bsz, s, SSD_INNER),
      dt_nat.reshape(bsz, s, LANES), dt_t,
      conv_w, conv_b.reshape(1, SSD_XBC), dtb, dt_bias.reshape(SSD_HEADS, 1),
      alog, a_log.reshape(SSD_HEADS, 1), dskip_x, ssd_norm.reshape(1, SSD_INNER))


def _conf_kernel(glu_ref, zc_ref, w_ref, b_ref, lng_ref, lnb_ref, y_ref, sh_sc, u_sc):
    ts = glu_ref.shape[1]
    n = sh_sc.shape[1]
    halo = n - ts
    first = halo - (CONF_CONV_WIDTH - 1)

    @pl.when(pl.program_id(1) == 0)
    def _():
        sh_sc[0, 0:halo, :] = jnp.zeros((halo, CONF_WIDTH), F32)

    sh_sc[0, halo:n, :] = glu_ref[0, :, :CONF_WIDTH] * jax.nn.sigmoid(glu_ref[0, :, CONF_WIDTH:])
    tiles = sh_sc[0].reshape(n // SUBLANES, SUBLANES, CONF_WIDTH)
    sub = lax.broadcasted_iota(jnp.int32, (n // SUBLANES - 1, SUBLANES, CONF_WIDTH), 1)
    for s in range(1, SUBLANES):
        rolled = pltpu.roll(tiles, SUBLANES - s, 1)
        shifted = jnp.where(sub < SUBLANES - s, rolled[:-1], rolled[1:])
        sh_sc[s, 0:n - SUBLANES, :] = shifted.reshape(n - SUBLANES, CONF_WIDTH)
    for r0 in range(0, ts, CONF_ROW_CHUNK):
        for c0 in range(0, CONF_WIDTH, CONF_LANE_CHUNK):
            cs = slice(c0, c0 + CONF_LANE_CHUNK)
            acc = None
            for k in range(CONF_CONV_WIDTH):
                a, s = divmod(first + k, SUBLANES)
                o = r0 + a * SUBLANES
                window = sh_sc[s, o:o + CONF_ROW_CHUNK, cs]
                term = w_ref[k][None, :, cs] * window.reshape(CONF_ROW_CHUNK // SUBLANES, SUBLANES, -1)
                acc = term if acc is None else acc + term
            u_sc[r0:r0 + CONF_ROW_CHUNK, cs] = acc.reshape(CONF_ROW_CHUNK, -1)
    sh_sc[0, 0:halo, :] = sh_sc[0, ts:n, :]
    u = u_sc[...] + b_ref[...]
    mu = jnp.mean(u, axis=-1, keepdims=True)
    var = jnp.mean(jnp.square(u - mu), axis=-1, keepdims=True)
    un = (u - mu) * lax.rsqrt(var + NORM_EPS) * lng_ref[...] + lnb_ref[...]
    y_ref[0] = (_silu(un) * _silu(zc_ref[0])).astype(y_ref.dtype)


def _conformer(glu, zc, conv_w, conv_b, ln_g, ln_b, bsz, s):
    ts = min(CONF_ROWS, s)
    halo = 32
    w_tiles = jnp.broadcast_to(conv_w[:, None, :], (CONF_CONV_WIDTH, SUBLANES, CONF_WIDTH))
    const = lambda shape: pl.BlockSpec(shape, lambda b, i: tuple(0 for _ in shape))
    return pl.pallas_call(
        _conf_kernel,
        grid=(bsz, s // ts),
        in_specs=[
            pl.BlockSpec((1, ts, 2 * CONF_WIDTH), lambda b, i: (b, i, 0)),
            pl.BlockSpec((1, ts, CONF_WIDTH), lambda b, i: (b, i, 0)),
            const(w_tiles.shape), const((1, CONF_WIDTH)),
            const((1, CONF_WIDTH)), const((1, CONF_WIDTH)),
        ],
        out_specs=pl.BlockSpec((1, ts, CONF_WIDTH), lambda b, i: (b, i, 0)),
        out_shape=jax.ShapeDtypeStruct((bsz, s, CONF_WIDTH), BF16),
        scratch_shapes=[pltpu.VMEM((SUBLANES, halo + ts, CONF_WIDTH), F32),
                        pltpu.VMEM((ts, CONF_WIDTH), F32)],
        compiler_params=_params(("parallel", "arbitrary")),
        name="conformer",
    )(glu.reshape(bsz, s, 2 * CONF_WIDTH), zc.reshape(bsz, s, CONF_WIDTH), w_tiles,
      conv_b.reshape(1, -1), ln_g.reshape(1, -1), ln_b.reshape(1, -1))


def _outproj_even_kernel(h_ref, a_ref, b_ref, wa_ref, wb_ref, o_ref):
    o_ref[...] = h_ref[...] + (_dot(a_ref[...], wa_ref[...]) + _dot(b_ref[...], wb_ref[...]))


def _outproj_even(h2d, ya, yb, w_out, tm):
    t_tokens, d = h2d.shape
    wa = w_out[:SSD_INNER].astype(BF16)
    wb = w_out[SSD_INNER:].astype(BF16)
    row = lambda w: pl.BlockSpec((tm, w), lambda i: (i, 0))
    return pl.pallas_call(
        _outproj_even_kernel,
        grid=(t_tokens // tm,),
        in_specs=[row(d), row(SSD_INNER), row(CONF_WIDTH),
                  pl.BlockSpec(wa.shape, lambda i: (0, 0)),
                  pl.BlockSpec(wb.shape, lambda i: (0, 0))],
        out_specs=row(d),
        out_shape=jax.ShapeDtypeStruct((t_tokens, d), F32),
        compiler_params=_params(("parallel",)),
        name="outproj_even",
    )(h2d, ya, yb, wa, wb)


def _outproj_odd_kernel(h_ref, ocw_ref, os_ref, zt_ref, w_ref, g_ref, o_ref, *, final):
    o = ocw_ref[...].astype(F32) + os_ref[...].astype(F32)
    y = (o * _silu(zt_ref[...])).astype(BF16)
    out = h_ref[...] + _dot_tn(y, w_ref[...])
    if final:
        ms = jnp.mean(out * out, axis=-1, keepdims=True)
        out = (out * lax.rsqrt(ms + NORM_EPS)) * g_ref[...]
    o_ref[...] = out


def _outproj_odd(h2d, ocw, osel, z_t, w_out, final_gain, final, tm):
    t_tokens, d = h2d.shape
    w = w_out.astype(BF16)
    col = pl.BlockSpec((NSA_WIDTH, tm), lambda i: (0, i))
    return pl.pallas_call(
        functools.partial(_outproj_odd_kernel, final=final),
        grid=(t_tokens // tm,),
        in_specs=[pl.BlockSpec((tm, d), lambda i: (i, 0)), col, col, col,
                  pl.BlockSpec(w.shape, lambda i: (0, 0)),
                  pl.BlockSpec((1, d), lambda i: (0, 0))],
        out_specs=pl.BlockSpec((tm, d), lambda i: (i, 0)),
        out_shape=jax.ShapeDtypeStruct((t_tokens, d), F32),
        compiler_params=_params(("parallel",)),
        name="outproj_odd",
    )(h2d, ocw, osel, z_t, w, final_gain.reshape(1, d))


def _compress_kernel(xk_ref, xv_ref, pek_ref, pev_ref, w1k_ref, w1v_ref, w2k_ref, w2vt_ref,
                     kc_ref, vct_ref):
    half = CMP_STRIDE * NSA_HEAD_DIM

    def hidden(x_ref, pe_ref, w1_ref):
        x = x_ref[0, 0]
        n = x.shape[0]
        lo = (x + pe_ref[0:1, :]).astype(BF16)
        hi = (x + pe_ref[1:2, :]).astype(BF16)
        h = _dot(lo, w1_ref[0:half, :]) + pltpu.roll(_dot(hi, w1_ref[half:2 * half, :]), n - 1, 0)
        return _silu(h).astype(BF16)

    kc_ref[0, 0] = _dot(hidden(xk_ref, pek_ref, w1k_ref), w2k_ref[...]).astype(kc_ref.dtype)
    vct = _dot_nt(w2vt_ref[...], hidden(xv_ref, pev_ref, w1v_ref))
    row = lax.broadcasted_iota(jnp.int32, vct.shape, 0)
    vct_ref[0, 0] = jnp.where(row == NSA_HEAD_DIM, 1.0, vct).astype(vct_ref.dtype)


def _compress(kc, vc, pe_k, w1_k, w2_k, pe_v, w1_v, w2_v, bsz, s):
    g, d = NSA_KV_GROUPS, NSA_HEAD_DIM
    nh = s // CMP_STRIDE
    half = CMP_STRIDE * d

    def regroup(t):
        return t.reshape(g, bsz, nh, half)

    const = lambda shape: pl.BlockSpec(shape, lambda b, gi: tuple(0 for _ in shape))
    blk = pl.BlockSpec((1, 1, nh, half), lambda b, gi: (gi, b, 0, 0))
    return pl.pallas_call(
        _compress_kernel,
        grid=(bsz, g),
        in_specs=[blk, blk, const((2, half)), const((2, half)),
                  const((2 * half, CMP_HIDDEN)), const((2 * half, CMP_HIDDEN)),
                  const((CMP_HIDDEN, d)), const((V_AUG_ROWS, CMP_HIDDEN))],
        out_specs=[pl.BlockSpec((1, 1, nh, d), lambda b, gi: (b, gi, 0, 0)),
                   pl.BlockSpec((1, 1, V_AUG_ROWS, nh), lambda b, gi: (b, gi, 0, 0))],
        out_shape=[jax.ShapeDtypeStruct((bsz, g, nh, d), BF16),
                   jax.ShapeDtypeStruct((bsz, g, V_AUG_ROWS, nh), BF16)],
        compiler_params=_params(("parallel", "parallel")),
        name="compress_kv",
    )(regroup(kc), regroup(vc), pe_k.reshape(2, half), pe_v.reshape(2, half),
      w1_k.astype(BF16), w1_v.astype(BF16), w2_k.astype(BF16),
      jnp.pad(w2_v.T, ((0, V_AUG_ROWS - d), (0, 0))).astype(BF16))


def _selection_bias(imp, t0, topk, v_sc, cnt_sc):
    n_sel, tq = imp.shape
    shift = int(math.log2(SEL_BLOCK))
    ngroups = n_sel // SUBLANES
    jidx = lax.broadcasted_iota(jnp.int32, (n_sel, tq), 0)
    t = t0 + lax.broadcasted_iota(jnp.int32, (n_sel, tq), 1)
    cur = lax.shift_right_logical(t, shift)
    valid = jidx * SEL_BLOCK <= t
    forced = (jidx == 0) | (jidx == cur) | (jidx == cur - 1)
    v_sc[...] = jnp.where(valid, imp + jnp.where(forced, FORCE_BONUS, 0.0), -jnp.inf)
    cnt_sc[...] = jnp.zeros((n_sel, tq), jnp.int32)
    last_block = lax.shift_right_logical(t0 + tq - 1, shift)
    last_group = lax.shift_right_logical(last_block, int(math.log2(SUBLANES)))
    sub = lax.broadcasted_iota(jnp.int32, (SUBLANES, tq), 0)

    def group(ref, gi):
        return ref[gi * SUBLANES:(gi + 1) * SUBLANES, :]

    def rows_of(vals):
        return [jnp.broadcast_to(vals[ii:ii + 1, :], (SUBLANES, tq)) for ii in range(SUBLANES)]

    for hi in range(ngroups):
        @pl.when((hi <= last_group) & (last_block >= topk))
        def _():
            v_hi = group(v_sc, hi)
            rows_hi = rows_of(v_hi)
            own = jnp.zeros((SUBLANES, tq), jnp.int32)
            for ii in range(SUBLANES):
                own = own + jnp.where(sub > ii, jnp.where(rows_hi[ii] >= v_hi, 1, 0),
                                      jnp.where(rows_hi[ii] > v_hi, 1, 0))
            for lo in range(hi):
                v_lo = group(v_sc, lo)
                add_lo = jnp.zeros((SUBLANES, tq), jnp.int32)
                for ii in range(SUBLANES):
                    add_lo = add_lo + jnp.where(rows_hi[ii] > v_lo, 1, 0)
                cnt_sc[lo * SUBLANES:(lo + 1) * SUBLANES, :] += add_lo
                for row in rows_of(v_lo):
                    own = own + jnp.where(row >= v_hi, 1, 0)
            cnt_sc[hi * SUBLANES:(hi + 1) * SUBLANES, :] += own

    return jnp.where(valid & (cnt_sc[...] < topk), 0.0, NEG_BIG)


def _lane_tile(x, reps):
    return jnp.concatenate([x] * reps, axis=1)


def _window_branch(qa, k_refs, v_refs, i, tq):
    nblk = len(k_refs)
    krow = lax.broadcasted_iota(jnp.int32, (tq, tq), 0)
    qcol = lax.broadcasted_iota(jnp.int32, (tq, tq), 1)
    rel0 = qcol - krow
    s_blocks = []
    for jj in range(nblk):
        back = nblk - 1 - jj
        if back == 0:
            bias = jnp.where(rel0 >= 0, 0.0, NEG_BIG)
        else:
            limit = jnp.where(i >= back, WINDOW, -1)
            bias = jnp.where(rel0 + back * tq < limit, 0.0, NEG_BIG)
        s_blocks.append(_dot(k_refs[jj], qa) + _lane_tile(bias, NSA_HPG))
    m = jnp.max(s_blocks[0], axis=0, keepdims=True)
    for s in s_blocks[1:]:
        m = jnp.maximum(m, jnp.max(s, axis=0, keepdims=True))
    acc = _dot(v_refs[0], jnp.exp(s_blocks[0] - m).astype(BF16))
    for jj in range(1, nblk):
        acc = acc + _dot(v_refs[jj], jnp.exp(s_blocks[jj] - m).astype(BF16))
    return acc


def _cmp_win_kernel(*refs, nblk, n_cmp, topk):
    (qt_ref, kc_ref, vct_ref, ovt_ref, kw_ref, vw_ref, gl_ref, gb_ref,
     o_ref, selb_ref, qa_sc, v_sc, cnt_sc) = refs
    tq = qt_ref.shape[2]
    d = NSA_HEAD_DIM
    w = NSA_HPG * tq
    nc = kc_ref.shape[2]
    n_sel = ovt_ref.shape[0]
    i = pl.program_id(2)
    t0 = i * tq
    qa_sc[d:, :] = jnp.zeros((qa_sc.shape[0] - d, w), qa_sc.dtype)
    for r in range(NSA_HPG):
        qa_sc[0:d, r * tq:(r + 1) * tq] = qt_ref[r]

    starts = [pl.multiple_of(jnp.maximum(i - (nblk - 1) + jj, 0) * tq, tq) for jj in range(nblk)]
    k_blocks = [kw_ref[pl.ds(st, tq), :] for st in starts]
    v_blocks = [vw_ref[0, :, pl.ds(st, tq)] for st in starts]
    acc_w = _window_branch(qa_sc[...], k_blocks, v_blocks, i, tq)
    gates_w = jax.nn.sigmoid(gl_ref[0, 2] + gb_ref[0, 2])
    o_win = []
    for r in range(NSA_HPG):
        a = acc_w[:, r * tq:(r + 1) * tq]
        o_win.append(a[0:d] * (1.0 / a[d:d + 1]) * gates_w[r:r + 1, :])

    cidx = lax.broadcasted_iota(jnp.int32, (nc, tq), 0)
    t = t0 + lax.broadcasted_iota(jnp.int32, (nc, tq), 1)
    cbias = jnp.where((cidx * CMP_STRIDE + (CMP_BLOCK - 1) <= t) & (cidx < n_cmp), 0.0, -jnp.inf)
    s = _dot(kc_ref[0, 0], qa_sc[0:d, :]) + _lane_tile(cbias, NSA_HPG)
    m = jnp.max(s, axis=0, keepdims=True)
    m = jnp.where(m > -jnp.inf, m, 0.0)
    e = jnp.exp(s - m).astype(BF16)
    both = _dot(jnp.concatenate([vct_ref[0, 0], ovt_ref[...]], axis=0), e)
    rinv = 1.0 / jnp.maximum(both[d:d + 1, :], 1e-30)
    gates_c = jax.nn.sigmoid(gl_ref[0, 0] + gb_ref[0, 0])
    imp = jnp.zeros((n_sel, tq), F32)
    for r in range(NSA_HPG):
        cols = slice(r * tq, (r + 1) * tq)
        o_cmp = both[0:d, cols] * rinv[:, cols] * gates_c[r:r + 1, :]
        o_ref[r] = (o_cmp + o_win[r]).astype(o_ref.dtype)
        imp = imp + both[V_AUG_ROWS:, cols] * rinv[:, cols]
    bias = _selection_bias(imp, t0, topk, v_sc, cnt_sc)
    n_pad = selb_ref.shape[2] - bias.shape[0]
    if n_pad:
        bias = jnp.concatenate([bias, jnp.zeros((n_pad, tq), F32)], axis=0)
    selb_ref[0, 0] = bias.astype(selb_ref.dtype)


def _gate_specs(nq, branch, nbranch):
    return (pl.BlockSpec((1, nbranch, SUBLANES, ATT_TQ), lambda b, g, i: (g, branch, 0, b * nq + i)),
            pl.BlockSpec((1, nbranch, SUBLANES, 1), lambda b, g, i: (g, branch, 0, 0)))


def _cmp_win_attn(q_t, kcmp, vcmp_t, kw_a, vw_a, gl_t, gbias, bsz, s):
    g, d = NSA_KV_GROUPS, NSA_HEAD_DIM
    tq = ATT_TQ
    nq = s // tq
    nblk = WINDOW // tq + 1
    nc = s // CMP_STRIDE
    n_cmp = (s - CMP_BLOCK) // CMP_STRIDE + 1
    n_sel = s // SEL_BLOCK
    n_selp = _sel_pad(n_sel)
    topk = min(SEL_TOPK, n_sel)
    cmp_start = np.arange(nc) * CMP_STRIDE
    sel_start = np.arange(n_sel) * SEL_BLOCK
    ov = ((cmp_start[:, None] < sel_start[None, :] + SEL_BLOCK) &
          (cmp_start[:, None] + CMP_BLOCK > sel_start[None, :]) &
          (np.arange(nc)[:, None] < n_cmp))
    ovt = jnp.asarray(ov.T.astype(np.float32), dtype=BF16)
    gl_spec, gb_spec = _gate_specs(nq, 0, 3)
    t_tokens = bsz * s
    head_tile = pl.BlockSpec((NSA_HPG, d, tq), lambda b, gi, i: (gi, 0, b * nq + i))
    return pl.pallas_call(
        functools.partial(_cmp_win_kernel, nblk=nblk, n_cmp=n_cmp, topk=topk),
        grid=(bsz, g, nq),
        in_specs=[
            head_tile,
            pl.BlockSpec((1, 1, nc, d), lambda b, gi, i: (b, gi, 0, 0)),
            pl.BlockSpec((1, 1, V_AUG_ROWS, nc), lambda b, gi, i: (b, gi, 0, 0)),
            pl.BlockSpec((n_sel, nc), lambda b, gi, i: (0, 0)),
            pl.BlockSpec((s, LANES), lambda b, gi, i: (b, gi)),
            pl.BlockSpec((1, V_AUG_ROWS, s), lambda b, gi, i: (gi, 0, b)),
            gl_spec, gb_spec,
        ],
        out_specs=[head_tile, pl.BlockSpec((1, 1, n_selp, tq), lambda b, gi, i: (b, gi, 0, i))],
        out_shape=[jax.ShapeDtypeStruct((NSA_HEADS, d, t_tokens), BF16),
                   jax.ShapeDtypeStruct((bsz, g, n_selp, s), BF16)],
        scratch_shapes=[pltpu.VMEM((LANES, NSA_HPG * tq), BF16),
                        pltpu.VMEM((n_sel, tq), F32), pltpu.VMEM((n_sel, tq), jnp.int32)],
        compiler_params=_params(("parallel", "parallel", "parallel")),
        name="cmp_win_attn",
    )(q_t, kcmp, vcmp_t, ovt, kw_a, vw_a, gl_t, gbias)


def _sel_pad(n_sel):
    return -(-n_sel // NSA_HEAD_DIM) * NSA_HEAD_DIM


def _sel_attn_kernel(qt_ref, ka_ref, va_ref, selb_ref, gl_ref, gb_ref, o_ref,
                     qa_sc, sa_sc, sb_sc, ma_sc, mb_sc, m_sc, acc_sc, *, tk):
    tq = qt_ref.shape[2]
    d = NSA_HEAD_DIM
    w = NSA_HPG * tq
    i = pl.program_id(2)
    t0 = i * tq
    last = (i * tq) // tk
    for r in range(NSA_HPG):
        qa_sc[0:d, r * tq:(r + 1) * tq] = qt_ref[r]
        qa_sc[d:, r * tq:(r + 1) * tq] = selb_ref[0, 0]
    m_sc[...] = jnp.full(m_sc.shape, NEG_BIG, F32)
    acc_sc[...] = jnp.zeros(acc_sc.shape, F32)

    def scores(j, s_ref, mx_ref, diagonal):
        k0 = pl.multiple_of(j * tk, tk)
        s = _dot(ka_ref[pl.ds(k0, tk), :], qa_sc[...])
        if diagonal:
            kpos = k0 + lax.broadcasted_iota(jnp.int32, (tk, w), 0)
            t = t0 + (lax.broadcasted_iota(jnp.int32, (tk, w), 1) & (tq - 1))
            s = jnp.where(kpos <= t, s, NEG_BIG)
        s_ref[...] = s
        mx_ref[...] = jnp.max(s, axis=0, keepdims=True)

    def accumulate(j, s_ref, mx_ref):
        k0 = pl.multiple_of(j * tk, tk)
        m_old = m_sc[...]
        m_new = jnp.maximum(m_old, mx_ref[...])
        p = jnp.exp(s_ref[...] - m_new).astype(BF16)
        acc_sc[...] = jnp.exp(m_old - m_new) * acc_sc[...] + _dot(va_ref[0, :, pl.ds(k0, tk)], p)
        m_sc[...] = m_new

    scores(last, sa_sc, ma_sc, True)

    def pair(mi, carry):
        j = 2 * mi
        scores(j, sb_sc, mb_sc, False)
        accumulate(jnp.where(mi == 0, last, j - 1), sa_sc, ma_sc)
        scores(j + 1, sa_sc, ma_sc, False)
        accumulate(j, sb_sc, mb_sc)
        return carry

    lax.fori_loop(0, last // 2, pair, 0)

    @pl.when(last % 2 == 1)
    def _():
        scores(last - 1, sb_sc, mb_sc, False)
        accumulate(jnp.where(last == 1, last, last - 2), sa_sc, ma_sc)
        accumulate(last - 1, sb_sc, mb_sc)

    @pl.when(last % 2 == 0)
    def _():
        accumulate(jnp.where(last == 0, last, last - 1), sa_sc, ma_sc)

    gates = jax.nn.sigmoid(gl_ref[0, 0] + gb_ref[0, 0])
    for r in range(NSA_HPG):
        a = acc_sc[:, r * tq:(r + 1) * tq]
        o_ref[r] = (a[0:d] * (1.0 / a[d:d + 1]) * gates[r:r + 1, :]).astype(o_ref.dtype)


def _sel_attn(q_t, ka, va, selb, gl_t, gbias, bsz, s):
    g, d = NSA_KV_GROUPS, NSA_HEAD_DIM
    tq = ATT_TQ
    tk = min(SEL_TK, s)
    nq = s // tq
    w = NSA_HPG * tq
    n_selp = selb.shape[2]
    kd = d + n_selp
    gl_spec, gb_spec = _gate_specs(nq, 1, 1)
    return pl.pallas_call(
        functools.partial(_sel_attn_kernel, tk=tk),
        grid=(bsz, g, nq),
        in_specs=[
            pl.BlockSpec((NSA_HPG, d, tq), lambda b, gi, i: (gi, 0, b * nq + i)),
            pl.BlockSpec((s, kd), lambda b, gi, i: (b, gi)),
            pl.BlockSpec((1, V_AUG_ROWS, s), lambda b, gi, i: (gi, 0, b)),
            pl.BlockSpec((1, 1, n_selp, tq), lambda b, gi, i: (b, gi, 0, i)),
            gl_spec, gb_spec,
        ],
        out_specs=pl.BlockSpec((NSA_HPG, d, tq), lambda b, gi, i: (gi, 0, b * nq + i)),
        out_shape=jax.ShapeDtypeStruct((NSA_HEADS, d, bsz * s), BF16),
        scratch_shapes=[pltpu.VMEM((kd, w), BF16),
                        pltpu.VMEM((tk, w), F32), pltpu.VMEM((tk, w), F32),
                        pltpu.VMEM((1, w), F32), pltpu.VMEM((1, w), F32),
                        pltpu.VMEM((1, w), F32), pltpu.VMEM((V_AUG_ROWS, w), F32)],
        compiler_params=_params(("parallel", "parallel", "arbitrary")),
        name="sel_attn",
    )(q_t, ka, va, selb, gl_t, gbias)


def _even_layer(h2d, bsz, s, gain, w_in, ssd_conv_w, ssd_conv_b, dt_bias, a_log, d_skip, ssd_norm,
                conf_conv_w, conf_conv_b, conf_ln_g, conf_ln_b, w_out):
    o1 = SSD_INNER
    o2 = o1 + SSD_XBC
    o3 = o2 + SSD_HEADS
    o4 = o3 + 2 * CONF_WIDTH
    w_dt = w_in[:, o2:o3]
    w_nat = jnp.concatenate(
        [w_in[:, :o2], w_in[:, o3:], jnp.pad(w_dt, ((0, 0), (0, LANES - SSD_HEADS)))], axis=1).astype(BF16)
    nat_spec = [(SSD_INNER, F32), (SSD_XBC, F32), (2 * CONF_WIDTH, F32), (CONF_WIDTH, F32), (LANES, F32)]
    tm = min(PROJ_ROWS, s)
    z, xbc, glu, zc, dt_nat, dt_t = _norm_proj(
        h2d, gain, w_nat, nat_spec, w_dt.T.astype(BF16), jnp.zeros((SSD_HEADS,), F32),
        [(SSD_HEADS, F32)], tm)
    y_a = _ssd(xbc, z, dt_nat, dt_t, ssd_conv_w, ssd_conv_b, dt_bias, a_log, d_skip, ssd_norm, bsz, s)
    y_b = _conformer(glu, zc, conf_conv_w, conf_conv_b, conf_ln_g, conf_ln_b, bsz, s)
    return _outproj_even(h2d, y_a.reshape(bsz * s, SSD_INNER), y_b.reshape(bsz * s, CONF_WIDTH), w_out, tm)


def _odd_layer(h2d, bsz, s, gain, w_in, gate_bias, pe_k, w1_k, w2_k, pe_v, w1_v, w2_v, w_out,
               final_gain, final):
    g, r, d = NSA_KV_GROUPS, NSA_HPG, NSA_HEAD_DIM
    t_tokens = bsz * s
    sizes = [NSA_WIDTH] + [KV_WIDTH] * 6 + [3 * NSA_HEADS, NSA_WIDTH]
    offs = np.cumsum([0] + sizes)
    col = lambda k: w_in[:, offs[k]:offs[k + 1]]
    w_q, w_kc, w_vc, w_ks, w_vs, w_kw, w_vw, w_gl, w_z = [col(k) for k in range(9)]
    perm = np.array([[[(gi * r + ri) * 3 + br for ri in range(r)] for br in range(3)] for gi in range(g)])
    w_gl_g = jnp.pad(w_gl[:, perm.reshape(-1)].reshape(D_MODEL, g * 3, r),
                     ((0, 0), (0, 0), (0, SUBLANES - r))).reshape(D_MODEL, g * 3 * SUBLANES)
    gbias = jnp.pad(gate_bias[perm.reshape(-1)].reshape(g * 3, r), ((0, 0), (0, SUBLANES - r)))
    gbias = gbias.reshape(g, 3, SUBLANES, 1)
    scale = NSA_HEAD_DIM ** -0.5
    assert d + _sel_pad(s // SEL_BLOCK) == LANES

    def key_slots(w):
        return jnp.pad(w.reshape(D_MODEL, g, d), ((0, 0), (0, 0), (0, LANES - d))).reshape(D_MODEL, g * LANES)

    def value_rows(w):
        return jnp.pad(w.reshape(D_MODEL, g, d), ((0, 0), (0, 0), (0, V_AUG_ROWS - d))).reshape(D_MODEL, g * V_AUG_ROWS)

    ones_row = jnp.zeros((g, V_AUG_ROWS), F32).at[:, d].set(1.0).reshape(-1)
    w_nat = jnp.concatenate([w_kc, w_vc, key_slots(w_ks), key_slots(w_kw)], axis=1).astype(BF16)
    w_t = jnp.concatenate([w_q * scale, value_rows(w_vs), value_rows(w_vw), w_gl_g, w_z], axis=1).T.astype(BF16)
    t_bias = jnp.concatenate([jnp.zeros((NSA_WIDTH,), F32), ones_row, ones_row,
                              jnp.zeros((g * 3 * SUBLANES + NSA_WIDTH,), F32)])
    nat_spec = [(KV_WIDTH, F32), (KV_WIDTH, F32), (g * LANES, BF16), (g * LANES, BF16)]
    t_spec = [(NSA_WIDTH, BF16), (g * V_AUG_ROWS, BF16), (g * V_AUG_ROWS, BF16),
              (g * 3 * SUBLANES, F32), (NSA_WIDTH, F32)]
    tm = min(PROJ_ROWS, s)
    kc, vc, ks_a, kw_a, q_t, vs_a, vw_a, gl_t, z_t = _norm_proj(
        h2d, gain, w_nat, nat_spec, w_t, t_bias, t_spec, tm, onehot_out=2, regroup_outs=(0, 1), seq_len=s)

    q_t = q_t.reshape(NSA_HEADS, d, t_tokens)
    gl_t = gl_t.reshape(g, 3, SUBLANES, t_tokens)
    vs_a = vs_a.reshape(g, V_AUG_ROWS, t_tokens)
    vw_a = vw_a.reshape(g, V_AUG_ROWS, t_tokens)

    kcmp, vcmp_t = _compress(kc, vc, pe_k, w1_k, w2_k, pe_v, w1_v, w2_v, bsz, s)
    o_cw, selb = _cmp_win_attn(q_t, kcmp, vcmp_t, kw_a, vw_a, gl_t, gbias, bsz, s)
    o_sel = _sel_attn(q_t, ks_a, vs_a, selb, gl_t, gbias, bsz, s)
    flat = lambda o: o.reshape(NSA_WIDTH, t_tokens)
    return _outproj_odd(h2d, flat(o_cw), flat(o_sel), z_t, w_out, final_gain, final, tm)


def kernel(x, e_norm, e_w_in, e_ssd_conv_w, e_ssd_conv_b, e_dt_bias, e_a_log, e_d_skip, e_ssd_norm,
           e_conf_conv_w, e_conf_conv_b, e_conf_ln_g, e_conf_ln_b, e_w_out, o_norm, o_w_in, o_gate_bias,
           o_cmp_pe_k, o_cmp_w1_k, o_cmp_w2_k, o_cmp_pe_v, o_cmp_w1_v, o_cmp_w2_v, o_w_out, final_norm):
    bsz, s, d = x.shape
    depth = e_norm.shape[0] + o_norm.shape[0]
    h = x.reshape(bsz * s, d)
    for layer in range(depth):
        i = layer // 2
        if layer % 2 == 0:
            h = _even_layer(h, bsz, s, e_norm[i], e_w_in[i], e_ssd_conv_w[i], e_ssd_conv_b[i], e_dt_bias[i],
                            e_a_log[i], e_d_skip[i], e_ssd_norm[i], e_conf_conv_w[i], e_conf_conv_b[i],
                            e_conf_ln_g[i], e_conf_ln_b[i], e_w_out[i])
        else:
            h = _odd_layer(h, bsz, s, o_norm[i], o_w_in[i], o_gate_bias[i], o_cmp_pe_k[i], o_cmp_w1_k[i],
                           o_cmp_w2_k[i], o_cmp_pe_v[i], o_cmp_w1_v[i], o_cmp_w2_v[i], o_w_out[i],
                           final_norm, layer == depth - 1)
    return h.reshape(bsz, s, d)
```

```python
import functools
import math

import numpy as np
import jax
import jax.numpy as jnp
from jax import lax
from jax.experimental import pallas as pl
from jax.experimental.pallas import tpu as pltpu

F32 = jnp.float32
BF16 = jnp.bfloat16

D_MODEL = 1024
SSD_HEADS = 16
SSD_HEAD_DIM = 64
SSD_INNER = 1024
SSD_GROUPS = 2
SSD_STATE = 128
SSD_CONV = 4
SSD_CHUNK = 128
SSD_XBC = SSD_INNER + 2 * SSD_GROUPS * SSD_STATE
CONF_WIDTH = 1024
CONF_CONV_WIDTH = 31
NSA_HEADS = 16
NSA_KV_GROUPS = 4
NSA_HPG = 4
NSA_HEAD_DIM = 64
NSA_WIDTH = 1024
KV_WIDTH = 256
CMP_BLOCK = 32
CMP_STRIDE = 16
CMP_HIDDEN = 256
SEL_BLOCK = 64
SEL_TOPK = 16
WINDOW = 512
FORCE_BONUS = 1e6
NORM_EPS = 1e-6

LANES = 128
SUBLANES = 8
VMEM_LIMIT_BYTES = 56 * 1024 * 1024

PROJ_ROWS = 512
CONF_ROWS = 256
CONF_ROW_CHUNK = 32
CONF_LANE_CHUNK = 512
ATT_TQ = 256
WIN_TQ = 128
NEG_BIG = -1e30
V_AUG_ROWS = 80


def _dot(a, b, precision=None):
    return jnp.dot(a, b, preferred_element_type=F32, precision=precision)


def _dot_nt(a, b):
    return lax.dot_general(a, b, (((1,), (1,)), ((), ())), preferred_element_type=F32)


def _dot_tn(a, b):
    return lax.dot_general(a, b, (((0,), (0,)), ((), ())), preferred_element_type=F32)


def _silu(x):
    return x * jax.nn.sigmoid(x)


def _softplus(x):
    return jnp.maximum(x, 0.0) + jnp.log1p(jnp.exp(-jnp.abs(x)))


def _params(semantics):
    return pltpu.CompilerParams(dimension_semantics=semantics,
                                vmem_limit_bytes=VMEM_LIMIT_BYTES)


def _proj_kernel(*refs, nat_cols, t_rows, onehot_out, regroup_outs, seq_len):
    x_ref, g_ref = refs[0], refs[1]
    pos = 2
    w_ref = wt_ref = tb_ref = None
    if nat_cols:
        w_ref = refs[pos]
        pos += 1
    if t_rows:
        wt_ref, tb_ref = refs[pos], refs[pos + 1]
        pos += 2
    n_out = len(nat_cols) + len(t_rows)
    out_refs = refs[pos:pos + n_out]
    scratch = refs[pos + n_out:]
    x = x_ref[...]
    tm = x.shape[0]
    ms = jnp.mean(x * x, axis=-1, keepdims=True)
    xn = ((x * lax.rsqrt(ms + NORM_EPS)) * g_ref[...]).astype(BF16)
    k = 0
    for (a, b) in nat_cols:
        res = _dot(xn, w_ref[:, a:b])
        if k in regroup_outs:
            d = NSA_HEAD_DIM
            res_sc = scratch[0]
            per = LANES // d
            for c in range(res_sc.shape[0]):
                res_sc[c] = res[:, c * LANES:(c + 1) * LANES]
            for l in range(CMP_STRIDE):
                for c in range(res_sc.shape[0]):
                    rows = res_sc[c, pl.ds(l, tm // CMP_STRIDE, stride=CMP_STRIDE), :]
                    for j in range(per):
                        out_refs[k][c * per + j, :, l * d:(l + 1) * d] = rows[:, j * d:(j + 1) * d]
            k += 1
            continue
        if k == onehot_out:
            row = pl.program_id(0) * tm + lax.broadcasted_iota(jnp.int32, (tm, b - a), 0)
            blk = lax.shift_right_logical(lax.rem(row, seq_len), int(math.log2(SEL_BLOCK)))
            slot = lax.broadcasted_iota(jnp.int32, (tm, b - a), 1) & (LANES - 1)
            res = jnp.where(slot - NSA_HEAD_DIM == blk, 1.0, res)
        out_refs[k][...] = res.astype(out_refs[k].dtype)
        k += 1
    for (a, b) in t_rows:
        out_refs[k][...] = (_dot_nt(wt_ref[a:b, :], xn) + tb_ref[a:b, :]).astype(out_refs[k].dtype)
        k += 1


def _norm_proj(h2d, gain, w_nat, nat_spec, w_t, t_bias, t_spec, tm, onehot_out=-1, regroup_outs=(),
               seq_len=1):
    t_tokens, d = h2d.shape
    nat_cols, off = [], 0
    for width, _ in nat_spec:
        nat_cols.append((off, off + width))
        off += width
    t_rows, off = [], 0
    for rows, _ in t_spec:
        t_rows.append((off, off + rows))
        off += rows
    in_specs = [pl.BlockSpec((tm, d), lambda i: (i, 0)),
                pl.BlockSpec((1, d), lambda i: (0, 0))]
    args = [h2d, gain.reshape(1, d)]
    if nat_spec:
        in_specs.append(pl.BlockSpec(w_nat.shape, lambda i: (0, 0)))
        args.append(w_nat)
    if t_spec:
        in_specs.append(pl.BlockSpec(w_t.shape, lambda i: (0, 0)))
        in_specs.append(pl.BlockSpec((w_t.shape[0], 1), lambda i: (0, 0)))
        args += [w_t, t_bias.reshape(-1, 1).astype(F32)]
    out_shape, out_specs, scratch = [], [], []
    for k, (width, dt) in enumerate(nat_spec):
        if k in regroup_outs:
            slab = (NSA_KV_GROUPS, t_tokens // CMP_STRIDE, CMP_STRIDE * NSA_HEAD_DIM)
            out_shape.append(jax.ShapeDtypeStruct(slab, dt))
            out_specs.append(pl.BlockSpec((slab[0], tm // CMP_STRIDE, slab[2]), lambda i: (0, i, 0)))
            scratch = [pltpu.VMEM((width // LANES, tm, LANES), F32)]
        else:
            out_shape.append(jax.ShapeDtypeStruct((t_tokens, width), dt))
            out_specs.append(pl.BlockSpec((tm, width), lambda i: (i, 0)))
    for rows, dt in t_spec:
        out_shape.append(jax.ShapeDtypeStruct((rows, t_tokens), dt))
        out_specs.append(pl.BlockSpec((rows, tm), lambda i: (0, i)))
    return pl.pallas_call(
        functools.partial(_proj_kernel, nat_cols=tuple(nat_cols), t_rows=tuple(t_rows),
                          onehot_out=onehot_out, regroup_outs=tuple(regroup_outs), seq_len=seq_len),
        grid=(t_tokens // tm,),
        in_specs=in_specs,
        out_specs=out_specs,
        out_shape=out_shape,
        scratch_shapes=scratch,
        compiler_params=_params(("parallel",)),
        name="norm_proj",
    )(*args)


def _ssd_kernel(xbc_ref, z_ref, dt_ref, dtT_ref, cw_ref, cb_ref, dtb_ref, dtbT_ref,
                alog_ref, alogT_ref, dskip_ref, norm_ref, y_ref,
                state_sc, carry_sc, win_sc):
    L = SSD_CHUNK
    hp = LANES // SSD_HEAD_DIM
    gw = SSD_INNER // SSD_GROUPS
    heads_per_group = SSD_HEADS // SSD_GROUPS

    @pl.when(pl.program_id(1) == 0)
    def _():
        state_sc[...] = jnp.zeros_like(state_sc)
        carry_sc[...] = jnp.zeros_like(carry_sc)

    xraw = xbc_ref[0]
    win_sc[0:SUBLANES, :] = carry_sc[...]
    win_sc[SUBLANES:SUBLANES + L, :] = xraw
    carry_sc[...] = xraw[L - SUBLANES:L, :]
    conv = cw_ref[0:1, :] * win_sc[SUBLANES - 3:SUBLANES - 3 + L, :]
    for k in range(1, SSD_CONV):
        o = SUBLANES - (SSD_CONV - 1) + k
        conv = conv + cw_ref[k:k + 1, :] * win_sc[o:o + L, :]
    xact = _silu(conv + cb_ref[...])
    xs = xact[:, :SSD_INNER]
    bm = xact[:, SSD_INNER:SSD_INNER + SSD_GROUPS * SSD_STATE].astype(BF16)
    cm = xact[:, SSD_INNER + SSD_GROUPS * SSD_STATE:].astype(BF16)

    dt = _softplus(dt_ref[0] + dtb_ref[...])
    d_a = dt * (-jnp.exp(alog_ref[...]))
    rows = lax.broadcasted_iota(jnp.int32, (L, L), 0)
    cols = lax.broadcasted_iota(jnp.int32, (L, L), 1)
    causal = rows >= cols
    cum = _dot(causal.astype(F32), d_a, precision=lax.Precision.HIGHEST)
    dt_t = _softplus(dtT_ref[...] + dtbT_ref[...])
    d_a_t = dt_t * (-jnp.exp(alogT_ref[...]))
    cum_t = _dot(d_a_t, (rows <= cols).astype(F32), precision=lax.Precision.HIGHEST)
    ecum = jnp.exp(cum)
    cum_last = cum[L - 1:L, :]
    to_end = jnp.exp(cum_last - cum) * dt
    e_last = jnp.exp(cum_last)

    lane = lax.broadcasted_iota(jnp.int32, (L, LANES), 1)
    first_head = lane < SSD_HEAD_DIM
    lane1 = lax.broadcasted_iota(jnp.int32, (1, LANES), 1)
    first_head1 = lane1 < SSD_HEAD_DIM

    def per_head(arr, h0, mask):
        return jnp.where(mask, arr[:, h0:h0 + 1], arr[:, h0 + 1:h0 + 2])

    y_blocks, ecum_blocks, toend_blocks, elast_blocks = [], [], [], []
    for g in range(SSD_GROUPS):
        bm_g = bm[:, g * SSD_STATE:(g + 1) * SSD_STATE]
        cm_g = cm[:, g * SSD_STATE:(g + 1) * SSD_STATE]
        cb = _dot_nt(cm_g, bm_g)
        for j in range(heads_per_group // hp):
            h0 = g * heads_per_group + j * hp
            c0 = h0 * SSD_HEAD_DIM
            x_pair = xs[:, c0:c0 + LANES].astype(BF16)
            res = []
            for hh in range(hp):
                h = h0 + hh
                seg = cum[:, h:h + 1] - cum_t[h:h + 1, :]
                decay = jnp.exp(jnp.where(causal, seg, -jnp.inf))
                wts = (cb * decay) * dt_t[h:h + 1, :]
                res.append(_dot(wts.astype(BF16), x_pair))
            y_blocks.append(jnp.where(first_head, res[0], res[1]))
            ecum_blocks.append(per_head(ecum, h0, first_head))
            toend_blocks.append(per_head(to_end, h0, first_head))
            elast_blocks.append(per_head(e_last, h0, first_head1))
    y_diag = jnp.concatenate(y_blocks, axis=1)
    ecum_x = jnp.concatenate(ecum_blocks, axis=1)
    toend_x = jnp.concatenate(toend_blocks, axis=1)
    elast_x = jnp.concatenate(elast_blocks, axis=1)

    xw = (xs * toend_x).astype(BF16)
    y_off_blocks = []
    for g in range(SSD_GROUPS):
        bm_g = bm[:, g * SSD_STATE:(g + 1) * SSD_STATE]
        cm_g = cm[:, g * SSD_STATE:(g + 1) * SSD_STATE]
        st = state_sc[g]
        y_off_blocks.append(_dot(cm_g, st.astype(BF16)))
        state_sc[g] = st * elast_x[:, g * gw:(g + 1) * gw] + \
            _dot_tn(bm_g, xw[:, g * gw:(g + 1) * gw])
    y_off = jnp.concatenate(y_off_blocks, axis=1) * ecum_x

    y = y_diag + y_off + dskip_ref[...] * xs
    yg = y * _silu(z_ref[0])
    outs = []
    for g in range(SSD_GROUPS):
        yg_g = yg[:, g * gw:(g + 1) * gw]
        ms = jnp.mean(yg_g * yg_g, axis=-1, keepdims=True)
        outs.append(yg_g * lax.rsqrt(ms + NORM_EPS))
    y_ref[0] = (jnp.concatenate(outs, axis=1) * norm_ref[...]).astype(y_ref.dtype)


def _ssd(xbc, z, dt_nat, dt_t, conv_w, conv_b, dt_bias, a_log, d_skip, ssd_norm, bsz, s):
    nch = s // SSD_CHUNK
    L = SSD_CHUNK
    pad = LANES - SSD_HEADS
    dtb = jnp.pad(dt_bias, (0, pad)).reshape(1, LANES)
    alog = jnp.pad(a_log, (0, pad)).reshape(1, LANES)
    dskip_x = jnp.repeat(d_skip, SSD_HEAD_DIM).reshape(1, SSD_INNER)
    const = lambda shape: pl.BlockSpec(shape, lambda b, c: tuple(0 for _ in shape))
    return pl.pallas_call(
        _ssd_kernel,
        grid=(bsz, nch),
        in_specs=[
            pl.BlockSpec((1, L, SSD_XBC), lambda b, c: (b, c, 0)),
            pl.BlockSpec((1, L, SSD_INNER), lambda b, c: (b, c, 0)),
            pl.BlockSpec((1, L, LANES), lambda b, c: (b, c, 0)),
            pl.BlockSpec((SSD_HEADS, L), lambda b, c: (0, b * nch + c)),
            const((SSD_CONV, SSD_XBC)), const((1, SSD_XBC)),
            const((1, LANES)), const((SSD_HEADS, 1)),
            const((1, LANES)), const((SSD_HEADS, 1)),
            const((1, SSD_INNER)), const((1, SSD_INNER)),
        ],
        out_specs=pl.BlockSpec((1, L, SSD_INNER), lambda b, c: (b, c, 0)),
        out_shape=jax.ShapeDtypeStruct((bsz, s, SSD_INNER), BF16),
        scratch_shapes=[
            pltpu.VMEM((SSD_GROUPS, SSD_STATE, SSD_INNER // SSD_GROUPS), F32),
            pltpu.VMEM((SUBLANES, SSD_XBC), F32),
            pltpu.VMEM((SUBLANES + L, SSD_XBC), F32),
        ],
        compiler_params=_params(("parallel", "arbitrary")),
        name="ssd_scan",
    )(xbc.reshape(bsz, s, SSD_XBC), z.reshape(bsz, s, SSD_INNER),
      dt_nat.reshape(bsz, s, LANES), dt_t,
      conv_w, conv_b.reshape(1, SSD_XBC), dtb, dt_bias.reshape(SSD_HEADS, 1),
      alog, a_log.reshape(SSD_HEADS, 1), dskip_x, ssd_norm.reshape(1, SSD_INNER))


def _conf_kernel(glu_ref, zc_ref, w_ref, b_ref, lng_ref, lnb_ref, y_ref, sh_sc, u_sc):
    ts = glu_ref.shape[1]
    n = sh_sc.shape[1]
    halo = n - ts
    first = halo - (CONF_CONV_WIDTH - 1)

    @pl.when(pl.program_id(1) == 0)
    def _():
        sh_sc[0, 0:halo, :] = jnp.zeros((halo, CONF_WIDTH), F32)

    sh_sc[0, halo:n, :] = glu_ref[0, :, :CONF_WIDTH] * jax.nn.sigmoid(glu_ref[0, :, CONF_WIDTH:])
    tiles = sh_sc[0].reshape(n // SUBLANES, SUBLANES, CONF_WIDTH)
    sub = lax.broadcasted_iota(jnp.int32, (n // SUBLANES - 1, SUBLANES, CONF_WIDTH), 1)
    for s in range(1, SUBLANES):
        rolled = pltpu.roll(tiles, SUBLANES - s, 1)
        shifted = jnp.where(sub < SUBLANES - s, rolled[:-1], rolled[1:])
        sh_sc[s, 0:n - SUBLANES, :] = shifted.reshape(n - SUBLANES, CONF_WIDTH)
    for r0 in range(0, ts, CONF_ROW_CHUNK):
        for c0 in range(0, CONF_WIDTH, CONF_LANE_CHUNK):
            cs = slice(c0, c0 + CONF_LANE_CHUNK)
            acc = None
            for k in range(CONF_CONV_WIDTH):
                a, s = divmod(first + k, SUBLANES)
                o = r0 + a * SUBLANES
                window = sh_sc[s, o:o + CONF_ROW_CHUNK, cs]
                term = w_ref[k][None, :, cs] * window.reshape(CONF_ROW_CHUNK // SUBLANES, SUBLANES, -1)
                acc = term if acc is None else acc + term
            u_sc[r0:r0 + CONF_ROW_CHUNK, cs] = acc.reshape(CONF_ROW_CHUNK, -1)
    sh_sc[0, 0:halo, :] = sh_sc[0, ts:n, :]
    u = u_sc[...] + b_ref[...]
    mu = jnp.mean(u, axis=-1, keepdims=True)
    var = jnp.mean(jnp.square(u - mu), axis=-1, keepdims=True)
    un = (u - mu) * lax.rsqrt(var + NORM_EPS) * lng_ref[...] + lnb_ref[...]
    y_ref[0] = (_silu(un) * _silu(zc_ref[0])).astype(y_ref.dtype)


def _conformer(glu, zc, conv_w, conv_b, ln_g, ln_b, bsz, s):
    ts = min(CONF_ROWS, s)
    halo = 32
    w_tiles = jnp.broadcast_to(conv_w[:, None, :], (CONF_CONV_WIDTH, SUBLANES, CONF_WIDTH))
    const = lambda shape: pl.BlockSpec(shape, lambda b, i: tuple(0 for _ in shape))
    return pl.pallas_call(
        _conf_kernel,
        grid=(bsz, s // ts),
        in_specs=[
            pl.BlockSpec((1, ts, 2 * CONF_WIDTH), lambda b, i: (b, i, 0)),
            pl.BlockSpec((1, ts, CONF_WIDTH), lambda b, i: (b, i, 0)),
            const(w_tiles.shape), const((1, CONF_WIDTH)),
            const((1, CONF_WIDTH)), const((1, CONF_WIDTH)),
        ],
        out_specs=pl.BlockSpec((1, ts, CONF_WIDTH), lambda b, i: (b, i, 0)),
        out_shape=jax.ShapeDtypeStruct((bsz, s, CONF_WIDTH), BF16),
        scratch_shapes=[pltpu.VMEM((SUBLANES, halo + ts, CONF_WIDTH), F32),
                        pltpu.VMEM((ts, CONF_WIDTH), F32)],
        compiler_params=_params(("parallel", "arbitrary")),
        name="conformer",
    )(glu.reshape(bsz, s, 2 * CONF_WIDTH), zc.reshape(bsz, s, CONF_WIDTH), w_tiles,
      conv_b.reshape(1, -1), ln_g.reshape(1, -1), ln_b.reshape(1, -1))


def _outproj_even_kernel(h_ref, a_ref, b_ref, wa_ref, wb_ref, o_ref):
    o_ref[...] = h_ref[...] + (_dot(a_ref[...], wa_ref[...]) + _dot(b_ref[...], wb_ref[...]))


def _outproj_even(h2d, ya, yb, w_out, tm):
    t_tokens, d = h2d.shape
    wa = w_out[:SSD_INNER].astype(BF16)
    wb = w_out[SSD_INNER:].astype(BF16)
    row = lambda w: pl.BlockSpec((tm, w), lambda i: (i, 0))
    return pl.pallas_call(
        _outproj_even_kernel,
        grid=(t_tokens // tm,),
        in_specs=[row(d), row(SSD_INNER), row(CONF_WIDTH),
                  pl.BlockSpec(wa.shape, lambda i: (0, 0)),
                  pl.BlockSpec(wb.shape, lambda i: (0, 0))],
        out_specs=row(d),
        out_shape=jax.ShapeDtypeStruct((t_tokens, d), F32),
        compiler_params=_params(("parallel",)),
        name="outproj_even",
    )(h2d, ya, yb, wa, wb)


def _outproj_odd_kernel(h_ref, ocw_ref, os_ref, zt_ref, w_ref, g_ref, o_ref, *, final):
    o = ocw_ref[...].astype(F32) + os_ref[...].astype(F32)
    y = (o * _silu(zt_ref[...])).astype(BF16)
    out = h_ref[...] + _dot_tn(y, w_ref[...])
    if final:
        ms = jnp.mean(out * out, axis=-1, keepdims=True)
        out = (out * lax.rsqrt(ms + NORM_EPS)) * g_ref[...]
    o_ref[...] = out


def _outproj_odd(h2d, ocw, osel, z_t, w_out, final_gain, final, tm):
    t_tokens, d = h2d.shape
    w = w_out.astype(BF16)
    col = pl.BlockSpec((NSA_WIDTH, tm), lambda i: (0, i))
    return pl.pallas_call(
        functools.partial(_outproj_odd_kernel, final=final),
        grid=(t_tokens // tm,),
        in_specs=[pl.BlockSpec((tm, d), lambda i: (i, 0)), col, col, col,
                  pl.BlockSpec(w.shape, lambda i: (0, 0)),
                  pl.BlockSpec((1, d), lambda i: (0, 0))],
        out_specs=pl.BlockSpec((tm, d), lambda i: (i, 0)),
        out_shape=jax.ShapeDtypeStruct((t_tokens, d), F32),
        compiler_params=_params(("parallel",)),
        name="outproj_odd",
    )(h2d, ocw, osel, z_t, w, final_gain.reshape(1, d))


def _compress_kernel(xk_ref, xv_ref, pek_ref, pev_ref, w1k_ref, w1v_ref, w2k_ref, w2vt_ref,
                     kc_ref, vct_ref):
    half = CMP_STRIDE * NSA_HEAD_DIM

    def hidden(x_ref, pe_ref, w1_ref):
        x = x_ref[0, 0]
        n = x.shape[0]
        lo = (x + pe_ref[0:1, :]).astype(BF16)
        hi = (x + pe_ref[1:2, :]).astype(BF16)
        h = _dot(lo, w1_ref[0:half, :]) + pltpu.roll(_dot(hi, w1_ref[half:2 * half, :]), n - 1, 0)
        return _silu(h).astype(BF16)

    kc_ref[0, 0] = _dot(hidden(xk_ref, pek_ref, w1k_ref), w2k_ref[...]).astype(kc_ref.dtype)
    vct = _dot_nt(w2vt_ref[...], hidden(xv_ref, pev_ref, w1v_ref))
    row = lax.broadcasted_iota(jnp.int32, vct.shape, 0)
    vct_ref[0, 0] = jnp.where(row == NSA_HEAD_DIM, 1.0, vct).astype(vct_ref.dtype)


def _compress(kc, vc, pe_k, w1_k, w2_k, pe_v, w1_v, w2_v, bsz, s):
    g, d = NSA_KV_GROUPS, NSA_HEAD_DIM
    nh = s // CMP_STRIDE
    half = CMP_STRIDE * d

    def regroup(t):
        return t.reshape(g, bsz, nh, half)

    const = lambda shape: pl.BlockSpec(shape, lambda b, gi: tuple(0 for _ in shape))
    blk = pl.BlockSpec((1, 1, nh, half), lambda b, gi: (gi, b, 0, 0))
    return pl.pallas_call(
        _compress_kernel,
        grid=(bsz, g),
        in_specs=[blk, blk, const((2, half)), const((2, half)),
                  const((2 * half, CMP_HIDDEN)), const((2 * half, CMP_HIDDEN)),
                  const((CMP_HIDDEN, d)), const((V_AUG_ROWS, CMP_HIDDEN))],
        out_specs=[pl.BlockSpec((1, 1, nh, d), lambda b, gi: (b, gi, 0, 0)),
                   pl.BlockSpec((1, 1, V_AUG_ROWS, nh), lambda b, gi: (b, gi, 0, 0))],
        out_shape=[jax.ShapeDtypeStruct((bsz, g, nh, d), BF16),
                   jax.ShapeDtypeStruct((bsz, g, V_AUG_ROWS, nh), BF16)],
        compiler_params=_params(("parallel", "parallel")),
        name="compress_kv",
    )(regroup(kc), regroup(vc), pe_k.reshape(2, half), pe_v.reshape(2, half),
      w1_k.astype(BF16), w1_v.astype(BF16), w2_k.astype(BF16),
      jnp.pad(w2_v.T, ((0, V_AUG_ROWS - d), (0, 0))).astype(BF16))


def _selection_bias(imp, t0, topk, v_sc, cnt_sc):
    n_sel, tq = imp.shape
    shift = int(math.log2(SEL_BLOCK))
    ngroups = n_sel // SUBLANES
    jidx = lax.broadcasted_iota(jnp.int32, (n_sel, tq), 0)
    t = t0 + lax.broadcasted_iota(jnp.int32, (n_sel, tq), 1)
    cur = lax.shift_right_logical(t, shift)
    valid = jidx * SEL_BLOCK <= t
    forced = (jidx == 0) | (jidx == cur) | (jidx == cur - 1)
    v_sc[...] = jnp.where(valid, imp + jnp.where(forced, FORCE_BONUS, 0.0), -jnp.inf)
    cnt_sc[...] = jnp.zeros((n_sel, tq), jnp.int32)
    last_block = lax.shift_right_logical(t0 + tq - 1, shift)
    last_group = lax.shift_right_logical(last_block, int(math.log2(SUBLANES)))
    sub = lax.broadcasted_iota(jnp.int32, (SUBLANES, tq), 0)

    def group(ref, gi):
        return ref[gi * SUBLANES:(gi + 1) * SUBLANES, :]

    def rows_of(vals):
        return [jnp.broadcast_to(vals[ii:ii + 1, :], (SUBLANES, tq)) for ii in range(SUBLANES)]

    for hi in range(ngroups):
        @pl.when((hi <= last_group) & (last_block >= topk))
        def _():
            v_hi = group(v_sc, hi)
            rows_hi = rows_of(v_hi)
            own = jnp.zeros((SUBLANES, tq), jnp.int32)
            for ii in range(SUBLANES):
                own = own + jnp.where(sub > ii, jnp.where(rows_hi[ii] >= v_hi, 1, 0),
                                      jnp.where(rows_hi[ii] > v_hi, 1, 0))
            for lo in range(hi):
                v_lo = group(v_sc, lo)
                add_lo = jnp.zeros((SUBLANES, tq), jnp.int32)
                for ii in range(SUBLANES):
                    add_lo = add_lo + jnp.where(rows_hi[ii] > v_lo, 1, 0)
                cnt_sc[lo * SUBLANES:(lo + 1) * SUBLANES, :] += add_lo
                for row in rows_of(v_lo):
                    own = own + jnp.where(row >= v_hi, 1, 0)
            cnt_sc[hi * SUBLANES:(hi + 1) * SUBLANES, :] += own

    return jnp.where(valid & (cnt_sc[...] < topk), 0.0, NEG_BIG)


def _lane_tile(x, reps):
    return jnp.concatenate([x] * reps, axis=1)


def _window_branch(qt_ref, kw_ref, vw_ref, qw_sc, t0, tq):
    d = NSA_HEAD_DIM
    sub = WIN_TQ
    span = WINDOW + sub
    krow = lax.broadcasted_iota(jnp.int32, (span, sub), 0)
    qcol = lax.broadcasted_iota(jnp.int32, (span, sub), 1)
    qw_sc[:, d:, :] = jnp.zeros((qw_sc.shape[0], qw_sc.shape[1] - d, qw_sc.shape[2]), qw_sc.dtype)
    pieces = [[] for _ in range(NSA_HPG)]
    for h in range(tq // sub):
        q0 = t0 + h * sub
        start = pl.multiple_of(jnp.maximum(q0 - WINDOW, 0), sub)
        rel = (qcol - krow) + (q0 - start)
        bias = jnp.where(rel >= 0, jnp.where(rel < WINDOW, 0.0, NEG_BIG), NEG_BIG)
        for r in range(NSA_HPG):
            qw_sc[h, 0:d, r * sub:(r + 1) * sub] = qt_ref[r, :, h * sub:(h + 1) * sub]
        s = _dot(kw_ref[pl.ds(start, span), :], qw_sc[h]) + _lane_tile(bias, NSA_HPG)
        m = jnp.max(s, axis=0, keepdims=True)
        acc = _dot(vw_ref[0, :, pl.ds(start, span)], jnp.exp(s - m).astype(BF16))
        for r in range(NSA_HPG):
            a = acc[:, r * sub:(r + 1) * sub]
            pieces[r].append(a[0:d] * (1.0 / a[d:d + 1]))
    return [jnp.concatenate(p, axis=1) for p in pieces]


def _cmp_win_kernel(*refs, n_cmp, topk):
    (qt_ref, kc_ref, vct_ref, ovt_ref, kw_ref, vw_ref, gl_ref, gb_ref,
     o_ref, selb_ref, qa_sc, qw_sc, v_sc, cnt_sc) = refs
    tq = qt_ref.shape[2]
    d = NSA_HEAD_DIM
    nc = kc_ref.shape[2]
    n_sel = ovt_ref.shape[0]
    t0 = pl.program_id(2) * tq

    gates_w = jax.nn.sigmoid(gl_ref[0, 2] + gb_ref[0, 2])
    o_win = [o * gates_w[r:r + 1, :] for r, o in enumerate(_window_branch(qt_ref, kw_ref, vw_ref, qw_sc, t0, tq))]

    for r in range(NSA_HPG):
        qa_sc[:, r * tq:(r + 1) * tq] = qt_ref[r]

    cidx = lax.broadcasted_iota(jnp.int32, (nc, tq), 0)
    t = t0 + lax.broadcasted_iota(jnp.int32, (nc, tq), 1)
    cbias = jnp.where((cidx * CMP_STRIDE + (CMP_BLOCK - 1) <= t) & (cidx < n_cmp), 0.0, -jnp.inf)
    s = _dot(kc_ref[0, 0], qa_sc[...]) + _lane_tile(cbias, NSA_HPG)
    m = jnp.max(s, axis=0, keepdims=True)
    m = jnp.where(m > -jnp.inf, m, 0.0)
    e = jnp.exp(s - m).astype(BF16)
    both = _dot(jnp.concatenate([vct_ref[0, 0], ovt_ref[...]], axis=0), e)
    rinv = 1.0 / jnp.maximum(both[d:d + 1, :], 1e-30)
    gates_c = jax.nn.sigmoid(gl_ref[0, 0] + gb_ref[0, 0])
    imp = jnp.zeros((n_sel, tq), F32)
    for r in range(NSA_HPG):
        cols = slice(r * tq, (r + 1) * tq)
        o_cmp = both[0:d, cols] * rinv[:, cols] * gates_c[r:r + 1, :]
        o_ref[r] = (o_cmp + o_win[r]).astype(o_ref.dtype)
        imp = imp + both[V_AUG_ROWS:, cols] * rinv[:, cols]
    bias = _selection_bias(imp, t0, topk, v_sc, cnt_sc)
    n_pad = selb_ref.shape[2] - bias.shape[0]
    if n_pad:
        bias = jnp.concatenate([bias, jnp.zeros((n_pad, tq), F32)], axis=0)
    selb_ref[0, 0] = bias.astype(selb_ref.dtype)


def _gate_specs(nq, branch, nbranch, tq):
    return (pl.BlockSpec((1, nbranch, SUBLANES, tq), lambda b, g, i: (g, branch, 0, b * nq + i)),
            pl.BlockSpec((1, nbranch, SUBLANES, 1), lambda b, g, i: (g, branch, 0, 0)))


def _cmp_win_attn(q_t, kcmp, vcmp_t, kw_a, vw_a, gl_t, gbias, bsz, s):
    g, d = NSA_KV_GROUPS, NSA_HEAD_DIM
    tq = ATT_TQ
    nq = s // tq
    nc = s // CMP_STRIDE
    n_cmp = (s - CMP_BLOCK) // CMP_STRIDE + 1
    n_sel = s // SEL_BLOCK
    n_selp = _sel_pad(n_sel)
    topk = min(SEL_TOPK, n_sel)
    cmp_start = np.arange(nc) * CMP_STRIDE
    sel_start = np.arange(n_sel) * SEL_BLOCK
    ov = ((cmp_start[:, None] < sel_start[None, :] + SEL_BLOCK) &
          (cmp_start[:, None] + CMP_BLOCK > sel_start[None, :]) &
          (np.arange(nc)[:, None] < n_cmp))
    ovt = jnp.asarray(ov.T.astype(np.float32), dtype=BF16)
    gl_spec, gb_spec = _gate_specs(nq, 0, 3, tq)
    t_tokens = bsz * s
    head_tile = pl.BlockSpec((NSA_HPG, d, tq), lambda b, gi, i: (gi, 0, b * nq + i))
    return pl.pallas_call(
        functools.partial(_cmp_win_kernel, n_cmp=n_cmp, topk=topk),
        grid=(bsz, g, nq),
        in_specs=[
            head_tile,
            pl.BlockSpec((1, 1, nc, d), lambda b, gi, i: (b, gi, 0, 0)),
            pl.BlockSpec((1, 1, V_AUG_ROWS, nc), lambda b, gi, i: (b, gi, 0, 0)),
            pl.BlockSpec((n_sel, nc), lambda b, gi, i: (0, 0)),
            pl.BlockSpec((s, LANES), lambda b, gi, i: (b, gi)),
            pl.BlockSpec((1, V_AUG_ROWS, s), lambda b, gi, i: (gi, 0, b)),
            gl_spec, gb_spec,
        ],
        out_specs=[head_tile, pl.BlockSpec((1, 1, n_selp, tq), lambda b, gi, i: (b, gi, 0, i))],
        out_shape=[jax.ShapeDtypeStruct((NSA_HEADS, d, t_tokens), BF16),
                   jax.ShapeDtypeStruct((bsz, g, n_selp, s), BF16)],
        scratch_shapes=[pltpu.VMEM((d, NSA_HPG * tq), BF16),
                        pltpu.VMEM((tq // WIN_TQ, LANES, NSA_HPG * WIN_TQ), BF16),
                        pltpu.VMEM((n_sel, tq), F32), pltpu.VMEM((n_sel, tq), jnp.int32)],
        compiler_params=_params(("parallel", "parallel", "parallel")),
        name="cmp_win_attn",
    )(q_t, kcmp, vcmp_t, ovt, kw_a, vw_a, gl_t, gbias)


def _sel_pad(n_sel):
    return -(-n_sel // NSA_HEAD_DIM) * NSA_HEAD_DIM


def _sel_attn_kernel(qt_ref, ka_ref, va_ref, selb_ref, gl_ref, gb_ref, o_ref,
                     qa0_sc, qa1_sc, s0_sc, s1_sc, mx0_sc, mx1_sc, m0_sc, m1_sc, acc0_sc, acc1_sc,
                     *, tq):
    tk = 2 * tq
    d = NSA_HEAD_DIM
    w = NSA_HPG * tq
    diag = pl.program_id(2)
    t0 = diag * tk
    streams = ((qa0_sc, s0_sc, mx0_sc, m0_sc, acc0_sc), (qa1_sc, s1_sc, mx1_sc, m1_sc, acc1_sc))
    for x, (qa_sc, _, _, m_sc, acc_sc) in enumerate(streams):
        for r in range(NSA_HPG):
            qa_sc[0:d, r * tq:(r + 1) * tq] = qt_ref[r, :, x * tq:(x + 1) * tq]
            qa_sc[d:, r * tq:(r + 1) * tq] = selb_ref[0, 0, :, x * tq:(x + 1) * tq]
        m_sc[...] = jnp.full(m_sc.shape, NEG_BIG, F32)
        acc_sc[...] = jnp.zeros(acc_sc.shape, F32)

    def scores(x, j, diagonal):
        qa_sc, s_sc, mx_sc, _, _ = streams[x]
        rows = (x + 1) * tq if diagonal else tk
        k0 = pl.multiple_of(j * tk, tk)
        s = _dot(ka_ref[pl.ds(k0, rows), :], qa_sc[...])
        if diagonal:
            kpos = k0 + lax.broadcasted_iota(jnp.int32, (rows, w), 0)
            t = t0 + x * tq + (lax.broadcasted_iota(jnp.int32, (rows, w), 1) & (tq - 1))
            s = jnp.where(kpos <= t, s, NEG_BIG)
        s_sc[0:rows, :] = s
        mx_sc[...] = jnp.max(s, axis=0, keepdims=True)

    def accumulate(x, j, diagonal=False):
        _, s_sc, mx_sc, m_sc, acc_sc = streams[x]
        rows = (x + 1) * tq if diagonal else tk
        k0 = pl.multiple_of(j * tk, tk)
        m_old = m_sc[...]
        m_new = jnp.maximum(m_old, mx_sc[...])
        p = jnp.exp(s_sc[0:rows, :] - m_new).astype(BF16)
        acc_sc[...] = jnp.exp(m_old - m_new) * acc_sc[...] + _dot(va_ref[0, :, pl.ds(k0, rows)], p)
        m_sc[...] = m_new

    scores(0, diag, True)
    scores(1, diag, True)
    accumulate(0, diag, True)

    def below(k, carry):
        scores(0, k, False)
        accumulate(1, jnp.where(k == 0, diag, k - 1))
        scores(1, k, False)
        accumulate(0, k)
        return carry

    lax.fori_loop(0, diag, below, 0)
    accumulate(1, jnp.where(diag == 0, diag, diag - 1))

    gates = jax.nn.sigmoid(gl_ref[0, 0] + gb_ref[0, 0])
    for x, (_, _, _, _, acc_sc) in enumerate(streams):
        for r in range(NSA_HPG):
            a = acc_sc[:, r * tq:(r + 1) * tq]
            cols = slice(x * tq, (x + 1) * tq)
            o_ref[r, :, cols] = (a[0:d] * (1.0 / a[d:d + 1]) * gates[r:r + 1, cols]).astype(o_ref.dtype)


def _sel_attn(q_t, ka, va, selb, gl_t, gbias, bsz, s):
    g, d = NSA_KV_GROUPS, NSA_HEAD_DIM
    tq = ATT_TQ
    tk = 2 * tq
    nq = s // tk
    w = NSA_HPG * tq
    n_selp = selb.shape[2]
    kd = d + n_selp
    gl_spec, gb_spec = _gate_specs(nq, 1, 1, tk)
    return pl.pallas_call(
        functools.partial(_sel_attn_kernel, tq=tq),
        grid=(bsz, g, nq),
        in_specs=[
            pl.BlockSpec((NSA_HPG, d, tk), lambda b, gi, i: (gi, 0, b * nq + i)),
            pl.BlockSpec((s, kd), lambda b, gi, i: (b, gi)),
            pl.BlockSpec((1, V_AUG_ROWS, s), lambda b, gi, i: (gi, 0, b)),
            pl.BlockSpec((1, 1, n_selp, tk), lambda b, gi, i: (b, gi, 0, i)),
            gl_spec, gb_spec,
        ],
        out_specs=pl.BlockSpec((NSA_HPG, d, tk), lambda b, gi, i: (gi, 0, b * nq + i)),
        out_shape=jax.ShapeDtypeStruct((NSA_HEADS, d, bsz * s), BF16),
        scratch_shapes=[pltpu.VMEM((kd, w), BF16), pltpu.VMEM((kd, w), BF16),
                        pltpu.VMEM((tk, w), F32), pltpu.VMEM((tk, w), F32),
                        pltpu.VMEM((1, w), F32), pltpu.VMEM((1, w), F32),
                        pltpu.VMEM((1, w), F32), pltpu.VMEM((1, w), F32),
                        pltpu.VMEM((V_AUG_ROWS, w), F32), pltpu.VMEM((V_AUG_ROWS, w), F32)],
        compiler_params=_params(("parallel", "parallel", "arbitrary")),
        name="sel_attn",
    )(q_t, ka, va, selb, gl_t, gbias)


def _even_layer(h2d, bsz, s, gain, w_in, ssd_conv_w, ssd_conv_b, dt_bias, a_log, d_skip, ssd_norm,
                conf_conv_w, conf_conv_b, conf_ln_g, conf_ln_b, w_out):
    o1 = SSD_INNER
    o2 = o1 + SSD_XBC
    o3 = o2 + SSD_HEADS
    o4 = o3 + 2 * CONF_WIDTH
    w_dt = w_in[:, o2:o3]
    w_nat = jnp.concatenate(
        [w_in[:, :o2], w_in[:, o3:], jnp.pad(w_dt, ((0, 0), (0, LANES - SSD_HEADS)))], axis=1).astype(BF16)
    nat_spec = [(SSD_INNER, F32), (SSD_XBC, F32), (2 * CONF_WIDTH, F32), (CONF_WIDTH, F32), (LANES, F32)]
    tm = min(PROJ_ROWS, s)
    z, xbc, glu, zc, dt_nat, dt_t = _norm_proj(
        h2d, gain, w_nat, nat_spec, w_dt.T.astype(BF16), jnp.zeros((SSD_HEADS,), F32),
        [(SSD_HEADS, F32)], tm)
    y_a = _ssd(xbc, z, dt_nat, dt_t, ssd_conv_w, ssd_conv_b, dt_bias, a_log, d_skip, ssd_norm, bsz, s)
    y_b = _conformer(glu, zc, conf_conv_w, conf_conv_b, conf_ln_g, conf_ln_b, bsz, s)
    return _outproj_even(h2d, y_a.reshape(bsz * s, SSD_INNER), y_b.reshape(bsz * s, CONF_WIDTH), w_out, tm)


def _odd_layer(h2d, bsz, s, gain, w_in, gate_bias, pe_k, w1_k, w2_k, pe_v, w1_v, w2_v, w_out,
               final_gain, final):
    g, r, d = NSA_KV_GROUPS, NSA_HPG, NSA_HEAD_DIM
    t_tokens = bsz * s
    sizes = [NSA_WIDTH] + [KV_WIDTH] * 6 + [3 * NSA_HEADS, NSA_WIDTH]
    offs = np.cumsum([0] + sizes)
    col = lambda k: w_in[:, offs[k]:offs[k + 1]]
    w_q, w_kc, w_vc, w_ks, w_vs, w_kw, w_vw, w_gl, w_z = [col(k) for k in range(9)]
    perm = np.array([[[(gi * r + ri) * 3 + br for ri in range(r)] for br in range(3)] for gi in range(g)])
    w_gl_g = jnp.pad(w_gl[:, perm.reshape(-1)].reshape(D_MODEL, g * 3, r),
                     ((0, 0), (0, 0), (0, SUBLANES - r))).reshape(D_MODEL, g * 3 * SUBLANES)
    gbias = jnp.pad(gate_bias[perm.reshape(-1)].reshape(g * 3, r), ((0, 0), (0, SUBLANES - r)))
    gbias = gbias.reshape(g, 3, SUBLANES, 1)
    scale = NSA_HEAD_DIM ** -0.5
    assert d + _sel_pad(s // SEL_BLOCK) == LANES

    def key_slots(w):
        return jnp.pad(w.reshape(D_MODEL, g, d), ((0, 0), (0, 0), (0, LANES - d))).reshape(D_MODEL, g * LANES)

    def value_rows(w):
        return jnp.pad(w.reshape(D_MODEL, g, d), ((0, 0), (0, 0), (0, V_AUG_ROWS - d))).reshape(D_MODEL, g * V_AUG_ROWS)

    ones_row = jnp.zeros((g, V_AUG_ROWS), F32).at[:, d].set(1.0).reshape(-1)
    w_nat = jnp.concatenate([w_kc, w_vc, key_slots(w_ks), key_slots(w_kw)], axis=1).astype(BF16)
    w_t = jnp.concatenate([w_q * scale, value_rows(w_vs), value_rows(w_vw), w_gl_g, w_z], axis=1).T.astype(BF16)
    t_bias = jnp.concatenate([jnp.zeros((NSA_WIDTH,), F32), ones_row, ones_row,
                              jnp.zeros((g * 3 * SUBLANES + NSA_WIDTH,), F32)])
    nat_spec = [(KV_WIDTH, F32), (KV_WIDTH, F32), (g * LANES, BF16), (g * LANES, BF16)]
    t_spec = [(NSA_WIDTH, BF16), (g * V_AUG_ROWS, BF16), (g * V_AUG_ROWS, BF16),
              (g * 3 * SUBLANES, F32), (NSA_WIDTH, F32)]
    tm = min(PROJ_ROWS, s)
    kc, vc, ks_a, kw_a, q_t, vs_a, vw_a, gl_t, z_t = _norm_proj(
        h2d, gain, w_nat, nat_spec, w_t, t_bias, t_spec, tm, onehot_out=2, regroup_outs=(0, 1), seq_len=s)

    q_t = q_t.reshape(NSA_HEADS, d, t_tokens)
    gl_t = gl_t.reshape(g, 3, SUBLANES, t_tokens)
    vs_a = vs_a.reshape(g, V_AUG_ROWS, t_tokens)
    vw_a = vw_a.reshape(g, V_AUG_ROWS, t_tokens)

    kcmp, vcmp_t = _compress(kc, vc, pe_k, w1_k, w2_k, pe_v, w1_v, w2_v, bsz, s)
    o_cw, selb = _cmp_win_attn(q_t, kcmp, vcmp_t, kw_a, vw_a, gl_t, gbias, bsz, s)
    o_sel = _sel_attn(q_t, ks_a, vs_a, selb, gl_t, gbias, bsz, s)
    flat = lambda o: o.reshape(NSA_WIDTH, t_tokens)
    return _outproj_odd(h2d, flat(o_cw), flat(o_sel), z_t, w_out, final_gain, final, tm)


def kernel(x, e_norm, e_w_in, e_ssd_conv_w, e_ssd_conv_b, e_dt_bias, e_a_log, e_d_skip, e_ssd_norm,
           e_conf_conv_w, e_conf_conv_b, e_conf_ln_g, e_conf_ln_b, e_w_out, o_norm, o_w_in, o_gate_bias,
           o_cmp_pe_k, o_cmp_w1_k, o_cmp_w2_k, o_cmp_pe_v, o_cmp_w1_v, o_cmp_w2_v, o_w_out, final_norm):
    bsz, s, d = x.shape
    depth = e_norm.shape[0] + o_norm.shape[0]
    h = x.reshape(bsz * s, d)
    for layer in range(depth):
        i = layer // 2
        if layer % 2 == 0:
            h = _even_layer(h, bsz, s, e_norm[i], e_w_in[i], e_ssd_conv_w[i], e_ssd_conv_b[i], e_dt_bias[i],
                            e_a_log[i], e_d_skip[i], e_ssd_norm[i], e_conf_conv_w[i], e_conf_conv_b[i],
                            e_conf_ln_g[i], e_conf_ln_b[i], e_w_out[i])
        else:
            h = _odd_layer(h, bsz, s, o_norm[i], o_w_in[i], o_gate_bias[i], o_cmp_pe_k[i], o_cmp_w1_k[i],
                           o_cmp_w2_k[i], o_cmp_pe_v[i], o_cmp_w1_v[i], o_cmp_w2_v[i], o_w_out[i],
                           final_norm, layer == depth - 1)
    return h.reshape(bsz, s, d)
```

```python
import functools
import math

import numpy as np
import jax
import jax.numpy as jnp
from jax import lax
from jax.experimental import pallas as pl
from jax.experimental.pallas import tpu as pltpu

F32 = jnp.float32
BF16 = jnp.bfloat16

D_MODEL = 1024
SSD_HEADS = 16
SSD_HEAD_DIM = 64
SSD_INNER = 1024
SSD_GROUPS = 2
SSD_STATE = 128
SSD_CONV = 4
SSD_CHUNK = 128
SSD_XBC = SSD_INNER + 2 * SSD_GROUPS * SSD_STATE
CONF_WIDTH = 1024
CONF_CONV_WIDTH = 31
NSA_HEADS = 16
NSA_KV_GROUPS = 4
NSA_HPG = 4
NSA_HEAD_DIM = 64
NSA_WIDTH = 1024
KV_WIDTH = 256
CMP_BLOCK = 32
CMP_STRIDE = 16
CMP_HIDDEN = 256
SEL_BLOCK = 64
SEL_TOPK = 16
WINDOW = 512
FORCE_BONUS = 1e6
NORM_EPS = 1e-6

LANES = 128
SUBLANES = 8
VMEM_LIMIT_BYTES = 56 * 1024 * 1024

PROJ_ROWS = 512
CONF_ROWS = 256
CONF_ROW_CHUNK = 32
CONF_LANE_CHUNK = 512
ATT_TQ = 256
WIN_TQ = 128
NEG_BIG = -1e30
V_AUG_ROWS = 80


def _dot(a, b, precision=None):
    return jnp.dot(a, b, preferred_element_type=F32, precision=precision)


def _dot_nt(a, b):
    return lax.dot_general(a, b, (((1,), (1,)), ((), ())), preferred_element_type=F32)


def _dot_tn(a, b):
    return lax.dot_general(a, b, (((0,), (0,)), ((), ())), preferred_element_type=F32)


def _silu(x):
    return x * jax.nn.sigmoid(x)


def _softplus(x):
    return jnp.maximum(x, 0.0) + jnp.log1p(jnp.exp(-jnp.abs(x)))


def _params(semantics):
    return pltpu.CompilerParams(dimension_semantics=semantics,
                                vmem_limit_bytes=VMEM_LIMIT_BYTES)


def _proj_kernel(*refs, nat_cols, t_rows, onehot_out, regroup_outs, seq_len):
    x_ref, g_ref = refs[0], refs[1]
    pos = 2
    w_ref = wt_ref = tb_ref = None
    if nat_cols:
        w_ref = refs[pos]
        pos += 1
    if t_rows:
        wt_ref, tb_ref = refs[pos], refs[pos + 1]
        pos += 2
    n_out = len(nat_cols) + len(t_rows)
    out_refs = refs[pos:pos + n_out]
    scratch = refs[pos + n_out:]
    x = x_ref[...]
    tm = x.shape[0]
    ms = jnp.mean(x * x, axis=-1, keepdims=True)
    xn = ((x * lax.rsqrt(ms + NORM_EPS)) * g_ref[...]).astype(BF16)
    k = 0
    for (a, b) in nat_cols:
        res = _dot(xn, w_ref[:, a:b])
        if k in regroup_outs:
            d = NSA_HEAD_DIM
            res_sc = scratch[0]
            per = LANES // d
            for c in range(res_sc.shape[0]):
                res_sc[c] = res[:, c * LANES:(c + 1) * LANES]
            for l in range(CMP_STRIDE):
                for c in range(res_sc.shape[0]):
                    rows = res_sc[c, pl.ds(l, tm // CMP_STRIDE, stride=CMP_STRIDE), :]
                    for j in range(per):
                        out_refs[k][c * per + j, :, l * d:(l + 1) * d] = rows[:, j * d:(j + 1) * d]
            k += 1
            continue
        if k == onehot_out:
            row = pl.program_id(0) * tm + lax.broadcasted_iota(jnp.int32, (tm, b - a), 0)
            blk = lax.shift_right_logical(lax.rem(row, seq_len), int(math.log2(SEL_BLOCK)))
            slot = lax.broadcasted_iota(jnp.int32, (tm, b - a), 1) & (LANES - 1)
            res = jnp.where(slot - NSA_HEAD_DIM == blk, 1.0, res)
        out_refs[k][...] = res.astype(out_refs[k].dtype)
        k += 1
    for (a, b) in t_rows:
        out_refs[k][...] = (_dot_nt(wt_ref[a:b, :], xn) + tb_ref[a:b, :]).astype(out_refs[k].dtype)
        k += 1


def _norm_proj(h2d, gain, w_nat, nat_spec, w_t, t_bias, t_spec, tm, onehot_out=-1, regroup_outs=(),
               seq_len=1):
    t_tokens, d = h2d.shape
    nat_cols, off = [], 0
    for width, _ in nat_spec:
        nat_cols.append((off, off + width))
        off += width
    t_rows, off = [], 0
    for rows, _ in t_spec:
        t_rows.append((off, off + rows))
        off += rows
    in_specs = [pl.BlockSpec((tm, d), lambda i: (i, 0)),
                pl.BlockSpec((1, d), lambda i: (0, 0))]
    args = [h2d, gain.reshape(1, d)]
    if nat_spec:
        in_specs.append(pl.BlockSpec(w_nat.shape, lambda i: (0, 0)))
        args.append(w_nat)
    if t_spec:
        in_specs.append(pl.BlockSpec(w_t.shape, lambda i: (0, 0)))
        in_specs.append(pl.BlockSpec((w_t.shape[0], 1), lambda i: (0, 0)))
        args += [w_t, t_bias.reshape(-1, 1).astype(F32)]
    out_shape, out_specs, scratch = [], [], []
    for k, (width, dt) in enumerate(nat_spec):
        if k in regroup_outs:
            slab = (NSA_KV_GROUPS, t_tokens // CMP_STRIDE, CMP_STRIDE * NSA_HEAD_DIM)
            out_shape.append(jax.ShapeDtypeStruct(slab, dt))
            out_specs.append(pl.BlockSpec((slab[0], tm // CMP_STRIDE, slab[2]), lambda i: (0, i, 0)))
            scratch = [pltpu.VMEM((width // LANES, tm, LANES), F32)]
        else:
            out_shape.append(jax.ShapeDtypeStruct((t_tokens, width), dt))
            out_specs.append(pl.BlockSpec((tm, width), lambda i: (i, 0)))
    for rows, dt in t_spec:
        out_shape.append(jax.ShapeDtypeStruct((rows, t_tokens), dt))
        out_specs.append(pl.BlockSpec((rows, tm), lambda i: (0, i)))
    return pl.pallas_call(
        functools.partial(_proj_kernel, nat_cols=tuple(nat_cols), t_rows=tuple(t_rows),
                          onehot_out=onehot_out, regroup_outs=tuple(regroup_outs), seq_len=seq_len),
        grid=(t_tokens // tm,),
        in_specs=in_specs,
        out_specs=out_specs,
        out_shape=out_shape,
        scratch_shapes=scratch,
        compiler_params=_params(("parallel",)),
        name="norm_proj",
    )(*args)


def _ssd_kernel(xbc_ref, z_ref, dt_ref, dtT_ref, cw_ref, cb_ref, dtb_ref, dtbT_ref,
                alog_ref, alogT_ref, dskip_ref, norm_ref, y_ref,
                state_sc, carry_sc, win_sc):
    L = SSD_CHUNK
    hp = LANES // SSD_HEAD_DIM
    gw = SSD_INNER // SSD_GROUPS
    heads_per_group = SSD_HEADS // SSD_GROUPS

    @pl.when(pl.program_id(1) == 0)
    def _():
        state_sc[...] = jnp.zeros_like(state_sc)
        carry_sc[...] = jnp.zeros_like(carry_sc)

    xraw = xbc_ref[0]
    win_sc[0:SUBLANES, :] = carry_sc[...]
    win_sc[SUBLANES:SUBLANES + L, :] = xraw
    carry_sc[...] = xraw[L - SUBLANES:L, :]
    conv = cw_ref[0:1, :] * win_sc[SUBLANES - 3:SUBLANES - 3 + L, :]
    for k in range(1, SSD_CONV):
        o = SUBLANES - (SSD_CONV - 1) + k
        conv = conv + cw_ref[k:k + 1, :] * win_sc[o:o + L, :]
    xact = _silu(conv + cb_ref[...])
    xs = xact[:, :SSD_INNER]
    bm = xact[:, SSD_INNER:SSD_INNER + SSD_GROUPS * SSD_STATE].astype(BF16)
    cm = xact[:, SSD_INNER + SSD_GROUPS * SSD_STATE:].astype(BF16)

    dt = _softplus(dt_ref[0] + dtb_ref[...])
    d_a = dt * (-jnp.exp(alog_ref[...]))
    rows = lax.broadcasted_iota(jnp.int32, (L, L), 0)
    cols = lax.broadcasted_iota(jnp.int32, (L, L), 1)
    causal = rows >= cols
    cum = _dot(causal.astype(F32), d_a, precision=lax.Precision.HIGHEST)
    dt_t = _softplus(dtT_ref[...] + dtbT_ref[...])
    d_a_t = dt_t * (-jnp.exp(alogT_ref[...]))
    cum_t = _dot(d_a_t, (rows <= cols).astype(F32), precision=lax.Precision.HIGHEST)
    ecum = jnp.exp(cum)
    cum_last = cum[L - 1:L, :]
    to_end = jnp.exp(cum_last - cum) * dt
    e_last = jnp.exp(cum_last)

    lane = lax.broadcasted_iota(jnp.int32, (L, LANES), 1)
    first_head = lane < SSD_HEAD_DIM
    lane1 = lax.broadcasted_iota(jnp.int32, (1, LANES), 1)
    first_head1 = lane1 < SSD_HEAD_DIM

    def per_head(arr, h0, mask):
        return jnp.where(mask, arr[:, h0:h0 + 1], arr[:, h0 + 1:h0 + 2])

    y_blocks, ecum_blocks, toend_blocks, elast_blocks = [], [], [], []
    for g in range(SSD_GROUPS):
        bm_g = bm[:, g * SSD_STATE:(g + 1) * SSD_STATE]
        cm_g = cm[:, g * SSD_STATE:(g + 1) * SSD_STATE]
        cb = _dot_nt(cm_g, bm_g)
        for j in range(heads_per_group // hp):
            h0 = g * heads_per_group + j * hp
            c0 = h0 * SSD_HEAD_DIM
            x_pair = xs[:, c0:c0 + LANES].astype(BF16)
            res = []
            for hh in range(hp):
                h = h0 + hh
                seg = cum[:, h:h + 1] - cum_t[h:h + 1, :]
                decay = jnp.exp(jnp.where(causal, seg, -jnp.inf))
                wts = (cb * decay) * dt_t[h:h + 1, :]
                res.append(_dot(wts.astype(BF16), x_pair))
            y_blocks.append(jnp.where(first_head, res[0], res[1]))
            ecum_blocks.append(per_head(ecum, h0, first_head))
            toend_blocks.append(per_head(to_end, h0, first_head))
            elast_blocks.append(per_head(e_last, h0, first_head1))
    y_diag = jnp.concatenate(y_blocks, axis=1)
    ecum_x = jnp.concatenate(ecum_blocks, axis=1)
    toend_x = jnp.concatenate(toend_blocks, axis=1)
    elast_x = jnp.concatenate(elast_blocks, axis=1)

    xw = (xs * toend_x).astype(BF16)
    y_off_blocks = []
    for g in range(SSD_GROUPS):
        bm_g = bm[:, g * SSD_STATE:(g + 1) * SSD_STATE]
        cm_g = cm[:, g * SSD_STATE:(g + 1) * SSD_STATE]
        st = state_sc[g]
        y_off_blocks.append(_dot(cm_g, st.astype(BF16)))
        state_sc[g] = st * elast_x[:, g * gw:(g + 1) * gw] + \
            _dot_tn(bm_g, xw[:, g * gw:(g + 1) * gw])
    y_off = jnp.concatenate(y_off_blocks, axis=1) * ecum_x

    y = y_diag + y_off + dskip_ref[...] * xs
    yg = y * _silu(z_ref[0])
    outs = []
    for g in range(SSD_GROUPS):
        yg_g = yg[:, g * gw:(g + 1) * gw]
        ms = jnp.mean(yg_g * yg_g, axis=-1, keepdims=True)
        outs.append(yg_g * lax.rsqrt(ms + NORM_EPS))
    y_ref[0] = (jnp.concatenate(outs, axis=1) * norm_ref[...]).astype(y_ref.dtype)


def _ssd(xbc, z, dt_nat, dt_t, conv_w, conv_b, dt_bias, a_log, d_skip, ssd_norm, bsz, s):
    nch = s // SSD_CHUNK
    L = SSD_CHUNK
    pad = LANES - SSD_HEADS
    dtb = jnp.pad(dt_bias, (0, pad)).reshape(1, LANES)
    alog = jnp.pad(a_log, (0, pad)).reshape(1, LANES)
    dskip_x = jnp.repeat(d_skip, SSD_HEAD_DIM).reshape(1, SSD_INNER)
    const = lambda shape: pl.BlockSpec(shape, lambda b, c: tuple(0 for _ in shape))
    return pl.pallas_call(
        _ssd_kernel,
        grid=(bsz, nch),
        in_specs=[
            pl.BlockSpec((1, L, SSD_XBC), lambda b, c: (b, c, 0)),
            pl.BlockSpec((1, L, SSD_INNER), lambda b, c: (b, c, 0)),
            pl.BlockSpec((1, L, LANES), lambda b, c: (b, c, 0)),
            pl.BlockSpec((SSD_HEADS, L), lambda b, c: (0, b * nch + c)),
            const((SSD_CONV, SSD_XBC)), const((1, SSD_XBC)),
            const((1, LANES)), const((SSD_HEADS, 1)),
            const((1, LANES)), const((SSD_HEADS, 1)),
            const((1, SSD_INNER)), const((1, SSD_INNER)),
        ],
        out_specs=pl.BlockSpec((1, L, SSD_INNER), lambda b, c: (b, c, 0)),
        out_shape=jax.ShapeDtypeStruct((bsz, s, SSD_INNER), BF16),
        scratch_shapes=[
            pltpu.VMEM((SSD_GROUPS, SSD_STATE, SSD_INNER // SSD_GROUPS), F32),
            pltpu.VMEM((SUBLANES, SSD_XBC), F32),
            pltpu.VMEM((SUBLANES + L, SSD_XBC), F32),
        ],
        compiler_params=_params(("parallel", "arbitrary")),
        name="ssd_scan",
    )(xbc.reshape(bsz, s, SSD_XBC), z.reshape(bsz, s, SSD_INNER),
      dt_nat.reshape(bsz, s, LANES), dt_t,
      conv_w, conv_b.reshape(1, SSD_XBC), dtb, dt_bias.reshape(SSD_HEADS, 1),
      alog, a_log.reshape(SSD_HEADS, 1), dskip_x, ssd_norm.reshape(1, SSD_INNER))


def _conf_kernel(glu_ref, zc_ref, w_ref, b_ref, lng_ref, lnb_ref, y_ref, sh_sc, u_sc):
    ts = glu_ref.shape[1]
    n = sh_sc.shape[1]
    halo = n - ts
    first = halo - (CONF_CONV_WIDTH - 1)

    @pl.when(pl.program_id(1) == 0)
    def _():
        sh_sc[0, 0:halo, :] = jnp.zeros((halo, CONF_WIDTH), F32)

    sh_sc[0, halo:n, :] = glu_ref[0, :, :CONF_WIDTH] * jax.nn.sigmoid(glu_ref[0, :, CONF_WIDTH:])
    tiles = sh_sc[0].reshape(n // SUBLANES, SUBLANES, CONF_WIDTH)
    sub = lax.broadcasted_iota(jnp.int32, (n // SUBLANES - 1, SUBLANES, CONF_WIDTH), 1)
    for s in range(1, SUBLANES):
        rolled = pltpu.roll(tiles, SUBLANES - s, 1)
        shifted = jnp.where(sub < SUBLANES - s, rolled[:-1], rolled[1:])
        sh_sc[s, 0:n - SUBLANES, :] = shifted.reshape(n - SUBLANES, CONF_WIDTH)
    for r0 in range(0, ts, CONF_ROW_CHUNK):
        for c0 in range(0, CONF_WIDTH, CONF_LANE_CHUNK):
            cs = slice(c0, c0 + CONF_LANE_CHUNK)
            acc = None
            for k in range(CONF_CONV_WIDTH):
                a, s = divmod(first + k, SUBLANES)
                o = r0 + a * SUBLANES
                window = sh_sc[s, o:o + CONF_ROW_CHUNK, cs]
                term = w_ref[k][None, :, cs] * window.reshape(CONF_ROW_CHUNK // SUBLANES, SUBLANES, -1)
                acc = term if acc is None else acc + term
            u_sc[r0:r0 + CONF_ROW_CHUNK, cs] = acc.reshape(CONF_ROW_CHUNK, -1)
    sh_sc[0, 0:halo, :] = sh_sc[0, ts:n, :]
    u = u_sc[...] + b_ref[...]
    mu = jnp.mean(u, axis=-1, keepdims=True)
    var = jnp.mean(jnp.square(u - mu), axis=-1, keepdims=True)
    un = (u - mu) * lax.rsqrt(var + NORM_EPS) * lng_ref[...] + lnb_ref[...]
    y_ref[0] = (_silu(un) * _silu(zc_ref[0])).astype(y_ref.dtype)


def _conformer(glu, zc, conv_w, conv_b, ln_g, ln_b, bsz, s):
    ts = min(CONF_ROWS, s)
    halo = 32
    w_tiles = jnp.broadcast_to(conv_w[:, None, :], (CONF_CONV_WIDTH, SUBLANES, CONF_WIDTH))
    const = lambda shape: pl.BlockSpec(shape, lambda b, i: tuple(0 for _ in shape))
    return pl.pallas_call(
        _conf_kernel,
        grid=(bsz, s // ts),
        in_specs=[
            pl.BlockSpec((1, ts, 2 * CONF_WIDTH), lambda b, i: (b, i, 0)),
            pl.BlockSpec((1, ts, CONF_WIDTH), lambda b, i: (b, i, 0)),
            const(w_tiles.shape), const((1, CONF_WIDTH)),
            const((1, CONF_WIDTH)), const((1, CONF_WIDTH)),
        ],
        out_specs=pl.BlockSpec((1, ts, CONF_WIDTH), lambda b, i: (b, i, 0)),
        out_shape=jax.ShapeDtypeStruct((bsz, s, CONF_WIDTH), BF16),
        scratch_shapes=[pltpu.VMEM((SUBLANES, halo + ts, CONF_WIDTH), F32),
                        pltpu.VMEM((ts, CONF_WIDTH), F32)],
        compiler_params=_params(("parallel", "arbitrary")),
        name="conformer",
    )(glu.reshape(bsz, s, 2 * CONF_WIDTH), zc.reshape(bsz, s, CONF_WIDTH), w_tiles,
      conv_b.reshape(1, -1), ln_g.reshape(1, -1), ln_b.reshape(1, -1))


def _outproj_even_kernel(h_ref, a_ref, b_ref, wa_ref, wb_ref, o_ref):
    o_ref[...] = h_ref[...] + (_dot(a_ref[...], wa_ref[...]) + _dot(b_ref[...], wb_ref[...]))


def _outproj_even(h2d, ya, yb, w_out, tm):
    t_tokens, d = h2d.shape
    wa = w_out[:SSD_INNER].astype(BF16)
    wb = w_out[SSD_INNER:].astype(BF16)
    row = lambda w: pl.BlockSpec((tm, w), lambda i: (i, 0))
    return pl.pallas_call(
        _outproj_even_kernel,
        grid=(t_tokens // tm,),
        in_specs=[row(d), row(SSD_INNER), row(CONF_WIDTH),
                  pl.BlockSpec(wa.shape, lambda i: (0, 0)),
                  pl.BlockSpec(wb.shape, lambda i: (0, 0))],
        out_specs=row(d),
        out_shape=jax.ShapeDtypeStruct((t_tokens, d), F32),
        compiler_params=_params(("parallel",)),
        name="outproj_even",
    )(h2d, ya, yb, wa, wb)


def _outproj_odd_kernel(h_ref, ocw_ref, os_ref, zt_ref, w_ref, g_ref, o_ref, *, final):
    o = ocw_ref[...].astype(F32) + os_ref[...].astype(F32)
    y = (o * _silu(zt_ref[...])).astype(BF16)
    out = h_ref[...] + _dot_tn(y, w_ref[...])
    if final:
        ms = jnp.mean(out * out, axis=-1, keepdims=True)
        out = (out * lax.rsqrt(ms + NORM_EPS)) * g_ref[...]
    o_ref[...] = out


def _outproj_odd(h2d, ocw, osel, z_t, w_out, final_gain, final, tm):
    t_tokens, d = h2d.shape
    w = w_out.astype(BF16)
    col = pl.BlockSpec((NSA_WIDTH, tm), lambda i: (0, i))
    return pl.pallas_call(
        functools.partial(_outproj_odd_kernel, final=final),
        grid=(t_tokens // tm,),
        in_specs=[pl.BlockSpec((tm, d), lambda i: (i, 0)), col, col, col,
                  pl.BlockSpec(w.shape, lambda i: (0, 0)),
                  pl.BlockSpec((1, d), lambda i: (0, 0))],
        out_specs=pl.BlockSpec((tm, d), lambda i: (i, 0)),
        out_shape=jax.ShapeDtypeStruct((t_tokens, d), F32),
        compiler_params=_params(("parallel",)),
        name="outproj_odd",
    )(h2d, ocw, osel, z_t, w, final_gain.reshape(1, d))


def _compress_kernel(xk_ref, xv_ref, pek_ref, pev_ref, w1k_ref, w1v_ref, w2k_ref, w2vt_ref,
                     kc_ref, vct_ref):
    half = CMP_STRIDE * NSA_HEAD_DIM

    def hidden(x_ref, pe_ref, w1_ref):
        x = x_ref[0, 0]
        n = x.shape[0]
        lo = (x + pe_ref[0:1, :]).astype(BF16)
        hi = (x + pe_ref[1:2, :]).astype(BF16)
        h = _dot(lo, w1_ref[0:half, :]) + pltpu.roll(_dot(hi, w1_ref[half:2 * half, :]), n - 1, 0)
        return _silu(h).astype(BF16)

    kc_ref[0, 0] = _dot(hidden(xk_ref, pek_ref, w1k_ref), w2k_ref[...]).astype(kc_ref.dtype)
    vct = _dot_nt(w2vt_ref[...], hidden(xv_ref, pev_ref, w1v_ref))
    row = lax.broadcasted_iota(jnp.int32, vct.shape, 0)
    vct_ref[0, 0] = jnp.where(row == NSA_HEAD_DIM, 1.0, vct).astype(vct_ref.dtype)


def _compress(kc, vc, pe_k, w1_k, w2_k, pe_v, w1_v, w2_v, bsz, s):
    g, d = NSA_KV_GROUPS, NSA_HEAD_DIM
    nh = s // CMP_STRIDE
    half = CMP_STRIDE * d

    def regroup(t):
        return t.reshape(g, bsz, nh, half)

    const = lambda shape: pl.BlockSpec(shape, lambda b, gi: tuple(0 for _ in shape))
    blk = pl.BlockSpec((1, 1, nh, half), lambda b, gi: (gi, b, 0, 0))
    return pl.pallas_call(
        _compress_kernel,
        grid=(bsz, g),
        in_specs=[blk, blk, const((2, half)), const((2, half)),
                  const((2 * half, CMP_HIDDEN)), const((2 * half, CMP_HIDDEN)),
                  const((CMP_HIDDEN, d)), const((V_AUG_ROWS, CMP_HIDDEN))],
        out_specs=[pl.BlockSpec((1, 1, nh, d), lambda b, gi: (b, gi, 0, 0)),
                   pl.BlockSpec((1, 1, V_AUG_ROWS, nh), lambda b, gi: (b, gi, 0, 0))],
        out_shape=[jax.ShapeDtypeStruct((bsz, g, nh, d), BF16),
                   jax.ShapeDtypeStruct((bsz, g, V_AUG_ROWS, nh), BF16)],
        compiler_params=_params(("parallel", "parallel")),
        name="compress_kv",
    )(regroup(kc), regroup(vc), pe_k.reshape(2, half), pe_v.reshape(2, half),
      w1_k.astype(BF16), w1_v.astype(BF16), w2_k.astype(BF16),
      jnp.pad(w2_v.T, ((0, V_AUG_ROWS - d), (0, 0))).astype(BF16))


def _selection_bias(imp, t0, topk, v_sc, cnt_sc):
    n_sel, tq = imp.shape
    shift = int(math.log2(SEL_BLOCK))
    ngroups = n_sel // SUBLANES
    jidx = lax.broadcasted_iota(jnp.int32, (n_sel, tq), 0)
    t = t0 + lax.broadcasted_iota(jnp.int32, (n_sel, tq), 1)
    cur = lax.shift_right_logical(t, shift)
    valid = jidx * SEL_BLOCK <= t
    forced = (jidx == 0) | (jidx == cur) | (jidx == cur - 1)
    v_sc[...] = jnp.where(valid, imp + jnp.where(forced, FORCE_BONUS, 0.0), -jnp.inf)
    cnt_sc[...] = jnp.zeros((n_sel, tq), jnp.int32)
    last_block = lax.shift_right_logical(t0 + tq - 1, shift)
    last_group = lax.shift_right_logical(last_block, int(math.log2(SUBLANES)))
    sub = lax.broadcasted_iota(jnp.int32, (SUBLANES, tq), 0)

    def group(ref, gi):
        return ref[gi * SUBLANES:(gi + 1) * SUBLANES, :]

    def rows_of(vals):
        return [jnp.broadcast_to(vals[ii:ii + 1, :], (SUBLANES, tq)) for ii in range(SUBLANES)]

    for hi in range(ngroups):
        @pl.when((hi <= last_group) & (last_block >= topk))
        def _():
            v_hi = group(v_sc, hi)
            rows_hi = rows_of(v_hi)
            own = jnp.zeros((SUBLANES, tq), jnp.int32)
            for ii in range(SUBLANES):
                own = own + jnp.where(sub > ii, jnp.where(rows_hi[ii] >= v_hi, 1, 0),
                                      jnp.where(rows_hi[ii] > v_hi, 1, 0))
            for lo in range(hi):
                v_lo = group(v_sc, lo)
                add_lo = jnp.zeros((SUBLANES, tq), jnp.int32)
                for ii in range(SUBLANES):
                    add_lo = add_lo + jnp.where(rows_hi[ii] > v_lo, 1, 0)
                cnt_sc[lo * SUBLANES:(lo + 1) * SUBLANES, :] += add_lo
                for row in rows_of(v_lo):
                    own = own + jnp.where(row >= v_hi, 1, 0)
            cnt_sc[hi * SUBLANES:(hi + 1) * SUBLANES, :] += own

    return jnp.where(valid & (cnt_sc[...] < topk), 0.0, NEG_BIG)


def _lane_tile(x, reps):
    return jnp.concatenate([x] * reps, axis=1)


def _window_branch(qt_ref, kw_ref, vw_ref, qw_sc, sw_sc, t0, tq):
    d = NSA_HEAD_DIM
    sub = WIN_TQ
    span = WINDOW + sub
    krow = lax.broadcasted_iota(jnp.int32, (span, sub), 0)
    qcol = lax.broadcasted_iota(jnp.int32, (span, sub), 1)
    qw_sc[:, d:, :] = jnp.zeros((qw_sc.shape[0], qw_sc.shape[1] - d, qw_sc.shape[2]), qw_sc.dtype)
    starts, maxima = [], []
    for h in range(tq // sub):
        q0 = t0 + h * sub
        start = pl.multiple_of(jnp.maximum(q0 - WINDOW, 0), sub)
        rel = (qcol - krow) + (q0 - start)
        bias = jnp.where(rel >= 0, jnp.where(rel < WINDOW, 0.0, NEG_BIG), NEG_BIG)
        for r in range(NSA_HPG):
            qw_sc[h, 0:d, r * sub:(r + 1) * sub] = qt_ref[r, :, h * sub:(h + 1) * sub]
        s = _dot(kw_ref[pl.ds(start, span), :], qw_sc[h]) + _lane_tile(bias, NSA_HPG)
        sw_sc[h] = s
        starts.append(start)
        maxima.append(jnp.max(s, axis=0, keepdims=True))

    def finish():
        pieces = [[] for _ in range(NSA_HPG)]
        for h in range(tq // sub):
            p = jnp.exp(sw_sc[h] - maxima[h]).astype(BF16)
            acc = _dot(vw_ref[0, :, pl.ds(starts[h], span)], p)
            for r in range(NSA_HPG):
                a = acc[:, r * sub:(r + 1) * sub]
                pieces[r].append(a[0:d] * (1.0 / a[d:d + 1]))
        return [jnp.concatenate(p, axis=1) for p in pieces]

    return finish


def _cmp_win_kernel(*refs, n_cmp, topk):
    (qt_ref, kc_ref, vct_ref, ovt_ref, kw_ref, vw_ref, gl_ref, gb_ref,
     o_ref, selb_ref, qa_sc, qw_sc, sw_sc, sc_sc, v_sc, cnt_sc) = refs
    tq = qt_ref.shape[2]
    d = NSA_HEAD_DIM
    nc = kc_ref.shape[2]
    n_sel = ovt_ref.shape[0]
    t0 = pl.program_id(2) * tq

    window_finish = _window_branch(qt_ref, kw_ref, vw_ref, qw_sc, sw_sc, t0, tq)

    for r in range(NSA_HPG):
        qa_sc[:, r * tq:(r + 1) * tq] = qt_ref[r]
    cidx = lax.broadcasted_iota(jnp.int32, (nc, tq), 0)
    t = t0 + lax.broadcasted_iota(jnp.int32, (nc, tq), 1)
    cbias = jnp.where((cidx * CMP_STRIDE + (CMP_BLOCK - 1) <= t) & (cidx < n_cmp), 0.0, -jnp.inf)
    s = _dot(kc_ref[0, 0], qa_sc[...]) + _lane_tile(cbias, NSA_HPG)
    sc_sc[...] = s
    m = jnp.max(s, axis=0, keepdims=True)
    m = jnp.where(m > -jnp.inf, m, 0.0)

    gates_w = jax.nn.sigmoid(gl_ref[0, 2] + gb_ref[0, 2])
    o_win = [o * gates_w[r:r + 1, :] for r, o in enumerate(window_finish())]

    e = jnp.exp(sc_sc[...] - m).astype(BF16)
    both = _dot(jnp.concatenate([vct_ref[0, 0], ovt_ref[...]], axis=0), e)
    rinv = 1.0 / jnp.maximum(both[d:d + 1, :], 1e-30)
    gates_c = jax.nn.sigmoid(gl_ref[0, 0] + gb_ref[0, 0])
    imp = jnp.zeros((n_sel, tq), F32)
    for r in range(NSA_HPG):
        cols = slice(r * tq, (r + 1) * tq)
        o_cmp = both[0:d, cols] * rinv[:, cols] * gates_c[r:r + 1, :]
        o_ref[r] = (o_cmp + o_win[r]).astype(o_ref.dtype)
        imp = imp + both[V_AUG_ROWS:, cols] * rinv[:, cols]
    bias = _selection_bias(imp, t0, topk, v_sc, cnt_sc)
    n_pad = selb_ref.shape[2] - bias.shape[0]
    if n_pad:
        bias = jnp.concatenate([bias, jnp.zeros((n_pad, tq), F32)], axis=0)
    selb_ref[0, 0] = bias.astype(selb_ref.dtype)


def _gate_specs(nq, branch, nbranch, tq):
    return (pl.BlockSpec((1, nbranch, SUBLANES, tq), lambda b, g, i: (g, branch, 0, b * nq + i)),
            pl.BlockSpec((1, nbranch, SUBLANES, 1), lambda b, g, i: (g, branch, 0, 0)))


def _cmp_win_attn(q_t, kcmp, vcmp_t, kw_a, vw_a, gl_t, gbias, bsz, s):
    g, d = NSA_KV_GROUPS, NSA_HEAD_DIM
    tq = ATT_TQ
    nq = s // tq
    nc = s // CMP_STRIDE
    n_cmp = (s - CMP_BLOCK) // CMP_STRIDE + 1
    n_sel = s // SEL_BLOCK
    n_selp = _sel_pad(n_sel)
    topk = min(SEL_TOPK, n_sel)
    cmp_start = np.arange(nc) * CMP_STRIDE
    sel_start = np.arange(n_sel) * SEL_BLOCK
    ov = ((cmp_start[:, None] < sel_start[None, :] + SEL_BLOCK) &
          (cmp_start[:, None] + CMP_BLOCK > sel_start[None, :]) &
          (np.arange(nc)[:, None] < n_cmp))
    ovt = jnp.asarray(ov.T.astype(np.float32), dtype=BF16)
    gl_spec, gb_spec = _gate_specs(nq, 0, 3, tq)
    t_tokens = bsz * s
    head_tile = pl.BlockSpec((NSA_HPG, d, tq), lambda b, gi, i: (gi, 0, b * nq + i))
    return pl.pallas_call(
        functools.partial(_cmp_win_kernel, n_cmp=n_cmp, topk=topk),
        grid=(bsz, g, nq),
        in_specs=[
            head_tile,
            pl.BlockSpec((1, 1, nc, d), lambda b, gi, i: (b, gi, 0, 0)),
            pl.BlockSpec((1, 1, V_AUG_ROWS, nc), lambda b, gi, i: (b, gi, 0, 0)),
            pl.BlockSpec((n_sel, nc), lambda b, gi, i: (0, 0)),
            pl.BlockSpec((s, LANES), lambda b, gi, i: (b, gi)),
            pl.BlockSpec((1, V_AUG_ROWS, s), lambda b, gi, i: (gi, 0, b)),
            gl_spec, gb_spec,
        ],
        out_specs=[head_tile, pl.BlockSpec((1, 1, n_selp, tq), lambda b, gi, i: (b, gi, 0, i))],
        out_shape=[jax.ShapeDtypeStruct((NSA_HEADS, d, t_tokens), BF16),
                   jax.ShapeDtypeStruct((bsz, g, n_selp, s), BF16)],
        scratch_shapes=[pltpu.VMEM((d, NSA_HPG * tq), BF16),
                        pltpu.VMEM((tq // WIN_TQ, LANES, NSA_HPG * WIN_TQ), BF16),
                        pltpu.VMEM((tq // WIN_TQ, WINDOW + WIN_TQ, NSA_HPG * WIN_TQ), F32),
                        pltpu.VMEM((nc, NSA_HPG * tq), F32),
                        pltpu.VMEM((n_sel, tq), F32), pltpu.VMEM((n_sel, tq), jnp.int32)],
        compiler_params=_params(("parallel", "parallel", "parallel")),
        name="cmp_win_attn",
    )(q_t, kcmp, vcmp_t, ovt, kw_a, vw_a, gl_t, gbias)


def _sel_pad(n_sel):
    return -(-n_sel // NSA_HEAD_DIM) * NSA_HEAD_DIM


def _sel_attn_kernel(qt_ref, ka_ref, va_ref, selb_ref, gl_ref, gb_ref, o_ref,
                     qa0_sc, qa1_sc, s0_sc, s1_sc, mx0_sc, mx1_sc, m0_sc, m1_sc, acc0_sc, acc1_sc,
                     *, tq):
    tk = 2 * tq
    d = NSA_HEAD_DIM
    w = NSA_HPG * tq
    diag = pl.program_id(2)
    t0 = diag * tk
    streams = ((qa0_sc, s0_sc, mx0_sc, m0_sc, acc0_sc), (qa1_sc, s1_sc, mx1_sc, m1_sc, acc1_sc))
    for x, (qa_sc, _, _, m_sc, acc_sc) in enumerate(streams):
        for r in range(NSA_HPG):
            qa_sc[0:d, r * tq:(r + 1) * tq] = qt_ref[r, :, x * tq:(x + 1) * tq]
            qa_sc[d:, r * tq:(r + 1) * tq] = selb_ref[0, 0, :, x * tq:(x + 1) * tq]
        m_sc[...] = jnp.full(m_sc.shape, NEG_BIG, F32)
        acc_sc[...] = jnp.zeros(acc_sc.shape, F32)

    def scores(x, j, diagonal):
        qa_sc, s_sc, mx_sc, _, _ = streams[x]
        rows = (x + 1) * tq if diagonal else tk
        k0 = pl.multiple_of(j * tk, tk)
        s = _dot(ka_ref[pl.ds(k0, rows), :], qa_sc[...])
        if diagonal:
            kpos = k0 + lax.broadcasted_iota(jnp.int32, (rows, w), 0)
            t = t0 + x * tq + (lax.broadcasted_iota(jnp.int32, (rows, w), 1) & (tq - 1))
            s = jnp.where(kpos <= t, s, NEG_BIG)
        s_sc[0:rows, :] = s
        mx_sc[...] = jnp.max(s, axis=0, keepdims=True)

    def accumulate(x, j, diagonal=False):
        _, s_sc, mx_sc, m_sc, acc_sc = streams[x]
        rows = (x + 1) * tq if diagonal else tk
        k0 = pl.multiple_of(j * tk, tk)
        m_old = m_sc[...]
        m_new = jnp.maximum(m_old, mx_sc[...])
        p = jnp.exp(s_sc[0:rows, :] - m_new).astype(BF16)
        acc_sc[...] = jnp.exp(m_old - m_new) * acc_sc[...] + _dot(va_ref[0, :, pl.ds(k0, rows)], p)
        m_sc[...] = m_new

    scores(0, diag, True)
    scores(1, diag, True)
    accumulate(0, diag, True)

    def below(k, carry):
        scores(0, k, False)
        accumulate(1, jnp.where(k == 0, diag, k - 1))
        scores(1, k, False)
        accumulate(0, k)
        return carry

    lax.fori_loop(0, diag, below, 0)
    accumulate(1, jnp.where(diag == 0, diag, diag - 1))

    gates = jax.nn.sigmoid(gl_ref[0, 0] + gb_ref[0, 0])
    for x, (_, _, _, _, acc_sc) in enumerate(streams):
        for r in range(NSA_HPG):
            a = acc_sc[:, r * tq:(r + 1) * tq]
            cols = slice(x * tq, (x + 1) * tq)
            o_ref[r, :, cols] = (a[0:d] * (1.0 / a[d:d + 1]) * gates[r:r + 1, cols]).astype(o_ref.dtype)


def _sel_attn(q_t, ka, va, selb, gl_t, gbias, bsz, s):
    g, d = NSA_KV_GROUPS, NSA_HEAD_DIM
    tq = ATT_TQ
    tk = 2 * tq
    nq = s // tk
    w = NSA_HPG * tq
    n_selp = selb.shape[2]
    kd = d + n_selp
    gl_spec, gb_spec = _gate_specs(nq, 1, 1, tk)
    return pl.pallas_call(
        functools.partial(_sel_attn_kernel, tq=tq),
        grid=(bsz, g, nq),
        in_specs=[
            pl.BlockSpec((NSA_HPG, d, tk), lambda b, gi, i: (gi, 0, b * nq + i)),
            pl.BlockSpec((s, kd), lambda b, gi, i: (b, gi)),
            pl.BlockSpec((1, V_AUG_ROWS, s), lambda b, gi, i: (gi, 0, b)),
            pl.BlockSpec((1, 1, n_selp, tk), lambda b, gi, i: (b, gi, 0, i)),
            gl_spec, gb_spec,
        ],
        out_specs=pl.BlockSpec((NSA_HPG, d, tk), lambda b, gi, i: (gi, 0, b * nq + i)),
        out_shape=jax.ShapeDtypeStruct((NSA_HEADS, d, bsz * s), BF16),
        scratch_shapes=[pltpu.VMEM((kd, w), BF16), pltpu.VMEM((kd, w), BF16),
                        pltpu.VMEM((tk, w), F32), pltpu.VMEM((tk, w), F32),
                        pltpu.VMEM((1, w), F32), pltpu.VMEM((1, w), F32),
                        pltpu.VMEM((1, w), F32), pltpu.VMEM((1, w), F32),
                        pltpu.VMEM((V_AUG_ROWS, w), F32), pltpu.VMEM((V_AUG_ROWS, w), F32)],
        compiler_params=_params(("parallel", "parallel", "arbitrary")),
        name="sel_attn",
    )(q_t, ka, va, selb, gl_t, gbias)


def _even_layer(h2d, bsz, s, gain, w_in, ssd_conv_w, ssd_conv_b, dt_bias, a_log, d_skip, ssd_norm,
                conf_conv_w, conf_conv_b, conf_ln_g, conf_ln_b, w_out):
    o1 = SSD_INNER
    o2 = o1 + SSD_XBC
    o3 = o2 + SSD_HEADS
    o4 = o3 + 2 * CONF_WIDTH
    w_dt = w_in[:, o2:o3]
    w_nat = jnp.concatenate(
        [w_in[:, :o2], w_in[:, o3:], jnp.pad(w_dt, ((0, 0), (0, LANES - SSD_HEADS)))], axis=1).astype(BF16)
    nat_spec = [(SSD_INNER, F32), (SSD_XBC, F32), (2 * CONF_WIDTH, F32), (CONF_WIDTH, F32), (LANES, F32)]
    tm = min(PROJ_ROWS, s)
    z, xbc, glu, zc, dt_nat, dt_t = _norm_proj(
        h2d, gain, w_nat, nat_spec, w_dt.T.astype(BF16), jnp.zeros((SSD_HEADS,), F32),
        [(SSD_HEADS, F32)], tm)
    y_a = _ssd(xbc, z, dt_nat, dt_t, ssd_conv_w, ssd_conv_b, dt_bias, a_log, d_skip, ssd_norm, bsz, s)
    y_b = _conformer(glu, zc, conf_conv_w, conf_conv_b, conf_ln_g, conf_ln_b, bsz, s)
    return _outproj_even(h2d, y_a.reshape(bsz * s, SSD_INNER), y_b.reshape(bsz * s, CONF_WIDTH), w_out, tm)


def _odd_layer(h2d, bsz, s, gain, w_in, gate_bias, pe_k, w1_k, w2_k, pe_v, w1_v, w2_v, w_out,
               final_gain, final):
    g, r, d = NSA_KV_GROUPS, NSA_HPG, NSA_HEAD_DIM
    t_tokens = bsz * s
    sizes = [NSA_WIDTH] + [KV_WIDTH] * 6 + [3 * NSA_HEADS, NSA_WIDTH]
    offs = np.cumsum([0] + sizes)
    col = lambda k: w_in[:, offs[k]:offs[k + 1]]
    w_q, w_kc, w_vc, w_ks, w_vs, w_kw, w_vw, w_gl, w_z = [col(k) for k in range(9)]
    perm = np.array([[[(gi * r + ri) * 3 + br for ri in range(r)] for br in range(3)] for gi in range(g)])
    w_gl_g = jnp.pad(w_gl[:, perm.reshape(-1)].reshape(D_MODEL, g * 3, r),
                     ((0, 0), (0, 0), (0, SUBLANES - r))).reshape(D_MODEL, g * 3 * SUBLANES)
    gbias = jnp.pad(gate_bias[perm.reshape(-1)].reshape(g * 3, r), ((0, 0), (0, SUBLANES - r)))
    gbias = gbias.reshape(g, 3, SUBLANES, 1)
    scale = NSA_HEAD_DIM ** -0.5
    assert d + _sel_pad(s // SEL_BLOCK) == LANES

    def key_slots(w):
        return jnp.pad(w.reshape(D_MODEL, g, d), ((0, 0), (0, 0), (0, LANES - d))).reshape(D_MODEL, g * LANES)

    def value_rows(w):
        return jnp.pad(w.reshape(D_MODEL, g, d), ((0, 0), (0, 0), (0, V_AUG_ROWS - d))).reshape(D_MODEL, g * V_AUG_ROWS)

    ones_row = jnp.zeros((g, V_AUG_ROWS), F32).at[:, d].set(1.0).reshape(-1)
    w_nat = jnp.concatenate([w_kc, w_vc, key_slots(w_ks), key_slots(w_kw)], axis=1).astype(BF16)
    w_t = jnp.concatenate([w_q * scale, value_rows(w_vs), value_rows(w_vw), w_gl_g, w_z], axis=1).T.astype(BF16)
    t_bias = jnp.concatenate([jnp.zeros((NSA_WIDTH,), F32), ones_row, ones_row,
                              jnp.zeros((g * 3 * SUBLANES + NSA_WIDTH,), F32)])
    nat_spec = [(KV_WIDTH, F32), (KV_WIDTH, F32), (g * LANES, BF16), (g * LANES, BF16)]
    t_spec = [(NSA_WIDTH, BF16), (g * V_AUG_ROWS, BF16), (g * V_AUG_ROWS, BF16),
              (g * 3 * SUBLANES, F32), (NSA_WIDTH, F32)]
    tm = min(PROJ_ROWS, s)
    kc, vc, ks_a, kw_a, q_t, vs_a, vw_a, gl_t, z_t = _norm_proj(
        h2d, gain, w_nat, nat_spec, w_t, t_bias, t_spec, tm, onehot_out=2, regroup_outs=(0, 1), seq_len=s)

    q_t = q_t.reshape(NSA_HEADS, d, t_tokens)
    gl_t = gl_t.reshape(g, 3, SUBLANES, t_tokens)
    vs_a = vs_a.reshape(g, V_AUG_ROWS, t_tokens)
    vw_a = vw_a.reshape(g, V_AUG_ROWS, t_tokens)

    kcmp, vcmp_t = _compress(kc, vc, pe_k, w1_k, w2_k, pe_v, w1_v, w2_v, bsz, s)
    o_cw, selb = _cmp_win_attn(q_t, kcmp, vcmp_t, kw_a, vw_a, gl_t, gbias, bsz, s)
    o_sel = _sel_attn(q_t, ks_a, vs_a, selb, gl_t, gbias, bsz, s)
    flat = lambda o: o.reshape(NSA_WIDTH, t_tokens)
    return _outproj_odd(h2d, flat(o_cw), flat(o_sel), z_t, w_out, final_gain, final, tm)


def kernel(x, e_norm, e_w_in, e_ssd_conv_w, e_ssd_conv_b, e_dt_bias, e_a_log, e_d_skip, e_ssd_norm,
           e_conf_conv_w, e_conf_conv_b, e_conf_ln_g, e_conf_ln_b, e_w_out, o_norm, o_w_in, o_gate_bias,
           o_cmp_pe_k, o_cmp_w1_k, o_cmp_w2_k, o_cmp_pe_v, o_cmp_w1_v, o_cmp_w2_v, o_w_out, final_norm):
    bsz, s, d = x.shape
    depth = e_norm.shape[0] + o_norm.shape[0]
    h = x.reshape(bsz * s, d)
    for layer in range(depth):
        i = layer // 2
        if layer % 2 == 0:
            h = _even_layer(h, bsz, s, e_norm[i], e_w_in[i], e_ssd_conv_w[i], e_ssd_conv_b[i], e_dt_bias[i],
                            e_a_log[i], e_d_skip[i], e_ssd_norm[i], e_conf_conv_w[i], e_conf_conv_b[i],
                            e_conf_ln_g[i], e_conf_ln_b[i], e_w_out[i])
        else:
            h = _odd_layer(h, bsz, s, o_norm[i], o_w_in[i], o_gate_bias[i], o_cmp_pe_k[i], o_cmp_w1_k[i],
                           o_cmp_w2_k[i], o_cmp_pe_v[i], o_cmp_w1_v[i], o_cmp_w2_v[i], o_w_out[i],
                           final_norm, layer == depth - 1)
    return h.reshape(bsz, s, d)
```

```python
import functools
import math

import numpy as np
import jax
import jax.numpy as jnp
from jax import lax
from jax.experimental import pallas as pl
from jax.experimental.pallas import tpu as pltpu

F32 = jnp.float32
BF16 = jnp.bfloat16

D_MODEL = 1024
SSD_HEADS = 16
SSD_HEAD_DIM = 64
SSD_INNER = 1024
SSD_GROUPS = 2
SSD_STATE = 128
SSD_CONV = 4
SSD_CHUNK = 128
SSD_XBC = SSD_INNER + 2 * SSD_GROUPS * SSD_STATE
CONF_WIDTH = 1024
CONF_CONV_WIDTH = 31
NSA_HEADS = 16
NSA_KV_GROUPS = 4
NSA_HPG = 4
NSA_HEAD_DIM = 64
NSA_WIDTH = 1024
KV_WIDTH = 256
CMP_BLOCK = 32
CMP_STRIDE = 16
CMP_HIDDEN = 256
SEL_BLOCK = 64
SEL_TOPK = 16
WINDOW = 512
FORCE_BONUS = 1e6
NORM_EPS = 1e-6

LANES = 128
SUBLANES = 8
VMEM_LIMIT_BYTES = 56 * 1024 * 1024

PROJ_ROWS = 512
CONF_ROWS = 256
CONF_ROW_CHUNK = 32
CONF_LANE_CHUNK = 512
ATT_TQ = 256
CW_TQ = 512
WIN_TQ = 128
NEG_BIG = -1e30
V_AUG_ROWS = 80


def _dot(a, b, precision=None):
    return jnp.dot(a, b, preferred_element_type=F32, precision=precision)


def _dot_nt(a, b):
    return lax.dot_general(a, b, (((1,), (1,)), ((), ())), preferred_element_type=F32)


def _dot_tn(a, b):
    return lax.dot_general(a, b, (((0,), (0,)), ((), ())), preferred_element_type=F32)


def _silu(x):
    return x * jax.nn.sigmoid(x)


def _softplus(x):
    return jnp.maximum(x, 0.0) + jnp.log1p(jnp.exp(-jnp.abs(x)))


def _params(semantics):
    return pltpu.CompilerParams(dimension_semantics=semantics,
                                vmem_limit_bytes=VMEM_LIMIT_BYTES)


def _proj_kernel(*refs, nat_cols, t_rows, onehot_out, regroup_outs, seq_len):
    x_ref, g_ref = refs[0], refs[1]
    pos = 2
    w_ref = wt_ref = tb_ref = None
    if nat_cols:
        w_ref = refs[pos]
        pos += 1
    if t_rows:
        wt_ref, tb_ref = refs[pos], refs[pos + 1]
        pos += 2
    n_out = len(nat_cols) + len(t_rows)
    out_refs = refs[pos:pos + n_out]
    scratch = refs[pos + n_out:]
    x = x_ref[...]
    tm = x.shape[0]
    ms = jnp.mean(x * x, axis=-1, keepdims=True)
    xn = ((x * lax.rsqrt(ms + NORM_EPS)) * g_ref[...]).astype(BF16)
    k = 0
    for (a, b) in nat_cols:
        res = _dot(xn, w_ref[:, a:b])
        if k in regroup_outs:
            d = NSA_HEAD_DIM
            res_sc = scratch[0]
            per = LANES // d
            for c in range(res_sc.shape[0]):
                res_sc[c] = res[:, c * LANES:(c + 1) * LANES]
            for l in range(CMP_STRIDE):
                for c in range(res_sc.shape[0]):
                    rows = res_sc[c, pl.ds(l, tm // CMP_STRIDE, stride=CMP_STRIDE), :]
                    for j in range(per):
                        out_refs[k][c * per + j, :, l * d:(l + 1) * d] = rows[:, j * d:(j + 1) * d]
            k += 1
            continue
        if k == onehot_out:
            row = pl.program_id(0) * tm + lax.broadcasted_iota(jnp.int32, (tm, b - a), 0)
            blk = lax.shift_right_logical(lax.rem(row, seq_len), int(math.log2(SEL_BLOCK)))
            slot = lax.broadcasted_iota(jnp.int32, (tm, b - a), 1) & (LANES - 1)
            res = jnp.where(slot - NSA_HEAD_DIM == blk, 1.0, res)
        out_refs[k][...] = res.astype(out_refs[k].dtype)
        k += 1
    for (a, b) in t_rows:
        out_refs[k][...] = (_dot_nt(wt_ref[a:b, :], xn) + tb_ref[a:b, :]).astype(out_refs[k].dtype)
        k += 1


def _norm_proj(h2d, gain, w_nat, nat_spec, w_t, t_bias, t_spec, tm, onehot_out=-1, regroup_outs=(),
               seq_len=1):
    t_tokens, d = h2d.shape
    nat_cols, off = [], 0
    for width, _ in nat_spec:
        nat_cols.append((off, off + width))
        off += width
    t_rows, off = [], 0
    for rows, _ in t_spec:
        t_rows.append((off, off + rows))
        off += rows
    in_specs = [pl.BlockSpec((tm, d), lambda i: (i, 0)),
                pl.BlockSpec((1, d), lambda i: (0, 0))]
    args = [h2d, gain.reshape(1, d)]
    if nat_spec:
        in_specs.append(pl.BlockSpec(w_nat.shape, lambda i: (0, 0)))
        args.append(w_nat)
    if t_spec:
        in_specs.append(pl.BlockSpec(w_t.shape, lambda i: (0, 0)))
        in_specs.append(pl.BlockSpec((w_t.shape[0], 1), lambda i: (0, 0)))
        args += [w_t, t_bias.reshape(-1, 1).astype(F32)]
    out_shape, out_specs, scratch = [], [], []
    for k, (width, dt) in enumerate(nat_spec):
        if k in regroup_outs:
            slab = (NSA_KV_GROUPS, t_tokens // CMP_STRIDE, CMP_STRIDE * NSA_HEAD_DIM)
            out_shape.append(jax.ShapeDtypeStruct(slab, dt))
            out_specs.append(pl.BlockSpec((slab[0], tm // CMP_STRIDE, slab[2]), lambda i: (0, i, 0)))
            scratch = [pltpu.VMEM((width // LANES, tm, LANES), F32)]
        else:
            out_shape.append(jax.ShapeDtypeStruct((t_tokens, width), dt))
            out_specs.append(pl.BlockSpec((tm, width), lambda i: (i, 0)))
    for rows, dt in t_spec:
        out_shape.append(jax.ShapeDtypeStruct((rows, t_tokens), dt))
        out_specs.append(pl.BlockSpec((rows, tm), lambda i: (0, i)))
    return pl.pallas_call(
        functools.partial(_proj_kernel, nat_cols=tuple(nat_cols), t_rows=tuple(t_rows),
                          onehot_out=onehot_out, regroup_outs=tuple(regroup_outs), seq_len=seq_len),
        grid=(t_tokens // tm,),
        in_specs=in_specs,
        out_specs=out_specs,
        out_shape=out_shape,
        scratch_shapes=scratch,
        compiler_params=_params(("parallel",)),
        name="norm_proj",
    )(*args)


def _ssd_kernel(xbc_ref, z_ref, dt_ref, dtT_ref, cw_ref, cb_ref, dtb_ref, dtbT_ref,
                alog_ref, alogT_ref, dskip_ref, norm_ref, y_ref,
                state_sc, carry_sc, win_sc):
    L = SSD_CHUNK
    hp = LANES // SSD_HEAD_DIM
    gw = SSD_INNER // SSD_GROUPS
    heads_per_group = SSD_HEADS // SSD_GROUPS

    @pl.when(pl.program_id(1) == 0)
    def _():
        state_sc[...] = jnp.zeros_like(state_sc)
        carry_sc[...] = jnp.zeros_like(carry_sc)

    xraw = xbc_ref[0]
    win_sc[0:SUBLANES, :] = carry_sc[...]
    win_sc[SUBLANES:SUBLANES + L, :] = xraw
    carry_sc[...] = xraw[L - SUBLANES:L, :]
    conv = cw_ref[0:1, :] * win_sc[SUBLANES - 3:SUBLANES - 3 + L, :]
    for k in range(1, SSD_CONV):
        o = SUBLANES - (SSD_CONV - 1) + k
        conv = conv + cw_ref[k:k + 1, :] * win_sc[o:o + L, :]
    xact = _silu(conv + cb_ref[...])
    xs = xact[:, :SSD_INNER]
    bm = xact[:, SSD_INNER:SSD_INNER + SSD_GROUPS * SSD_STATE].astype(BF16)
    cm = xact[:, SSD_INNER + SSD_GROUPS * SSD_STATE:].astype(BF16)

    dt = _softplus(dt_ref[0] + dtb_ref[...])
    d_a = dt * (-jnp.exp(alog_ref[...]))
    rows = lax.broadcasted_iota(jnp.int32, (L, L), 0)
    cols = lax.broadcasted_iota(jnp.int32, (L, L), 1)
    causal = rows >= cols
    cum = _dot(causal.astype(F32), d_a, precision=lax.Precision.HIGHEST)
    dt_t = _softplus(dtT_ref[...] + dtbT_ref[...])
    d_a_t = dt_t * (-jnp.exp(alogT_ref[...]))
    cum_t = _dot(d_a_t, (rows <= cols).astype(F32), precision=lax.Precision.HIGHEST)
    ecum = jnp.exp(cum)
    cum_last = cum[L - 1:L, :]
    to_end = jnp.exp(cum_last - cum) * dt
    e_last = jnp.exp(cum_last)

    lane = lax.broadcasted_iota(jnp.int32, (L, LANES), 1)
    first_head = lane < SSD_HEAD_DIM
    lane1 = lax.broadcasted_iota(jnp.int32, (1, LANES), 1)
    first_head1 = lane1 < SSD_HEAD_DIM

    def per_head(arr, h0, mask):
        return jnp.where(mask, arr[:, h0:h0 + 1], arr[:, h0 + 1:h0 + 2])

    y_blocks, ecum_blocks, toend_blocks, elast_blocks = [], [], [], []
    for g in range(SSD_GROUPS):
        bm_g = bm[:, g * SSD_STATE:(g + 1) * SSD_STATE]
        cm_g = cm[:, g * SSD_STATE:(g + 1) * SSD_STATE]
        cb = _dot_nt(cm_g, bm_g)
        for j in range(heads_per_group // hp):
            h0 = g * heads_per_group + j * hp
            c0 = h0 * SSD_HEAD_DIM
            x_pair = xs[:, c0:c0 + LANES].astype(BF16)
            res = []
            for hh in range(hp):
                h = h0 + hh
                seg = cum[:, h:h + 1] - cum_t[h:h + 1, :]
                decay = jnp.exp(jnp.where(causal, seg, -jnp.inf))
                wts = (cb * decay) * dt_t[h:h + 1, :]
                res.append(_dot(wts.astype(BF16), x_pair))
            y_blocks.append(jnp.where(first_head, res[0], res[1]))
            ecum_blocks.append(per_head(ecum, h0, first_head))
            toend_blocks.append(per_head(to_end, h0, first_head))
            elast_blocks.append(per_head(e_last, h0, first_head1))
    y_diag = jnp.concatenate(y_blocks, axis=1)
    ecum_x = jnp.concatenate(ecum_blocks, axis=1)
    toend_x = jnp.concatenate(toend_blocks, axis=1)
    elast_x = jnp.concatenate(elast_blocks, axis=1)

    xw = (xs * toend_x).astype(BF16)
    y_off_blocks = []
    for g in range(SSD_GROUPS):
        bm_g = bm[:, g * SSD_STATE:(g + 1) * SSD_STATE]
        cm_g = cm[:, g * SSD_STATE:(g + 1) * SSD_STATE]
        st = state_sc[g]
        y_off_blocks.append(_dot(cm_g, st.astype(BF16)))
        state_sc[g] = st * elast_x[:, g * gw:(g + 1) * gw] + \
            _dot_tn(bm_g, xw[:, g * gw:(g + 1) * gw])
    y_off = jnp.concatenate(y_off_blocks, axis=1) * ecum_x

    y = y_diag + y_off + dskip_ref[...] * xs
    yg = y * _silu(z_ref[0])
    outs = []
    for g in range(SSD_GROUPS):
        yg_g = yg[:, g * gw:(g + 1) * gw]
        ms = jnp.mean(yg_g * yg_g, axis=-1, keepdims=True)
        outs.append(yg_g * lax.rsqrt(ms + NORM_EPS))
    y_ref[0] = (jnp.concatenate(outs, axis=1) * norm_ref[...]).astype(y_ref.dtype)


def _ssd(xbc, z, dt_nat, dt_t, conv_w, conv_b, dt_bias, a_log, d_skip, ssd_norm, bsz, s):
    nch = s // SSD_CHUNK
    L = SSD_CHUNK
    pad = LANES - SSD_HEADS
    dtb = jnp.pad(dt_bias, (0, pad)).reshape(1, LANES)
    alog = jnp.pad(a_log, (0, pad)).reshape(1, LANES)
    dskip_x = jnp.repeat(d_skip, SSD_HEAD_DIM).reshape(1, SSD_INNER)
    const = lambda shape: pl.BlockSpec(shape, lambda b, c: tuple(0 for _ in shape))
    return pl.pallas_call(
        _ssd_kernel,
        grid=(bsz, nch),
        in_specs=[
            pl.BlockSpec((1, L, SSD_XBC), lambda b, c: (b, c, 0)),
            pl.BlockSpec((1, L, SSD_INNER), lambda b, c: (b, c, 0)),
            pl.BlockSpec((1, L, LANES), lambda b, c: (b, c, 0)),
            pl.BlockSpec((SSD_HEADS, L), lambda b, c: (0, b * nch + c)),
            const((SSD_CONV, SSD_XBC)), const((1, SSD_XBC)),
            const((1, LANES)), const((SSD_HEADS, 1)),
            const((1, LANES)), const((SSD_HEADS, 1)),
            const((1, SSD_INNER)), const((1, SSD_INNER)),
        ],
        out_specs=pl.BlockSpec((1, L, SSD_INNER), lambda b, c: (b, c, 0)),
        out_shape=jax.ShapeDtypeStruct((bsz, s, SSD_INNER), BF16),
        scratch_shapes=[
            pltpu.VMEM((SSD_GROUPS, SSD_STATE, SSD_INNER // SSD_GROUPS), F32),
            pltpu.VMEM((SUBLANES, SSD_XBC), F32),
            pltpu.VMEM((SUBLANES + L, SSD_XBC), F32),
        ],
        compiler_params=_params(("parallel", "arbitrary")),
        name="ssd_scan",
    )(xbc.reshape(bsz, s, SSD_XBC), z.reshape(bsz, s, SSD_INNER),
      dt_nat.reshape(bsz, s, LANES), dt_t,
      conv_w, conv_b.reshape(1, SSD_XBC), dtb, dt_bias.reshape(SSD_HEADS, 1),
      alog, a_log.reshape(SSD_HEADS, 1), dskip_x, ssd_norm.reshape(1, SSD_INNER))


def _conf_kernel(glu_ref, zc_ref, w_ref, b_ref, lng_ref, lnb_ref, y_ref, sh_sc, u_sc):
    ts = glu_ref.shape[1]
    n = sh_sc.shape[1]
    halo = n - ts
    first = halo - (CONF_CONV_WIDTH - 1)

    @pl.when(pl.program_id(1) == 0)
    def _():
        sh_sc[0, 0:halo, :] = jnp.zeros((halo, CONF_WIDTH), F32)

    sh_sc[0, halo:n, :] = glu_ref[0, :, :CONF_WIDTH] * jax.nn.sigmoid(glu_ref[0, :, CONF_WIDTH:])
    tiles = sh_sc[0].reshape(n // SUBLANES, SUBLANES, CONF_WIDTH)
    sub = lax.broadcasted_iota(jnp.int32, (n // SUBLANES - 1, SUBLANES, CONF_WIDTH), 1)
    for s in range(1, SUBLANES):
        rolled = pltpu.roll(tiles, SUBLANES - s, 1)
        shifted = jnp.where(sub < SUBLANES - s, rolled[:-1], rolled[1:])
        sh_sc[s, 0:n - SUBLANES, :] = shifted.reshape(n - SUBLANES, CONF_WIDTH)
    for r0 in range(0, ts, CONF_ROW_CHUNK):
        for c0 in range(0, CONF_WIDTH, CONF_LANE_CHUNK):
            cs = slice(c0, c0 + CONF_LANE_CHUNK)
            acc = None
            for k in range(CONF_CONV_WIDTH):
                a, s = divmod(first + k, SUBLANES)
                o = r0 + a * SUBLANES
                window = sh_sc[s, o:o + CONF_ROW_CHUNK, cs]
                term = w_ref[k][None, :, cs] * window.reshape(CONF_ROW_CHUNK // SUBLANES, SUBLANES, -1)
                acc = term if acc is None else acc + term
            u_sc[r0:r0 + CONF_ROW_CHUNK, cs] = acc.reshape(CONF_ROW_CHUNK, -1)
    sh_sc[0, 0:halo, :] = sh_sc[0, ts:n, :]
    u = u_sc[...] + b_ref[...]
    mu = jnp.mean(u, axis=-1, keepdims=True)
    var = jnp.mean(jnp.square(u - mu), axis=-1, keepdims=True)
    un = (u - mu) * lax.rsqrt(var + NORM_EPS) * lng_ref[...] + lnb_ref[...]
    y_ref[0] = (_silu(un) * _silu(zc_ref[0])).astype(y_ref.dtype)


def _conformer(glu, zc, conv_w, conv_b, ln_g, ln_b, bsz, s):
    ts = min(CONF_ROWS, s)
    halo = 32
    w_tiles = jnp.broadcast_to(conv_w[:, None, :], (CONF_CONV_WIDTH, SUBLANES, CONF_WIDTH))
    const = lambda shape: pl.BlockSpec(shape, lambda b, i: tuple(0 for _ in shape))
    return pl.pallas_call(
        _conf_kernel,
        grid=(bsz, s // ts),
        in_specs=[
            pl.BlockSpec((1, ts, 2 * CONF_WIDTH), lambda b, i: (b, i, 0)),
            pl.BlockSpec((1, ts, CONF_WIDTH), lambda b, i: (b, i, 0)),
            const(w_tiles.shape), const((1, CONF_WIDTH)),
            const((1, CONF_WIDTH)), const((1, CONF_WIDTH)),
        ],
        out_specs=pl.BlockSpec((1, ts, CONF_WIDTH), lambda b, i: (b, i, 0)),
        out_shape=jax.ShapeDtypeStruct((bsz, s, CONF_WIDTH), BF16),
        scratch_shapes=[pltpu.VMEM((SUBLANES, halo + ts, CONF_WIDTH), F32),
                        pltpu.VMEM((ts, CONF_WIDTH), F32)],
        compiler_params=_params(("parallel", "arbitrary")),
        name="conformer",
    )(glu.reshape(bsz, s, 2 * CONF_WIDTH), zc.reshape(bsz, s, CONF_WIDTH), w_tiles,
      conv_b.reshape(1, -1), ln_g.reshape(1, -1), ln_b.reshape(1, -1))


def _outproj_even_kernel(h_ref, a_ref, b_ref, wa_ref, wb_ref, o_ref):
    o_ref[...] = h_ref[...] + (_dot(a_ref[...], wa_ref[...]) + _dot(b_ref[...], wb_ref[...]))


def _outproj_even(h2d, ya, yb, w_out, tm):
    t_tokens, d = h2d.shape
    wa = w_out[:SSD_INNER].astype(BF16)
    wb = w_out[SSD_INNER:].astype(BF16)
    row = lambda w: pl.BlockSpec((tm, w), lambda i: (i, 0))
    return pl.pallas_call(
        _outproj_even_kernel,
        grid=(t_tokens // tm,),
        in_specs=[row(d), row(SSD_INNER), row(CONF_WIDTH),
                  pl.BlockSpec(wa.shape, lambda i: (0, 0)),
                  pl.BlockSpec(wb.shape, lambda i: (0, 0))],
        out_specs=row(d),
        out_shape=jax.ShapeDtypeStruct((t_tokens, d), F32),
        compiler_params=_params(("parallel",)),
        name="outproj_even",
    )(h2d, ya, yb, wa, wb)


def _outproj_odd_kernel(h_ref, ocw_ref, os_ref, zt_ref, w_ref, g_ref, o_ref, *, final):
    o = ocw_ref[...].astype(F32) + os_ref[...].astype(F32)
    y = (o * _silu(zt_ref[...])).astype(BF16)
    out = h_ref[...] + _dot_tn(y, w_ref[...])
    if final:
        ms = jnp.mean(out * out, axis=-1, keepdims=True)
        out = (out * lax.rsqrt(ms + NORM_EPS)) * g_ref[...]
    o_ref[...] = out


def _outproj_odd(h2d, ocw, osel, z_t, w_out, final_gain, final, tm):
    t_tokens, d = h2d.shape
    w = w_out.astype(BF16)
    col = pl.BlockSpec((NSA_WIDTH, tm), lambda i: (0, i))
    return pl.pallas_call(
        functools.partial(_outproj_odd_kernel, final=final),
        grid=(t_tokens // tm,),
        in_specs=[pl.BlockSpec((tm, d), lambda i: (i, 0)), col, col, col,
                  pl.BlockSpec(w.shape, lambda i: (0, 0)),
                  pl.BlockSpec((1, d), lambda i: (0, 0))],
        out_specs=pl.BlockSpec((tm, d), lambda i: (i, 0)),
        out_shape=jax.ShapeDtypeStruct((t_tokens, d), F32),
        compiler_params=_params(("parallel",)),
        name="outproj_odd",
    )(h2d, ocw, osel, z_t, w, final_gain.reshape(1, d))


def _compress_kernel(xk_ref, xv_ref, pek_ref, pev_ref, w1k_ref, w1v_ref, w2k_ref, w2vt_ref,
                     kc_ref, vct_ref):
    half = CMP_STRIDE * NSA_HEAD_DIM

    def hidden(x_ref, pe_ref, w1_ref):
        x = x_ref[0, 0]
        n = x.shape[0]
        lo = (x + pe_ref[0:1, :]).astype(BF16)
        hi = (x + pe_ref[1:2, :]).astype(BF16)
        h = _dot(lo, w1_ref[0:half, :]) + pltpu.roll(_dot(hi, w1_ref[half:2 * half, :]), n - 1, 0)
        return _silu(h).astype(BF16)

    kc_ref[0, 0] = _dot(hidden(xk_ref, pek_ref, w1k_ref), w2k_ref[...]).astype(kc_ref.dtype)
    vct = _dot_nt(w2vt_ref[...], hidden(xv_ref, pev_ref, w1v_ref))
    row = lax.broadcasted_iota(jnp.int32, vct.shape, 0)
    vct_ref[0, 0] = jnp.where(row == NSA_HEAD_DIM, 1.0, vct).astype(vct_ref.dtype)


def _compress(kc, vc, pe_k, w1_k, w2_k, pe_v, w1_v, w2_v, bsz, s):
    g, d = NSA_KV_GROUPS, NSA_HEAD_DIM
    nh = s // CMP_STRIDE
    half = CMP_STRIDE * d

    def regroup(t):
        return t.reshape(g, bsz, nh, half)

    const = lambda shape: pl.BlockSpec(shape, lambda b, gi: tuple(0 for _ in shape))
    blk = pl.BlockSpec((1, 1, nh, half), lambda b, gi: (gi, b, 0, 0))
    return pl.pallas_call(
        _compress_kernel,
        grid=(bsz, g),
        in_specs=[blk, blk, const((2, half)), const((2, half)),
                  const((2 * half, CMP_HIDDEN)), const((2 * half, CMP_HIDDEN)),
                  const((CMP_HIDDEN, d)), const((V_AUG_ROWS, CMP_HIDDEN))],
        out_specs=[pl.BlockSpec((1, 1, nh, d), lambda b, gi: (b, gi, 0, 0)),
                   pl.BlockSpec((1, 1, V_AUG_ROWS, nh), lambda b, gi: (b, gi, 0, 0))],
        out_shape=[jax.ShapeDtypeStruct((bsz, g, nh, d), BF16),
                   jax.ShapeDtypeStruct((bsz, g, V_AUG_ROWS, nh), BF16)],
        compiler_params=_params(("parallel", "parallel")),
        name="compress_kv",
    )(regroup(kc), regroup(vc), pe_k.reshape(2, half), pe_v.reshape(2, half),
      w1_k.astype(BF16), w1_v.astype(BF16), w2_k.astype(BF16),
      jnp.pad(w2_v.T, ((0, V_AUG_ROWS - d), (0, 0))).astype(BF16))


def _selection_bias(imp, t0, topk, v_sc, cnt_sc):
    n_sel, tq = imp.shape
    shift = int(math.log2(SEL_BLOCK))
    ngroups = n_sel // SUBLANES
    jidx = lax.broadcasted_iota(jnp.int32, (n_sel, tq), 0)
    t = t0 + lax.broadcasted_iota(jnp.int32, (n_sel, tq), 1)
    cur = lax.shift_right_logical(t, shift)
    valid = jidx * SEL_BLOCK <= t
    forced = (jidx == 0) | (jidx == cur) | (jidx == cur - 1)
    v_sc[...] = jnp.where(valid, imp + jnp.where(forced, FORCE_BONUS, 0.0), -jnp.inf)
    cnt_sc[...] = jnp.zeros((n_sel, tq), jnp.int32)
    last_block = lax.shift_right_logical(t0 + tq - 1, shift)
    last_group = lax.shift_right_logical(last_block, int(math.log2(SUBLANES)))
    sub = lax.broadcasted_iota(jnp.int32, (SUBLANES, tq), 0)

    def group(ref, gi):
        return ref[gi * SUBLANES:(gi + 1) * SUBLANES, :]

    def rows_of(vals):
        return [jnp.broadcast_to(vals[ii:ii + 1, :], (SUBLANES, tq)) for ii in range(SUBLANES)]

    for hi in range(ngroups):
        @pl.when((hi <= last_group) & (last_block >= topk))
        def _():
            v_hi = group(v_sc, hi)
            rows_hi = rows_of(v_hi)
            own = jnp.zeros((SUBLANES, tq), jnp.int32)
            for ii in range(SUBLANES):
                own = own + jnp.where(sub > ii, jnp.where(rows_hi[ii] >= v_hi, 1, 0),
                                      jnp.where(rows_hi[ii] > v_hi, 1, 0))
            for lo in range(hi):
                v_lo = group(v_sc, lo)
                add_lo = jnp.zeros((SUBLANES, tq), jnp.int32)
                for ii in range(SUBLANES):
                    add_lo = add_lo + jnp.where(rows_hi[ii] > v_lo, 1, 0)
                cnt_sc[lo * SUBLANES:(lo + 1) * SUBLANES, :] += add_lo
                for row in rows_of(v_lo):
                    own = own + jnp.where(row >= v_hi, 1, 0)
            cnt_sc[hi * SUBLANES:(hi + 1) * SUBLANES, :] += own

    return jnp.where(valid & (cnt_sc[...] < topk), 0.0, NEG_BIG)


def _lane_tile(x, reps):
    return jnp.concatenate([x] * reps, axis=1)


def _window_branch(qt_ref, kw_ref, vw_ref, qw_sc, sw_sc, t0, tq):
    d = NSA_HEAD_DIM
    sub = WIN_TQ
    span = WINDOW + sub
    krow = lax.broadcasted_iota(jnp.int32, (span, sub), 0)
    qcol = lax.broadcasted_iota(jnp.int32, (span, sub), 1)
    qw_sc[:, d:, :] = jnp.zeros((qw_sc.shape[0], qw_sc.shape[1] - d, qw_sc.shape[2]), qw_sc.dtype)
    starts, maxima = [], []
    for h in range(tq // sub):
        q0 = t0 + h * sub
        start = pl.multiple_of(jnp.maximum(q0 - WINDOW, 0), sub)
        rel = (qcol - krow) + (q0 - start)
        bias = jnp.where(rel >= 0, jnp.where(rel < WINDOW, 0.0, NEG_BIG), NEG_BIG)
        for r in range(NSA_HPG):
            qw_sc[h, 0:d, r * sub:(r + 1) * sub] = qt_ref[r, :, h * sub:(h + 1) * sub]
        s = _dot(kw_ref[pl.ds(start, span), :], qw_sc[h]) + _lane_tile(bias, NSA_HPG)
        sw_sc[h] = s
        starts.append(start)
        maxima.append(jnp.max(s, axis=0, keepdims=True))

    def finish():
        pieces = [[] for _ in range(NSA_HPG)]
        for h in range(tq // sub):
            p = jnp.exp(sw_sc[h] - maxima[h]).astype(BF16)
            acc = _dot(vw_ref[0, :, pl.ds(starts[h], span)], p)
            for r in range(NSA_HPG):
                a = acc[:, r * sub:(r + 1) * sub]
                pieces[r].append(a[0:d] * (1.0 / a[d:d + 1]))
        return [jnp.concatenate(p, axis=1) for p in pieces]

    return finish


def _cmp_win_kernel(*refs, n_cmp, topk):
    (qt_ref, kc_ref, vct_ref, ovt_ref, kw_ref, vw_ref, gl_ref, gb_ref,
     o_ref, selb_ref, qa_sc, qw_sc, sw_sc, sc_sc, v_sc, cnt_sc) = refs
    tq = qt_ref.shape[2]
    d = NSA_HEAD_DIM
    nc = kc_ref.shape[2]
    n_sel = ovt_ref.shape[0]
    t0 = pl.program_id(2) * tq

    window_finish = _window_branch(qt_ref, kw_ref, vw_ref, qw_sc, sw_sc, t0, tq)

    for r in range(NSA_HPG):
        qa_sc[:, r * tq:(r + 1) * tq] = qt_ref[r]
    cidx = lax.broadcasted_iota(jnp.int32, (nc, tq), 0)
    t = t0 + lax.broadcasted_iota(jnp.int32, (nc, tq), 1)
    cbias = jnp.where((cidx * CMP_STRIDE + (CMP_BLOCK - 1) <= t) & (cidx < n_cmp), 0.0, -jnp.inf)
    s = _dot(kc_ref[0, 0], qa_sc[...]) + _lane_tile(cbias, NSA_HPG)
    sc_sc[...] = s
    m = jnp.max(s, axis=0, keepdims=True)
    m = jnp.where(m > -jnp.inf, m, 0.0)

    gates_w = jax.nn.sigmoid(gl_ref[0, 2] + gb_ref[0, 2])
    o_win = [o * gates_w[r:r + 1, :] for r, o in enumerate(window_finish())]

    e = jnp.exp(sc_sc[...] - m).astype(BF16)
    both = _dot(jnp.concatenate([vct_ref[0, 0], ovt_ref[...]], axis=0), e)
    rinv = 1.0 / jnp.maximum(both[d:d + 1, :], 1e-30)
    gates_c = jax.nn.sigmoid(gl_ref[0, 0] + gb_ref[0, 0])
    imp = jnp.zeros((n_sel, tq), F32)
    for r in range(NSA_HPG):
        cols = slice(r * tq, (r + 1) * tq)
        o_cmp = both[0:d, cols] * rinv[:, cols] * gates_c[r:r + 1, :]
        o_ref[r] = (o_cmp + o_win[r]).astype(o_ref.dtype)
        imp = imp + both[V_AUG_ROWS:, cols] * rinv[:, cols]
    bias = _selection_bias(imp, t0, topk, v_sc, cnt_sc)
    n_pad = selb_ref.shape[2] - bias.shape[0]
    if n_pad:
        bias = jnp.concatenate([bias, jnp.zeros((n_pad, tq), F32)], axis=0)
    selb_ref[0, 0] = bias.astype(selb_ref.dtype)


def _gate_specs(nq, branch, nbranch, tq):
    return (pl.BlockSpec((1, nbranch, SUBLANES, tq), lambda b, g, i: (g, branch, 0, b * nq + i)),
            pl.BlockSpec((1, nbranch, SUBLANES, 1), lambda b, g, i: (g, branch, 0, 0)))


def _cmp_win_attn(q_t, kcmp, vcmp_t, kw_a, vw_a, gl_t, gbias, bsz, s):
    g, d = NSA_KV_GROUPS, NSA_HEAD_DIM
    tq = min(CW_TQ, s)
    nq = s // tq
    nc = s // CMP_STRIDE
    n_cmp = (s - CMP_BLOCK) // CMP_STRIDE + 1
    n_sel = s // SEL_BLOCK
    n_selp = _sel_pad(n_sel)
    topk = min(SEL_TOPK, n_sel)
    cmp_start = np.arange(nc) * CMP_STRIDE
    sel_start = np.arange(n_sel) * SEL_BLOCK
    ov = ((cmp_start[:, None] < sel_start[None, :] + SEL_BLOCK) &
          (cmp_start[:, None] + CMP_BLOCK > sel_start[None, :]) &
          (np.arange(nc)[:, None] < n_cmp))
    ovt = jnp.asarray(ov.T.astype(np.float32), dtype=BF16)
    gl_spec, gb_spec = _gate_specs(nq, 0, 3, tq)
    t_tokens = bsz * s
    head_tile = pl.BlockSpec((NSA_HPG, d, tq), lambda b, gi, i: (gi, 0, b * nq + i))
    return pl.pallas_call(
        functools.partial(_cmp_win_kernel, n_cmp=n_cmp, topk=topk),
        grid=(bsz, g, nq),
        in_specs=[
            head_tile,
            pl.BlockSpec((1, 1, nc, d), lambda b, gi, i: (b, gi, 0, 0)),
            pl.BlockSpec((1, 1, V_AUG_ROWS, nc), lambda b, gi, i: (b, gi, 0, 0)),
            pl.BlockSpec((n_sel, nc), lambda b, gi, i: (0, 0)),
            pl.BlockSpec((s, LANES), lambda b, gi, i: (b, gi)),
            pl.BlockSpec((1, V_AUG_ROWS, s), lambda b, gi, i: (gi, 0, b)),
            gl_spec, gb_spec,
        ],
        out_specs=[head_tile, pl.BlockSpec((1, 1, n_selp, tq), lambda b, gi, i: (b, gi, 0, i))],
        out_shape=[jax.ShapeDtypeStruct((NSA_HEADS, d, t_tokens), BF16),
                   jax.ShapeDtypeStruct((bsz, g, n_selp, s), BF16)],
        scratch_shapes=[pltpu.VMEM((d, NSA_HPG * tq), BF16),
                        pltpu.VMEM((tq // WIN_TQ, LANES, NSA_HPG * WIN_TQ), BF16),
                        pltpu.VMEM((tq // WIN_TQ, WINDOW + WIN_TQ, NSA_HPG * WIN_TQ), F32),
                        pltpu.VMEM((nc, NSA_HPG * tq), F32),
                        pltpu.VMEM((n_sel, tq), F32), pltpu.VMEM((n_sel, tq), jnp.int32)],
        compiler_params=_params(("parallel", "parallel", "parallel")),
        name="cmp_win_attn",
    )(q_t, kcmp, vcmp_t, ovt, kw_a, vw_a, gl_t, gbias)


def _sel_pad(n_sel):
    return -(-n_sel // NSA_HEAD_DIM) * NSA_HEAD_DIM


def _sel_attn_kernel(qt_ref, ka_ref, va_ref, selb_ref, gl_ref, gb_ref, o_ref,
                     qa0_sc, qa1_sc, s0_sc, s1_sc, mx0_sc, mx1_sc, m0_sc, m1_sc, acc0_sc, acc1_sc,
                     *, tq):
    tk = 2 * tq
    d = NSA_HEAD_DIM
    w = NSA_HPG * tq
    diag = pl.program_id(2)
    t0 = diag * tk
    streams = ((qa0_sc, s0_sc, mx0_sc, m0_sc, acc0_sc), (qa1_sc, s1_sc, mx1_sc, m1_sc, acc1_sc))
    for x, (qa_sc, _, _, m_sc, acc_sc) in enumerate(streams):
        for r in range(NSA_HPG):
            qa_sc[0:d, r * tq:(r + 1) * tq] = qt_ref[r, :, x * tq:(x + 1) * tq]
            qa_sc[d:, r * tq:(r + 1) * tq] = selb_ref[0, 0, :, x * tq:(x + 1) * tq]
        m_sc[...] = jnp.full(m_sc.shape, NEG_BIG, F32)
        acc_sc[...] = jnp.zeros(acc_sc.shape, F32)

    def scores(x, j, diagonal):
        qa_sc, s_sc, mx_sc, _, _ = streams[x]
        rows = (x + 1) * tq if diagonal else tk
        k0 = pl.multiple_of(j * tk, tk)
        s = _dot(ka_ref[pl.ds(k0, rows), :], qa_sc[...])
        if diagonal:
            kpos = k0 + lax.broadcasted_iota(jnp.int32, (rows, w), 0)
            t = t0 + x * tq + (lax.broadcasted_iota(jnp.int32, (rows, w), 1) & (tq - 1))
            s = jnp.where(kpos <= t, s, NEG_BIG)
        s_sc[0:rows, :] = s
        mx_sc[...] = jnp.max(s, axis=0, keepdims=True)

    def accumulate(x, j, diagonal=False):
        _, s_sc, mx_sc, m_sc, acc_sc = streams[x]
        rows = (x + 1) * tq if diagonal else tk
        k0 = pl.multiple_of(j * tk, tk)
        m_old = m_sc[...]
        m_new = jnp.maximum(m_old, mx_sc[...])
        p = jnp.exp(s_sc[0:rows, :] - m_new).astype(BF16)
        acc_sc[...] = jnp.exp(m_old - m_new) * acc_sc[...] + _dot(va_ref[0, :, pl.ds(k0, rows)], p)
        m_sc[...] = m_new

    scores(0, diag, True)
    scores(1, diag, True)
    accumulate(0, diag, True)

    def below(k, carry):
        scores(0, k, False)
        accumulate(1, jnp.where(k == 0, diag, k - 1))
        scores(1, k, False)
        accumulate(0, k)
        return carry

    lax.fori_loop(0, diag, below, 0)
    accumulate(1, jnp.where(diag == 0, diag, diag - 1))

    gates = jax.nn.sigmoid(gl_ref[0, 0] + gb_ref[0, 0])
    for x, (_, _, _, _, acc_sc) in enumerate(streams):
        for r in range(NSA_HPG):
            a = acc_sc[:, r * tq:(r + 1) * tq]
            cols = slice(x * tq, (x + 1) * tq)
            o_ref[r, :, cols] = (a[0:d] * (1.0 / a[d:d + 1]) * gates[r:r + 1, cols]).astype(o_ref.dtype)


def _sel_attn(q_t, ka, va, selb, gl_t, gbias, bsz, s):
    g, d = NSA_KV_GROUPS, NSA_HEAD_DIM
    tq = ATT_TQ
    tk = 2 * tq
    nq = s // tk
    w = NSA_HPG * tq
    n_selp = selb.shape[2]
    kd = d + n_selp
    gl_spec, gb_spec = _gate_specs(nq, 1, 1, tk)
    return pl.pallas_call(
        functools.partial(_sel_attn_kernel, tq=tq),
        grid=(bsz, g, nq),
        in_specs=[
            pl.BlockSpec((NSA_HPG, d, tk), lambda b, gi, i: (gi, 0, b * nq + i)),
            pl.BlockSpec((s, kd), lambda b, gi, i: (b, gi)),
            pl.BlockSpec((1, V_AUG_ROWS, s), lambda b, gi, i: (gi, 0, b)),
            pl.BlockSpec((1, 1, n_selp, tk), lambda b, gi, i: (b, gi, 0, i)),
            gl_spec, gb_spec,
        ],
        out_specs=pl.BlockSpec((NSA_HPG, d, tk), lambda b, gi, i: (gi, 0, b * nq + i)),
        out_shape=jax.ShapeDtypeStruct((NSA_HEADS, d, bsz * s), BF16),
        scratch_shapes=[pltpu.VMEM((kd, w), BF16), pltpu.VMEM((kd, w), BF16),
                        pltpu.VMEM((tk, w), F32), pltpu.VMEM((tk, w), F32),
                        pltpu.VMEM((1, w), F32), pltpu.VMEM((1, w), F32),
                        pltpu.VMEM((1, w), F32), pltpu.VMEM((1, w), F32),
                        pltpu.VMEM((V_AUG_ROWS, w), F32), pltpu.VMEM((V_AUG_ROWS, w), F32)],
        compiler_params=_params(("parallel", "parallel", "arbitrary")),
        name="sel_attn",
    )(q_t, ka, va, selb, gl_t, gbias)


def _even_layer(h2d, bsz, s, gain, w_in, ssd_conv_w, ssd_conv_b, dt_bias, a_log, d_skip, ssd_norm,
                conf_conv_w, conf_conv_b, conf_ln_g, conf_ln_b, w_out):
    o1 = SSD_INNER
    o2 = o1 + SSD_XBC
    o3 = o2 + SSD_HEADS
    o4 = o3 + 2 * CONF_WIDTH
    w_dt = w_in[:, o2:o3]
    w_nat = jnp.concatenate(
        [w_in[:, :o2], w_in[:, o3:], jnp.pad(w_dt, ((0, 0), (0, LANES - SSD_HEADS)))], axis=1).astype(BF16)
    nat_spec = [(SSD_INNER, F32), (SSD_XBC, F32), (2 * CONF_WIDTH, F32), (CONF_WIDTH, F32), (LANES, F32)]
    tm = min(PROJ_ROWS, s)
    z, xbc, glu, zc, dt_nat, dt_t = _norm_proj(
        h2d, gain, w_nat, nat_spec, w_dt.T.astype(BF16), jnp.zeros((SSD_HEADS,), F32),
        [(SSD_HEADS, F32)], tm)
    y_a = _ssd(xbc, z, dt_nat, dt_t, ssd_conv_w, ssd_conv_b, dt_bias, a_log, d_skip, ssd_norm, bsz, s)
    y_b = _conformer(glu, zc, conf_conv_w, conf_conv_b, conf_ln_g, conf_ln_b, bsz, s)
    return _outproj_even(h2d, y_a.reshape(bsz * s, SSD_INNER), y_b.reshape(bsz * s, CONF_WIDTH), w_out, tm)


def _odd_layer(h2d, bsz, s, gain, w_in, gate_bias, pe_k, w1_k, w2_k, pe_v, w1_v, w2_v, w_out,
               final_gain, final):
    g, r, d = NSA_KV_GROUPS, NSA_HPG, NSA_HEAD_DIM
    t_tokens = bsz * s
    sizes = [NSA_WIDTH] + [KV_WIDTH] * 6 + [3 * NSA_HEADS, NSA_WIDTH]
    offs = np.cumsum([0] + sizes)
    col = lambda k: w_in[:, offs[k]:offs[k + 1]]
    w_q, w_kc, w_vc, w_ks, w_vs, w_kw, w_vw, w_gl, w_z = [col(k) for k in range(9)]
    perm = np.array([[[(gi * r + ri) * 3 + br for ri in range(r)] for br in range(3)] for gi in range(g)])
    w_gl_g = jnp.pad(w_gl[:, perm.reshape(-1)].reshape(D_MODEL, g * 3, r),
                     ((0, 0), (0, 0), (0, SUBLANES - r))).reshape(D_MODEL, g * 3 * SUBLANES)
    gbias = jnp.pad(gate_bias[perm.reshape(-1)].reshape(g * 3, r), ((0, 0), (0, SUBLANES - r)))
    gbias = gbias.reshape(g, 3, SUBLANES, 1)
    scale = NSA_HEAD_DIM ** -0.5
    assert d + _sel_pad(s // SEL_BLOCK) == LANES

    def key_slots(w):
        return jnp.pad(w.reshape(D_MODEL, g, d), ((0, 0), (0, 0), (0, LANES - d))).reshape(D_MODEL, g * LANES)

    def value_rows(w):
        return jnp.pad(w.reshape(D_MODEL, g, d), ((0, 0), (0, 0), (0, V_AUG_ROWS - d))).reshape(D_MODEL, g * V_AUG_ROWS)

    ones_row = jnp.zeros((g, V_AUG_ROWS), F32).at[:, d].set(1.0).reshape(-1)
    w_nat = jnp.concatenate([w_kc, w_vc, key_slots(w_ks), key_slots(w_kw)], axis=1).astype(BF16)
    w_t = jnp.concatenate([w_q * scale, value_rows(w_vs), value_rows(w_vw), w_gl_g, w_z], axis=1).T.astype(BF16)
    t_bias = jnp.concatenate([jnp.zeros((NSA_WIDTH,), F32), ones_row, ones_row,
                              jnp.zeros((g * 3 * SUBLANES + NSA_WIDTH,), F32)])
    nat_spec = [(KV_WIDTH, F32), (KV_WIDTH, F32), (g * LANES, BF16), (g * LANES, BF16)]
    t_spec = [(NSA_WIDTH, BF16), (g * V_AUG_ROWS, BF16), (g * V_AUG_ROWS, BF16),
              (g * 3 * SUBLANES, F32), (NSA_WIDTH, F32)]
    tm = min(PROJ_ROWS, s)
    kc, vc, ks_a, kw_a, q_t, vs_a, vw_a, gl_t, z_t = _norm_proj(
        h2d, gain, w_nat, nat_spec, w_t, t_bias, t_spec, tm, onehot_out=2, regroup_outs=(0, 1), seq_len=s)

    q_t = q_t.reshape(NSA_HEADS, d, t_tokens)
    gl_t = gl_t.reshape(g, 3, SUBLANES, t_tokens)
    vs_a = vs_a.reshape(g, V_AUG_ROWS, t_tokens)
    vw_a = vw_a.reshape(g, V_AUG_ROWS, t_tokens)

    kcmp, vcmp_t = _compress(kc, vc, pe_k, w1_k, w2_k, pe_v, w1_v, w2_v, bsz, s)
    o_cw, selb = _cmp_win_attn(q_t, kcmp, vcmp_t, kw_a, vw_a, gl_t, gbias, bsz, s)
    o_sel = _sel_attn(q_t, ks_a, vs_a, selb, gl_t, gbias, bsz, s)
    flat = lambda o: o.reshape(NSA_WIDTH, t_tokens)
    return _outproj_odd(h2d, flat(o_cw), flat(o_sel), z_t, w_out, final_gain, final, tm)


def kernel(x, e_norm, e_w_in, e_ssd_conv_w, e_ssd_conv_b, e_dt_bias, e_a_log, e_d_skip, e_ssd_norm,
           e_conf_conv_w, e_conf_conv_b, e_conf_ln_g, e_conf_ln_b, e_w_out, o_norm, o_w_in, o_gate_bias,
           o_cmp_pe_k, o_cmp_w1_k, o_cmp_w2_k, o_cmp_pe_v, o_cmp_w1_v, o_cmp_w2_v, o_w_out, final_norm):
    bsz, s, d = x.shape
    depth = e_norm.shape[0] + o_norm.shape[0]
    h = x.reshape(bsz * s, d)
    for layer in range(depth):
        i = layer // 2
        if layer % 2 == 0:
            h = _even_layer(h, bsz, s, e_norm[i], e_w_in[i], e_ssd_conv_w[i], e_ssd_conv_b[i], e_dt_bias[i],
                            e_a_log[i], e_d_skip[i], e_ssd_norm[i], e_conf_conv_w[i], e_conf_conv_b[i],
                            e_conf_ln_g[i], e_conf_ln_b[i], e_w_out[i])
        else:
            h = _odd_layer(h, bsz, s, o_norm[i], o_w_in[i], o_gate_bias[i], o_cmp_pe_k[i], o_cmp_w1_k[i],
                           o_cmp_w2_k[i], o_cmp_pe_v[i], o_cmp_w1_v[i], o_cmp_w2_v[i], o_w_out[i],
                           final_norm, layer == depth - 1)
    return h.reshape(bsz, s, d)
```

```python
import functools
import math

import numpy as np
import jax
import jax.numpy as jnp
from jax import lax
from jax.experimental import pallas as pl
from jax.experimental.pallas import tpu as pltpu

F32 = jnp.float32
BF16 = jnp.bfloat16

D_MODEL = 1024
SSD_HEADS = 16
SSD_HEAD_DIM = 64
SSD_INNER = 1024
SSD_GROUPS = 2
SSD_STATE = 128
SSD_CONV = 4
SSD_CHUNK = 128
SSD_XBC = SSD_INNER + 2 * SSD_GROUPS * SSD_STATE
CONF_WIDTH = 1024
CONF_CONV_WIDTH = 31
NSA_HEADS = 16
NSA_KV_GROUPS = 4
NSA_HPG = 4
NSA_HEAD_DIM = 64
NSA_WIDTH = 1024
KV_WIDTH = 256
CMP_BLOCK = 32
CMP_STRIDE = 16
CMP_HIDDEN = 256
SEL_BLOCK = 64
SEL_TOPK = 16
WINDOW = 512
FORCE_BONUS = 1e6
NORM_EPS = 1e-6

LANES = 128
SUBLANES = 8
VMEM_LIMIT_BYTES = 56 * 1024 * 1024

PROJ_ROWS = 512
CONF_ROWS = 256
CONF_ROW_CHUNK = 32
CONF_LANE_CHUNK = 512
ATT_TQ = 256
CW_TQ = 1024
WIN_TQ = 128
NEG_BIG = -1e30
V_AUG_ROWS = 80


def _dot(a, b, precision=None):
    return jnp.dot(a, b, preferred_element_type=F32, precision=precision)


def _dot_nt(a, b):
    return lax.dot_general(a, b, (((1,), (1,)), ((), ())), preferred_element_type=F32)


def _dot_tn(a, b):
    return lax.dot_general(a, b, (((0,), (0,)), ((), ())), preferred_element_type=F32)


def _silu(x):
    return x * jax.nn.sigmoid(x)


def _softplus(x):
    return jnp.maximum(x, 0.0) + jnp.log1p(jnp.exp(-jnp.abs(x)))


def _params(semantics):
    return pltpu.CompilerParams(dimension_semantics=semantics,
                                vmem_limit_bytes=VMEM_LIMIT_BYTES)


def _proj_kernel(*refs, nat_cols, t_rows, onehot_out, regroup_outs, seq_len):
    x_ref, g_ref = refs[0], refs[1]
    pos = 2
    w_ref = wt_ref = tb_ref = None
    if nat_cols:
        w_ref = refs[pos]
        pos += 1
    if t_rows:
        wt_ref, tb_ref = refs[pos], refs[pos + 1]
        pos += 2
    n_out = len(nat_cols) + len(t_rows)
    out_refs = refs[pos:pos + n_out]
    scratch = refs[pos + n_out:]
    x = x_ref[...]
    tm = x.shape[0]
    ms = jnp.mean(x * x, axis=-1, keepdims=True)
    xn = ((x * lax.rsqrt(ms + NORM_EPS)) * g_ref[...]).astype(BF16)
    k = 0
    for (a, b) in nat_cols:
        res = _dot(xn, w_ref[:, a:b])
        if k in regroup_outs:
            d = NSA_HEAD_DIM
            res_sc = scratch[0]
            per = LANES // d
            for c in range(res_sc.shape[0]):
                res_sc[c] = res[:, c * LANES:(c + 1) * LANES]
            for l in range(CMP_STRIDE):
                for c in range(res_sc.shape[0]):
                    rows = res_sc[c, pl.ds(l, tm // CMP_STRIDE, stride=CMP_STRIDE), :]
                    for j in range(per):
                        out_refs[k][c * per + j, :, l * d:(l + 1) * d] = rows[:, j * d:(j + 1) * d]
            k += 1
            continue
        if k == onehot_out:
            row = pl.program_id(0) * tm + lax.broadcasted_iota(jnp.int32, (tm, b - a), 0)
            blk = lax.shift_right_logical(lax.rem(row, seq_len), int(math.log2(SEL_BLOCK)))
            slot = lax.broadcasted_iota(jnp.int32, (tm, b - a), 1) & (LANES - 1)
            res = jnp.where(slot - NSA_HEAD_DIM == blk, 1.0, res)
        out_refs[k][...] = res.astype(out_refs[k].dtype)
        k += 1
    for (a, b) in t_rows:
        out_refs[k][...] = (_dot_nt(wt_ref[a:b, :], xn) + tb_ref[a:b, :]).astype(out_refs[k].dtype)
        k += 1


def _norm_proj(h2d, gain, w_nat, nat_spec, w_t, t_bias, t_spec, tm, onehot_out=-1, regroup_outs=(),
               seq_len=1):
    t_tokens, d = h2d.shape
    nat_cols, off = [], 0
    for width, _ in nat_spec:
        nat_cols.append((off, off + width))
        off += width
    t_rows, off = [], 0
    for rows, _ in t_spec:
        t_rows.append((off, off + rows))
        off += rows
    in_specs = [pl.BlockSpec((tm, d), lambda i: (i, 0)),
                pl.BlockSpec((1, d), lambda i: (0, 0))]
    args = [h2d, gain.reshape(1, d)]
    if nat_spec:
        in_specs.append(pl.BlockSpec(w_nat.shape, lambda i: (0, 0)))
        args.append(w_nat)
    if t_spec:
        in_specs.append(pl.BlockSpec(w_t.shape, lambda i: (0, 0)))
        in_specs.append(pl.BlockSpec((w_t.shape[0], 1), lambda i: (0, 0)))
        args += [w_t, t_bias.reshape(-1, 1).astype(F32)]
    out_shape, out_specs, scratch = [], [], []
    for k, (width, dt) in enumerate(nat_spec):
        if k in regroup_outs:
            slab = (NSA_KV_GROUPS, t_tokens // CMP_STRIDE, CMP_STRIDE * NSA_HEAD_DIM)
            out_shape.append(jax.ShapeDtypeStruct(slab, dt))
            out_specs.append(pl.BlockSpec((slab[0], tm // CMP_STRIDE, slab[2]), lambda i: (0, i, 0)))
            scratch = [pltpu.VMEM((width // LANES, tm, LANES), F32)]
        else:
            out_shape.append(jax.ShapeDtypeStruct((t_tokens, width), dt))
            out_specs.append(pl.BlockSpec((tm, width), lambda i: (i, 0)))
    for rows, dt in t_spec:
        out_shape.append(jax.ShapeDtypeStruct((rows, t_tokens), dt))
        out_specs.append(pl.BlockSpec((rows, tm), lambda i: (0, i)))
    return pl.pallas_call(
        functools.partial(_proj_kernel, nat_cols=tuple(nat_cols), t_rows=tuple(t_rows),
                          onehot_out=onehot_out, regroup_outs=tuple(regroup_outs), seq_len=seq_len),
        grid=(t_tokens // tm,),
        in_specs=in_specs,
        out_specs=out_specs,
        out_shape=out_shape,
        scratch_shapes=scratch,
        compiler_params=_params(("parallel",)),
        name="norm_proj",
    )(*args)


def _ssd_kernel(xbc_ref, z_ref, dt_ref, dtT_ref, cw_ref, cb_ref, dtb_ref, dtbT_ref,
                alog_ref, alogT_ref, dskip_ref, norm_ref, y_ref,
                state_sc, carry_sc, win_sc):
    L = SSD_CHUNK
    hp = LANES // SSD_HEAD_DIM
    gw = SSD_INNER // SSD_GROUPS
    heads_per_group = SSD_HEADS // SSD_GROUPS

    @pl.when(pl.program_id(1) == 0)
    def _():
        state_sc[...] = jnp.zeros_like(state_sc)
        carry_sc[...] = jnp.zeros_like(carry_sc)

    xraw = xbc_ref[0]
    win_sc[0:SUBLANES, :] = carry_sc[...]
    win_sc[SUBLANES:SUBLANES + L, :] = xraw
    carry_sc[...] = xraw[L - SUBLANES:L, :]
    conv = cw_ref[0:1, :] * win_sc[SUBLANES - 3:SUBLANES - 3 + L, :]
    for k in range(1, SSD_CONV):
        o = SUBLANES - (SSD_CONV - 1) + k
        conv = conv + cw_ref[k:k + 1, :] * win_sc[o:o + L, :]
    xact = _silu(conv + cb_ref[...])
    xs = xact[:, :SSD_INNER]
    bm = xact[:, SSD_INNER:SSD_INNER + SSD_GROUPS * SSD_STATE].astype(BF16)
    cm = xact[:, SSD_INNER + SSD_GROUPS * SSD_STATE:].astype(BF16)

    dt = _softplus(dt_ref[0] + dtb_ref[...])
    d_a = dt * (-jnp.exp(alog_ref[...]))
    rows = lax.broadcasted_iota(jnp.int32, (L, L), 0)
    cols = lax.broadcasted_iota(jnp.int32, (L, L), 1)
    causal = rows >= cols
    cum = _dot(causal.astype(F32), d_a, precision=lax.Precision.HIGHEST)
    dt_t = _softplus(dtT_ref[...] + dtbT_ref[...])
    d_a_t = dt_t * (-jnp.exp(alogT_ref[...]))
    cum_t = _dot(d_a_t, (rows <= cols).astype(F32), precision=lax.Precision.HIGHEST)
    ecum = jnp.exp(cum)
    cum_last = cum[L - 1:L, :]
    to_end = jnp.exp(cum_last - cum) * dt
    e_last = jnp.exp(cum_last)

    lane = lax.broadcasted_iota(jnp.int32, (L, LANES), 1)
    first_head = lane < SSD_HEAD_DIM
    lane1 = lax.broadcasted_iota(jnp.int32, (1, LANES), 1)
    first_head1 = lane1 < SSD_HEAD_DIM

    def per_head(arr, h0, mask):
        return jnp.where(mask, arr[:, h0:h0 + 1], arr[:, h0 + 1:h0 + 2])

    y_blocks, ecum_blocks, toend_blocks, elast_blocks = [], [], [], []
    for g in range(SSD_GROUPS):
        bm_g = bm[:, g * SSD_STATE:(g + 1) * SSD_STATE]
        cm_g = cm[:, g * SSD_STATE:(g + 1) * SSD_STATE]
        cb = _dot_nt(cm_g, bm_g)
        for j in range(heads_per_group // hp):
            h0 = g * heads_per_group + j * hp
            c0 = h0 * SSD_HEAD_DIM
            x_pair = xs[:, c0:c0 + LANES].astype(BF16)
            res = []
            for hh in range(hp):
                h = h0 + hh
                seg = cum[:, h:h + 1] - cum_t[h:h + 1, :]
                decay = jnp.exp(jnp.where(causal, seg, -jnp.inf))
                wts = (cb * decay) * dt_t[h:h + 1, :]
                res.append(_dot(wts.astype(BF16), x_pair))
            y_blocks.append(jnp.where(first_head, res[0], res[1]))
            ecum_blocks.append(per_head(ecum, h0, first_head))
            toend_blocks.append(per_head(to_end, h0, first_head))
            elast_blocks.append(per_head(e_last, h0, first_head1))
    y_diag = jnp.concatenate(y_blocks, axis=1)
    ecum_x = jnp.concatenate(ecum_blocks, axis=1)
    toend_x = jnp.concatenate(toend_blocks, axis=1)
    elast_x = jnp.concatenate(elast_blocks, axis=1)

    xw = (xs * toend_x).astype(BF16)
    y_off_blocks = []
    for g in range(SSD_GROUPS):
        bm_g = bm[:, g * SSD_STATE:(g + 1) * SSD_STATE]
        cm_g = cm[:, g * SSD_STATE:(g + 1) * SSD_STATE]
        st = state_sc[g]
        y_off_blocks.append(_dot(cm_g, st.astype(BF16)))
        state_sc[g] = st * elast_x[:, g * gw:(g + 1) * gw] + \
            _dot_tn(bm_g, xw[:, g * gw:(g + 1) * gw])
    y_off = jnp.concatenate(y_off_blocks, axis=1) * ecum_x

    y = y_diag + y_off + dskip_ref[...] * xs
    yg = y * _silu(z_ref[0])
    outs = []
    for g in range(SSD_GROUPS):
        yg_g = yg[:, g * gw:(g + 1) * gw]
        ms = jnp.mean(yg_g * yg_g, axis=-1, keepdims=True)
        outs.append(yg_g * lax.rsqrt(ms + NORM_EPS))
    y_ref[0] = (jnp.concatenate(outs, axis=1) * norm_ref[...]).astype(y_ref.dtype)


def _ssd(xbc, z, dt_nat, dt_t, conv_w, conv_b, dt_bias, a_log, d_skip, ssd_norm, bsz, s):
    nch = s // SSD_CHUNK
    L = SSD_CHUNK
    pad = LANES - SSD_HEADS
    dtb = jnp.pad(dt_bias, (0, pad)).reshape(1, LANES)
    alog = jnp.pad(a_log, (0, pad)).reshape(1, LANES)
    dskip_x = jnp.repeat(d_skip, SSD_HEAD_DIM).reshape(1, SSD_INNER)
    const = lambda shape: pl.BlockSpec(shape, lambda b, c: tuple(0 for _ in shape))
    return pl.pallas_call(
        _ssd_kernel,
        grid=(bsz, nch),
        in_specs=[
            pl.BlockSpec((1, L, SSD_XBC), lambda b, c: (b, c, 0)),
            pl.BlockSpec((1, L, SSD_INNER), lambda b, c: (b, c, 0)),
            pl.BlockSpec((1, L, LANES), lambda b, c: (b, c, 0)),
            pl.BlockSpec((SSD_HEADS, L), lambda b, c: (0, b * nch + c)),
            const((SSD_CONV, SSD_XBC)), const((1, SSD_XBC)),
            const((1, LANES)), const((SSD_HEADS, 1)),
            const((1, LANES)), const((SSD_HEADS, 1)),
            const((1, SSD_INNER)), const((1, SSD_INNER)),
        ],
        out_specs=pl.BlockSpec((1, L, SSD_INNER), lambda b, c: (b, c, 0)),
        out_shape=jax.ShapeDtypeStruct((bsz, s, SSD_INNER), BF16),
        scratch_shapes=[
            pltpu.VMEM((SSD_GROUPS, SSD_STATE, SSD_INNER // SSD_GROUPS), F32),
            pltpu.VMEM((SUBLANES, SSD_XBC), F32),
            pltpu.VMEM((SUBLANES + L, SSD_XBC), F32),
        ],
        compiler_params=_params(("parallel", "arbitrary")),
        name="ssd_scan",
    )(xbc.reshape(bsz, s, SSD_XBC), z.reshape(bsz, s, SSD_INNER),
      dt_nat.reshape(bsz, s, LANES), dt_t,
      conv_w, conv_b.reshape(1, SSD_XBC), dtb, dt_bias.reshape(SSD_HEADS, 1),
      alog, a_log.reshape(SSD_HEADS, 1), dskip_x, ssd_norm.reshape(1, SSD_INNER))


def _conf_kernel(glu_ref, zc_ref, w_ref, b_ref, lng_ref, lnb_ref, y_ref, sh_sc, u_sc):
    ts = glu_ref.shape[1]
    n = sh_sc.shape[1]
    halo = n - ts
    first = halo - (CONF_CONV_WIDTH - 1)

    @pl.when(pl.program_id(1) == 0)
    def _():
        sh_sc[0, 0:halo, :] = jnp.zeros((halo, CONF_WIDTH), F32)

    sh_sc[0, halo:n, :] = glu_ref[0, :, :CONF_WIDTH] * jax.nn.sigmoid(glu_ref[0, :, CONF_WIDTH:])
    tiles = sh_sc[0].reshape(n // SUBLANES, SUBLANES, CONF_WIDTH)
    sub = lax.broadcasted_iota(jnp.int32, (n // SUBLANES - 1, SUBLANES, CONF_WIDTH), 1)
    for s in range(1, SUBLANES):
        rolled = pltpu.roll(tiles, SUBLANES - s, 1)
        shifted = jnp.where(sub < SUBLANES - s, rolled[:-1], rolled[1:])
        sh_sc[s, 0:n - SUBLANES, :] = shifted.reshape(n - SUBLANES, CONF_WIDTH)
    for r0 in range(0, ts, CONF_ROW_CHUNK):
        for c0 in range(0, CONF_WIDTH, CONF_LANE_CHUNK):
            cs = slice(c0, c0 + CONF_LANE_CHUNK)
            acc = None
            for k in range(CONF_CONV_WIDTH):
                a, s = divmod(first + k, SUBLANES)
                o = r0 + a * SUBLANES
                window = sh_sc[s, o:o + CONF_ROW_CHUNK, cs]
                term = w_ref[k][None, :, cs] * window.reshape(CONF_ROW_CHUNK // SUBLANES, SUBLANES, -1)
                acc = term if acc is None else acc + term
            u_sc[r0:r0 + CONF_ROW_CHUNK, cs] = acc.reshape(CONF_ROW_CHUNK, -1)
    sh_sc[0, 0:halo, :] = sh_sc[0, ts:n, :]
    u = u_sc[...] + b_ref[...]
    mu = jnp.mean(u, axis=-1, keepdims=True)
    var = jnp.mean(jnp.square(u - mu), axis=-1, keepdims=True)
    un = (u - mu) * lax.rsqrt(var + NORM_EPS) * lng_ref[...] + lnb_ref[...]
    y_ref[0] = (_silu(un) * _silu(zc_ref[0])).astype(y_ref.dtype)


def _conformer(glu, zc, conv_w, conv_b, ln_g, ln_b, bsz, s):
    ts = min(CONF_ROWS, s)
    halo = 32
    w_tiles = jnp.broadcast_to(conv_w[:, None, :], (CONF_CONV_WIDTH, SUBLANES, CONF_WIDTH))
    const = lambda shape: pl.BlockSpec(shape, lambda b, i: tuple(0 for _ in shape))
    return pl.pallas_call(
        _conf_kernel,
        grid=(bsz, s // ts),
        in_specs=[
            pl.BlockSpec((1, ts, 2 * CONF_WIDTH), lambda b, i: (b, i, 0)),
            pl.BlockSpec((1, ts, CONF_WIDTH), lambda b, i: (b, i, 0)),
            const(w_tiles.shape), const((1, CONF_WIDTH)),
            const((1, CONF_WIDTH)), const((1, CONF_WIDTH)),
        ],
        out_specs=pl.BlockSpec((1, ts, CONF_WIDTH), lambda b, i: (b, i, 0)),
        out_shape=jax.ShapeDtypeStruct((bsz, s, CONF_WIDTH), BF16),
        scratch_shapes=[pltpu.VMEM((SUBLANES, halo + ts, CONF_WIDTH), F32),
                        pltpu.VMEM((ts, CONF_WIDTH), F32)],
        compiler_params=_params(("parallel", "arbitrary")),
        name="conformer",
    )(glu.reshape(bsz, s, 2 * CONF_WIDTH), zc.reshape(bsz, s, CONF_WIDTH), w_tiles,
      conv_b.reshape(1, -1), ln_g.reshape(1, -1), ln_b.reshape(1, -1))


def _outproj_even_kernel(h_ref, a_ref, b_ref, wa_ref, wb_ref, o_ref):
    o_ref[...] = h_ref[...] + (_dot(a_ref[...], wa_ref[...]) + _dot(b_ref[...], wb_ref[...]))


def _outproj_even(h2d, ya, yb, w_out, tm):
    t_tokens, d = h2d.shape
    wa = w_out[:SSD_INNER].astype(BF16)
    wb = w_out[SSD_INNER:].astype(BF16)
    row = lambda w: pl.BlockSpec((tm, w), lambda i: (i, 0))
    return pl.pallas_call(
        _outproj_even_kernel,
        grid=(t_tokens // tm,),
        in_specs=[row(d), row(SSD_INNER), row(CONF_WIDTH),
                  pl.BlockSpec(wa.shape, lambda i: (0, 0)),
                  pl.BlockSpec(wb.shape, lambda i: (0, 0))],
        out_specs=row(d),
        out_shape=jax.ShapeDtypeStruct((t_tokens, d), F32),
        compiler_params=_params(("parallel",)),
        name="outproj_even",
    )(h2d, ya, yb, wa, wb)


def _outproj_odd_kernel(h_ref, ocw_ref, os_ref, zt_ref, w_ref, g_ref, o_ref, *, final):
    o = ocw_ref[...].astype(F32) + os_ref[...].astype(F32)
    y = (o * _silu(zt_ref[...])).astype(BF16)
    out = h_ref[...] + _dot_tn(y, w_ref[...])
    if final:
        ms = jnp.mean(out * out, axis=-1, keepdims=True)
        out = (out * lax.rsqrt(ms + NORM_EPS)) * g_ref[...]
    o_ref[...] = out


def _outproj_odd(h2d, ocw, osel, z_t, w_out, final_gain, final, tm):
    t_tokens, d = h2d.shape
    w = w_out.astype(BF16)
    col = pl.BlockSpec((NSA_WIDTH, tm), lambda i: (0, i))
    return pl.pallas_call(
        functools.partial(_outproj_odd_kernel, final=final),
        grid=(t_tokens // tm,),
        in_specs=[pl.BlockSpec((tm, d), lambda i: (i, 0)), col, col, col,
                  pl.BlockSpec(w.shape, lambda i: (0, 0)),
                  pl.BlockSpec((1, d), lambda i: (0, 0))],
        out_specs=pl.BlockSpec((tm, d), lambda i: (i, 0)),
        out_shape=jax.ShapeDtypeStruct((t_tokens, d), F32),
        compiler_params=_params(("parallel",)),
        name="outproj_odd",
    )(h2d, ocw, osel, z_t, w, final_gain.reshape(1, d))


def _compress_kernel(xk_ref, xv_ref, pek_ref, pev_ref, w1k_ref, w1v_ref, w2k_ref, w2vt_ref,
                     kc_ref, vct_ref):
    half = CMP_STRIDE * NSA_HEAD_DIM

    def hidden(x_ref, pe_ref, w1_ref):
        x = x_ref[0, 0]
        n = x.shape[0]
        lo = (x + pe_ref[0:1, :]).astype(BF16)
        hi = (x + pe_ref[1:2, :]).astype(BF16)
        h = _dot(lo, w1_ref[0:half, :]) + pltpu.roll(_dot(hi, w1_ref[half:2 * half, :]), n - 1, 0)
        return _silu(h).astype(BF16)

    kc_ref[0, 0] = _dot(hidden(xk_ref, pek_ref, w1k_ref), w2k_ref[...]).astype(kc_ref.dtype)
    vct = _dot_nt(w2vt_ref[...], hidden(xv_ref, pev_ref, w1v_ref))
    row = lax.broadcasted_iota(jnp.int32, vct.shape, 0)
    vct_ref[0, 0] = jnp.where(row == NSA_HEAD_DIM, 1.0, vct).astype(vct_ref.dtype)


def _compress(kc, vc, pe_k, w1_k, w2_k, pe_v, w1_v, w2_v, bsz, s):
    g, d = NSA_KV_GROUPS, NSA_HEAD_DIM
    nh = s // CMP_STRIDE
    half = CMP_STRIDE * d

    def regroup(t):
        return t.reshape(g, bsz, nh, half)

    const = lambda shape: pl.BlockSpec(shape, lambda b, gi: tuple(0 for _ in shape))
    blk = pl.BlockSpec((1, 1, nh, half), lambda b, gi: (gi, b, 0, 0))
    return pl.pallas_call(
        _compress_kernel,
        grid=(bsz, g),
        in_specs=[blk, blk, const((2, half)), const((2, half)),
                  const((2 * half, CMP_HIDDEN)), const((2 * half, CMP_HIDDEN)),
                  const((CMP_HIDDEN, d)), const((V_AUG_ROWS, CMP_HIDDEN))],
        out_specs=[pl.BlockSpec((1, 1, nh, d), lambda b, gi: (b, gi, 0, 0)),
                   pl.BlockSpec((1, 1, V_AUG_ROWS, nh), lambda b, gi: (b, gi, 0, 0))],
        out_shape=[jax.ShapeDtypeStruct((bsz, g, nh, d), BF16),
                   jax.ShapeDtypeStruct((bsz, g, V_AUG_ROWS, nh), BF16)],
        compiler_params=_params(("parallel", "parallel")),
        name="compress_kv",
    )(regroup(kc), regroup(vc), pe_k.reshape(2, half), pe_v.reshape(2, half),
      w1_k.astype(BF16), w1_v.astype(BF16), w2_k.astype(BF16),
      jnp.pad(w2_v.T, ((0, V_AUG_ROWS - d), (0, 0))).astype(BF16))


def _selection_bias(imp, t0, topk, v_sc, cnt_sc):
    n_sel, tq = imp.shape
    shift = int(math.log2(SEL_BLOCK))
    ngroups = n_sel // SUBLANES
    jidx = lax.broadcasted_iota(jnp.int32, (n_sel, tq), 0)
    t = t0 + lax.broadcasted_iota(jnp.int32, (n_sel, tq), 1)
    cur = lax.shift_right_logical(t, shift)
    valid = jidx * SEL_BLOCK <= t
    forced = (jidx == 0) | (jidx == cur) | (jidx == cur - 1)
    v_sc[...] = jnp.where(valid, imp + jnp.where(forced, FORCE_BONUS, 0.0), -jnp.inf)
    cnt_sc[...] = jnp.zeros((n_sel, tq), jnp.int32)
    last_block = lax.shift_right_logical(t0 + tq - 1, shift)
    last_group = lax.shift_right_logical(last_block, int(math.log2(SUBLANES)))
    sub = lax.broadcasted_iota(jnp.int32, (SUBLANES, tq), 0)

    def group(ref, gi):
        return ref[gi * SUBLANES:(gi + 1) * SUBLANES, :]

    def rows_of(vals):
        return [jnp.broadcast_to(vals[ii:ii + 1, :], (SUBLANES, tq)) for ii in range(SUBLANES)]

    for hi in range(ngroups):
        @pl.when((hi <= last_group) & (last_block >= topk))
        def _():
            v_hi = group(v_sc, hi)
            rows_hi = rows_of(v_hi)
            own = jnp.zeros((SUBLANES, tq), jnp.int32)
            for ii in range(SUBLANES):
                own = own + jnp.where(sub > ii, jnp.where(rows_hi[ii] >= v_hi, 1, 0),
                                      jnp.where(rows_hi[ii] > v_hi, 1, 0))
            for lo in range(hi):
                v_lo = group(v_sc, lo)
                add_lo = jnp.zeros((SUBLANES, tq), jnp.int32)
                for ii in range(SUBLANES):
                    add_lo = add_lo + jnp.where(rows_hi[ii] > v_lo, 1, 0)
                cnt_sc[lo * SUBLANES:(lo + 1) * SUBLANES, :] += add_lo
                for row in rows_of(v_lo):
                    own = own + jnp.where(row >= v_hi, 1, 0)
            cnt_sc[hi * SUBLANES:(hi + 1) * SUBLANES, :] += own

    return jnp.where(valid & (cnt_sc[...] < topk), 0.0, NEG_BIG)


def _lane_tile(x, reps):
    return jnp.concatenate([x] * reps, axis=1)


def _window_branch(qt_ref, kw_ref, vw_ref, qw_sc, sw_sc, t0, tq):
    d = NSA_HEAD_DIM
    sub = WIN_TQ
    span = WINDOW + sub
    krow = lax.broadcasted_iota(jnp.int32, (span, sub), 0)
    qcol = lax.broadcasted_iota(jnp.int32, (span, sub), 1)
    qw_sc[:, d:, :] = jnp.zeros((qw_sc.shape[0], qw_sc.shape[1] - d, qw_sc.shape[2]), qw_sc.dtype)
    starts, maxima = [], []
    for h in range(tq // sub):
        q0 = t0 + h * sub
        start = pl.multiple_of(jnp.maximum(q0 - WINDOW, 0), sub)
        rel = (qcol - krow) + (q0 - start)
        bias = jnp.where(rel >= 0, jnp.where(rel < WINDOW, 0.0, NEG_BIG), NEG_BIG)
        for r in range(NSA_HPG):
            qw_sc[h, 0:d, r * sub:(r + 1) * sub] = qt_ref[r, :, h * sub:(h + 1) * sub]
        s = _dot(kw_ref[pl.ds(start, span), :], qw_sc[h]) + _lane_tile(bias, NSA_HPG)
        sw_sc[h] = s
        starts.append(start)
        maxima.append(jnp.max(s, axis=0, keepdims=True))

    def finish():
        pieces = [[] for _ in range(NSA_HPG)]
        for h in range(tq // sub):
            p = jnp.exp(sw_sc[h] - maxima[h]).astype(BF16)
            acc = _dot(vw_ref[0, :, pl.ds(starts[h], span)], p)
            for r in range(NSA_HPG):
                a = acc[:, r * sub:(r + 1) * sub]
                pieces[r].append(a[0:d] * (1.0 / a[d:d + 1]))
        return [jnp.concatenate(p, axis=1) for p in pieces]

    return finish


def _cmp_win_kernel(*refs, n_cmp, topk):
    (qt_ref, kc_ref, vct_ref, ovt_ref, kw_ref, vw_ref, gl_ref, gb_ref,
     o_ref, selb_ref, qa_sc, qw_sc, sw_sc, sc_sc, v_sc, cnt_sc) = refs
    tq = qt_ref.shape[2]
    d = NSA_HEAD_DIM
    nc = kc_ref.shape[2]
    n_sel = ovt_ref.shape[0]
    t0 = pl.program_id(2) * tq

    window_finish = _window_branch(qt_ref, kw_ref, vw_ref, qw_sc, sw_sc, t0, tq)

    for r in range(NSA_HPG):
        qa_sc[:, r * tq:(r + 1) * tq] = qt_ref[r]
    cidx = lax.broadcasted_iota(jnp.int32, (nc, tq), 0)
    t = t0 + lax.broadcasted_iota(jnp.int32, (nc, tq), 1)
    cbias = jnp.where((cidx * CMP_STRIDE + (CMP_BLOCK - 1) <= t) & (cidx < n_cmp), 0.0, -jnp.inf)
    s = _dot(kc_ref[0, 0], qa_sc[...]) + _lane_tile(cbias, NSA_HPG)
    sc_sc[...] = s
    m = jnp.max(s, axis=0, keepdims=True)
    m = jnp.where(m > -jnp.inf, m, 0.0)

    gates_w = jax.nn.sigmoid(gl_ref[0, 2] + gb_ref[0, 2])
    o_win = [o * gates_w[r:r + 1, :] for r, o in enumerate(window_finish())]

    e = jnp.exp(sc_sc[...] - m).astype(BF16)
    both = _dot(jnp.concatenate([vct_ref[0, 0], ovt_ref[...]], axis=0), e)
    rinv = 1.0 / jnp.maximum(both[d:d + 1, :], 1e-30)
    gates_c = jax.nn.sigmoid(gl_ref[0, 0] + gb_ref[0, 0])
    imp = jnp.zeros((n_sel, tq), F32)
    for r in range(NSA_HPG):
        cols = slice(r * tq, (r + 1) * tq)
        o_cmp = both[0:d, cols] * rinv[:, cols] * gates_c[r:r + 1, :]
        o_ref[r] = (o_cmp + o_win[r]).astype(o_ref.dtype)
        imp = imp + both[V_AUG_ROWS:, cols] * rinv[:, cols]
    bias = _selection_bias(imp, t0, topk, v_sc, cnt_sc)
    n_pad = selb_ref.shape[2] - bias.shape[0]
    if n_pad:
        bias = jnp.concatenate([bias, jnp.zeros((n_pad, tq), F32)], axis=0)
    selb_ref[0, 0] = bias.astype(selb_ref.dtype)


def _gate_specs(nq, branch, nbranch, tq):
    return (pl.BlockSpec((1, nbranch, SUBLANES, tq), lambda b, g, i: (g, branch, 0, b * nq + i)),
            pl.BlockSpec((1, nbranch, SUBLANES, 1), lambda b, g, i: (g, branch, 0, 0)))


def _cmp_win_attn(q_t, kcmp, vcmp_t, kw_a, vw_a, gl_t, gbias, bsz, s):
    g, d = NSA_KV_GROUPS, NSA_HEAD_DIM
    tq = min(CW_TQ, s)
    nq = s // tq
    nc = s // CMP_STRIDE
    n_cmp = (s - CMP_BLOCK) // CMP_STRIDE + 1
    n_sel = s // SEL_BLOCK
    n_selp = _sel_pad(n_sel)
    topk = min(SEL_TOPK, n_sel)
    cmp_start = np.arange(nc) * CMP_STRIDE
    sel_start = np.arange(n_sel) * SEL_BLOCK
    ov = ((cmp_start[:, None] < sel_start[None, :] + SEL_BLOCK) &
          (cmp_start[:, None] + CMP_BLOCK > sel_start[None, :]) &
          (np.arange(nc)[:, None] < n_cmp))
    ovt = jnp.asarray(ov.T.astype(np.float32), dtype=BF16)
    gl_spec, gb_spec = _gate_specs(nq, 0, 3, tq)
    t_tokens = bsz * s
    head_tile = pl.BlockSpec((NSA_HPG, d, tq), lambda b, gi, i: (gi, 0, b * nq + i))
    return pl.pallas_call(
        functools.partial(_cmp_win_kernel, n_cmp=n_cmp, topk=topk),
        grid=(bsz, g, nq),
        in_specs=[
            head_tile,
            pl.BlockSpec((1, 1, nc, d), lambda b, gi, i: (b, gi, 0, 0)),
            pl.BlockSpec((1, 1, V_AUG_ROWS, nc), lambda b, gi, i: (b, gi, 0, 0)),
            pl.BlockSpec((n_sel, nc), lambda b, gi, i: (0, 0)),
            pl.BlockSpec((s, LANES), lambda b, gi, i: (b, gi)),
            pl.BlockSpec((1, V_AUG_ROWS, s), lambda b, gi, i: (gi, 0, b)),
            gl_spec, gb_spec,
        ],
        out_specs=[head_tile, pl.BlockSpec((1, 1, n_selp, tq), lambda b, gi, i: (b, gi, 0, i))],
        out_shape=[jax.ShapeDtypeStruct((NSA_HEADS, d, t_tokens), BF16),
                   jax.ShapeDtypeStruct((bsz, g, n_selp, s), BF16)],
        scratch_shapes=[pltpu.VMEM((d, NSA_HPG * tq), BF16),
                        pltpu.VMEM((tq // WIN_TQ, LANES, NSA_HPG * WIN_TQ), BF16),
                        pltpu.VMEM((tq // WIN_TQ, WINDOW + WIN_TQ, NSA_HPG * WIN_TQ), F32),
                        pltpu.VMEM((nc, NSA_HPG * tq), F32),
                        pltpu.VMEM((n_sel, tq), F32), pltpu.VMEM((n_sel, tq), jnp.int32)],
        compiler_params=_params(("parallel", "parallel", "parallel")),
        name="cmp_win_attn",
    )(q_t, kcmp, vcmp_t, ovt, kw_a, vw_a, gl_t, gbias)


def _sel_pad(n_sel):
    return -(-n_sel // NSA_HEAD_DIM) * NSA_HEAD_DIM


def _sel_attn_kernel(qt_ref, ka_ref, va_ref, selb_ref, gl_ref, gb_ref, o_ref,
                     qa0_sc, qa1_sc, s0_sc, s1_sc, mx0_sc, mx1_sc, m0_sc, m1_sc, acc0_sc, acc1_sc,
                     *, tq):
    tk = 2 * tq
    d = NSA_HEAD_DIM
    w = NSA_HPG * tq
    diag = pl.program_id(2)
    t0 = diag * tk
    streams = ((qa0_sc, s0_sc, mx0_sc, m0_sc, acc0_sc), (qa1_sc, s1_sc, mx1_sc, m1_sc, acc1_sc))
    for x, (qa_sc, _, _, m_sc, acc_sc) in enumerate(streams):
        for r in range(NSA_HPG):
            qa_sc[0:d, r * tq:(r + 1) * tq] = qt_ref[r, :, x * tq:(x + 1) * tq]
            qa_sc[d:, r * tq:(r + 1) * tq] = selb_ref[0, 0, :, x * tq:(x + 1) * tq]
        m_sc[...] = jnp.full(m_sc.shape, NEG_BIG, F32)
        acc_sc[...] = jnp.zeros(acc_sc.shape, F32)

    def scores(x, j, diagonal):
        qa_sc, s_sc, mx_sc, _, _ = streams[x]
        rows = (x + 1) * tq if diagonal else tk
        k0 = pl.multiple_of(j * tk, tk)
        s = _dot(ka_ref[pl.ds(k0, rows), :], qa_sc[...])
        if diagonal:
            kpos = k0 + lax.broadcasted_iota(jnp.int32, (rows, w), 0)
            t = t0 + x * tq + (lax.broadcasted_iota(jnp.int32, (rows, w), 1) & (tq - 1))
            s = jnp.where(kpos <= t, s, NEG_BIG)
        s_sc[0:rows, :] = s
        mx_sc[...] = jnp.max(s, axis=0, keepdims=True)

    def accumulate(x, j, diagonal=False):
        _, s_sc, mx_sc, m_sc, acc_sc = streams[x]
        rows = (x + 1) * tq if diagonal else tk
        k0 = pl.multiple_of(j * tk, tk)
        m_old = m_sc[...]
        m_new = jnp.maximum(m_old, mx_sc[...])
        p = jnp.exp(s_sc[0:rows, :] - m_new).astype(BF16)
        acc_sc[...] = jnp.exp(m_old - m_new) * acc_sc[...] + _dot(va_ref[0, :, pl.ds(k0, rows)], p)
        m_sc[...] = m_new

    scores(0, diag, True)
    scores(1, diag, True)
    accumulate(0, diag, True)

    def below(k, carry):
        scores(0, k, False)
        accumulate(1, jnp.where(k == 0, diag, k - 1))
        scores(1, k, False)
        accumulate(0, k)
        return carry

    lax.fori_loop(0, diag, below, 0)
    accumulate(1, jnp.where(diag == 0, diag, diag - 1))

    gates = jax.nn.sigmoid(gl_ref[0, 0] + gb_ref[0, 0])
    for x, (_, _, _, _, acc_sc) in enumerate(streams):
        for r in range(NSA_HPG):
            a = acc_sc[:, r * tq:(r + 1) * tq]
            cols = slice(x * tq, (x + 1) * tq)
            o_ref[r, :, cols] = (a[0:d] * (1.0 / a[d:d + 1]) * gates[r:r + 1, cols]).astype(o_ref.dtype)


def _sel_attn(q_t, ka, va, selb, gl_t, gbias, bsz, s):
    g, d = NSA_KV_GROUPS, NSA_HEAD_DIM
    tq = ATT_TQ
    tk = 2 * tq
    nq = s // tk
    w = NSA_HPG * tq
    n_selp = selb.shape[2]
    kd = d + n_selp
    gl_spec, gb_spec = _gate_specs(nq, 1, 1, tk)
    return pl.pallas_call(
        functools.partial(_sel_attn_kernel, tq=tq),
        grid=(bsz, g, nq),
        in_specs=[
            pl.BlockSpec((NSA_HPG, d, tk), lambda b, gi, i: (gi, 0, b * nq + i)),
            pl.BlockSpec((s, kd), lambda b, gi, i: (b, gi)),
            pl.BlockSpec((1, V_AUG_ROWS, s), lambda b, gi, i: (gi, 0, b)),
            pl.BlockSpec((1, 1, n_selp, tk), lambda b, gi, i: (b, gi, 0, i)),
            gl_spec, gb_spec,
        ],
        out_specs=pl.BlockSpec((NSA_HPG, d, tk), lambda b, gi, i: (gi, 0, b * nq + i)),
        out_shape=jax.ShapeDtypeStruct((NSA_HEADS, d, bsz * s), BF16),
        scratch_shapes=[pltpu.VMEM((kd, w), BF16), pltpu.VMEM((kd, w), BF16),
                        pltpu.VMEM((tk, w), F32), pltpu.VMEM((tk, w), F32),
                        pltpu.VMEM((1, w), F32), pltpu.VMEM((1, w), F32),
                        pltpu.VMEM((1, w), F32), pltpu.VMEM((1, w), F32),
                        pltpu.VMEM((V_AUG_ROWS, w), F32), pltpu.VMEM((V_AUG_ROWS, w), F32)],
        compiler_params=_params(("parallel", "parallel", "arbitrary")),
        name="sel_attn",
    )(q_t, ka, va, selb, gl_t, gbias)


def _even_layer(h2d, bsz, s, gain, w_in, ssd_conv_w, ssd_conv_b, dt_bias, a_log, d_skip, ssd_norm,
                conf_conv_w, conf_conv_b, conf_ln_g, conf_ln_b, w_out):
    o1 = SSD_INNER
    o2 = o1 + SSD_XBC
    o3 = o2 + SSD_HEADS
    o4 = o3 + 2 * CONF_WIDTH
    w_dt = w_in[:, o2:o3]
    w_nat = jnp.concatenate(
        [w_in[:, :o2], w_in[:, o3:], jnp.pad(w_dt, ((0, 0), (0, LANES - SSD_HEADS)))], axis=1).astype(BF16)
    nat_spec = [(SSD_INNER, F32), (SSD_XBC, F32), (2 * CONF_WIDTH, F32), (CONF_WIDTH, F32), (LANES, F32)]
    tm = min(PROJ_ROWS, s)
    z, xbc, glu, zc, dt_nat, dt_t = _norm_proj(
        h2d, gain, w_nat, nat_spec, w_dt.T.astype(BF16), jnp.zeros((SSD_HEADS,), F32),
        [(SSD_HEADS, F32)], tm)
    y_a = _ssd(xbc, z, dt_nat, dt_t, ssd_conv_w, ssd_conv_b, dt_bias, a_log, d_skip, ssd_norm, bsz, s)
    y_b = _conformer(glu, zc, conf_conv_w, conf_conv_b, conf_ln_g, conf_ln_b, bsz, s)
    return _outproj_even(h2d, y_a.reshape(bsz * s, SSD_INNER), y_b.reshape(bsz * s, CONF_WIDTH), w_out, tm)


def _odd_layer(h2d, bsz, s, gain, w_in, gate_bias, pe_k, w1_k, w2_k, pe_v, w1_v, w2_v, w_out,
               final_gain, final):
    g, r, d = NSA_KV_GROUPS, NSA_HPG, NSA_HEAD_DIM
    t_tokens = bsz * s
    sizes = [NSA_WIDTH] + [KV_WIDTH] * 6 + [3 * NSA_HEADS, NSA_WIDTH]
    offs = np.cumsum([0] + sizes)
    col = lambda k: w_in[:, offs[k]:offs[k + 1]]
    w_q, w_kc, w_vc, w_ks, w_vs, w_kw, w_vw, w_gl, w_z = [col(k) for k in range(9)]
    perm = np.array([[[(gi * r + ri) * 3 + br for ri in range(r)] for br in range(3)] for gi in range(g)])
    w_gl_g = jnp.pad(w_gl[:, perm.reshape(-1)].reshape(D_MODEL, g * 3, r),
                     ((0, 0), (0, 0), (0, SUBLANES - r))).reshape(D_MODEL, g * 3 * SUBLANES)
    gbias = jnp.pad(gate_bias[perm.reshape(-1)].reshape(g * 3, r), ((0, 0), (0, SUBLANES - r)))
    gbias = gbias.reshape(g, 3, SUBLANES, 1)
    scale = NSA_HEAD_DIM ** -0.5
    assert d + _sel_pad(s // SEL_BLOCK) == LANES

    def key_slots(w):
        return jnp.pad(w.reshape(D_MODEL, g, d), ((0, 0), (0, 0), (0, LANES - d))).reshape(D_MODEL, g * LANES)

    def value_rows(w):
        return jnp.pad(w.reshape(D_MODEL, g, d), ((0, 0), (0, 0), (0, V_AUG_ROWS - d))).reshape(D_MODEL, g * V_AUG_ROWS)

    ones_row = jnp.zeros((g, V_AUG_ROWS), F32).at[:, d].set(1.0).reshape(-1)
    w_nat = jnp.concatenate([w_kc, w_vc, key_slots(w_ks), key_slots(w_kw)], axis=1).astype(BF16)
    w_t = jnp.concatenate([w_q * scale, value_rows(w_vs), value_rows(w_vw), w_gl_g, w_z], axis=1).T.astype(BF16)
    t_bias = jnp.concatenate([jnp.zeros((NSA_WIDTH,), F32), ones_row, ones_row,
                              jnp.zeros((g * 3 * SUBLANES + NSA_WIDTH,), F32)])
    nat_spec = [(KV_WIDTH, F32), (KV_WIDTH, F32), (g * LANES, BF16), (g * LANES, BF16)]
    t_spec = [(NSA_WIDTH, BF16), (g * V_AUG_ROWS, BF16), (g * V_AUG_ROWS, BF16),
              (g * 3 * SUBLANES, F32), (NSA_WIDTH, F32)]
    tm = min(PROJ_ROWS, s)
    kc, vc, ks_a, kw_a, q_t, vs_a, vw_a, gl_t, z_t = _norm_proj(
        h2d, gain, w_nat, nat_spec, w_t, t_bias, t_spec, tm, onehot_out=2, regroup_outs=(0, 1), seq_len=s)

    q_t = q_t.reshape(NSA_HEADS, d, t_tokens)
    gl_t = gl_t.reshape(g, 3, SUBLANES, t_tokens)
    vs_a = vs_a.reshape(g, V_AUG_ROWS, t_tokens)
    vw_a = vw_a.reshape(g, V_AUG_ROWS, t_tokens)

    kcmp, vcmp_t = _compress(kc, vc, pe_k, w1_k, w2_k, pe_v, w1_v, w2_v, bsz, s)
    o_cw, selb = _cmp_win_attn(q_t, kcmp, vcmp_t, kw_a, vw_a, gl_t, gbias, bsz, s)
    o_sel = _sel_attn(q_t, ks_a, vs_a, selb, gl_t, gbias, bsz, s)
    flat = lambda o: o.reshape(NSA_WIDTH, t_tokens)
    return _outproj_odd(h2d, flat(o_cw), flat(o_sel), z_t, w_out, final_gain, final, tm)


def kernel(x, e_norm, e_w_in, e_ssd_conv_w, e_ssd_conv_b, e_dt_bias, e_a_log, e_d_skip, e_ssd_norm,
           e_conf_conv_w, e_conf_conv_b, e_conf_ln_g, e_conf_ln_b, e_w_out, o_norm, o_w_in, o_gate_bias,
           o_cmp_pe_k, o_cmp_w1_k, o_cmp_w2_k, o_cmp_pe_v, o_cmp_w1_v, o_cmp_w2_v, o_w_out, final_norm):
    bsz, s, d = x.shape
    depth = e_norm.shape[0] + o_norm.shape[0]
    h = x.reshape(bsz * s, d)
    for layer in range(depth):
        i = layer // 2
        if layer % 2 == 0:
            h = _even_layer(h, bsz, s, e_norm[i], e_w_in[i], e_ssd_conv_w[i], e_ssd_conv_b[i], e_dt_bias[i],
                            e_a_log[i], e_d_skip[i], e_ssd_norm[i], e_conf_conv_w[i], e_conf_conv_b[i],
                            e_conf_ln_g[i], e_conf_ln_b[i], e_w_out[i])
        else:
            h = _odd_layer(h, bsz, s, o_norm[i], o_w_in[i], o_gate_bias[i], o_cmp_pe_k[i], o_cmp_w1_k[i],
                           o_cmp_w2_k[i], o_cmp_pe_v[i], o_cmp_w1_v[i], o_cmp_w2_v[i], o_w_out[i],
                           final_norm, layer == depth - 1)
    return h.reshape(bsz, s, d)
```
